```python
import math
import jax, jax.numpy as jnp
from jax import lax
import numpy as np


D_MODEL = 1024
BATCH = 32
SEQ = 2048
DEPTH = 1

HEAD_DIM = 64
RWKV_HEADS = 8
RWKV_WIDTH = RWKV_HEADS * HEAD_DIM
DECAY_LORA = 64
ICLR_LORA = 64
GATE_LORA = 128
GROUPNORM_EPS = 64e-5
ATTN_GROUPS = ((128, 1), (512, 4), (2048, 16))
HEADS_PER_GROUP = 4
ATTN_HEADS = HEADS_PER_GROUP * len(ATTN_GROUPS)
ATTN_WIDTH = ATTN_HEADS * HEAD_DIM
ATTN_OUT_WIDTH = HEADS_PER_GROUP * HEAD_DIM
ATTN_BLOCK = 128
N_BUCKETS = 32
MAX_DISTANCE = 2048
N_EXPERTS = 32
TOP_K = 4
D_EXPERT = D_MODEL
SWIGLU_ALPHA = 1.702
SWIGLU_LIMIT = 7.0
MOE_BLOCK = 512
NORM_EPS = 1e-6
NEG_INF = -1e30
IN_SPLITS = (RWKV_WIDTH, 2 * RWKV_WIDTH, 3 * RWKV_WIDTH,
             3 * RWKV_WIDTH + ATTN_WIDTH, 3 * RWKV_WIDTH + 2 * ATTN_WIDTH,
             3 * RWKV_WIDTH + 3 * ATTN_WIDTH, 3 * RWKV_WIDTH + 3 * ATTN_WIDTH + D_MODEL)
IN_COLS = IN_SPLITS[-1] + D_MODEL

kernel_name = 'hybrid_rwkv7_dilated_attn_moe_block'


def rms_norm(x, g):
    xf = x.astype(jnp.float32)
    y = xf * lax.rsqrt(jnp.mean(xf * xf, axis=-1, keepdims=True) + NORM_EPS)
    return (y * g).astype(x.dtype)


def token_shift(x):
    return jnp.pad(x, ((0, 0), (1, 0), (0, 0)))[:, :-1]


def wkv7_scan(r, decay, k, v, a_vec, b_vec):
    Bsz, S, H, N = r.shape
    xs = tuple(t.transpose(1, 0, 2, 3) for t in (r, decay, k, v, a_vec, b_vec))

    def step(state, inp):
        r_t, w_t, k_t, v_t, a_t, b_t = inp
        sa = jnp.einsum('bhvk,bhk->bhv', state, a_t)
        state = (state * w_t[:, :, None, :] + sa[..., None] * b_t[:, :, None, :]
                 + v_t[..., None] * k_t[:, :, None, :])
        return state, jnp.einsum('bhvk,bhk->bhv', state, r_t)

    _, ys = lax.scan(step, jnp.zeros((Bsz, H, N, N), jnp.float32), xs)
    return ys.transpose(1, 0, 2, 3)


def rwkv7_time_mix(h, r, k, v, mu_rkv, mu_wag, w0, w1, w2, a0, a1, a2, g1, g2,
                   k_k, k_a, r_k, ln_w, ln_b):
    Bsz, S, _ = h.shape
    f32 = jnp.float32
    h = h.astype(f32)
    r, k, v = r.astype(f32), k.astype(f32), v.astype(f32)
    dh = token_shift(h) - h
    xw, xa, xg = h + dh * mu_wag[0], h + dh * mu_wag[1], h + dh * mu_wag[2]
    r = r + (token_shift(r) - r) * mu_rkv[0]
    k = k + (token_shift(k) - k) * mu_rkv[1]
    v = v + (token_shift(v) - v) * mu_rkv[2]
    w_log = -jax.nn.softplus(-(w0 + jnp.tanh(xw @ w1) @ w2)) - 0.5
    decay = jnp.exp(-jnp.exp(w_log))
    a = jax.nn.sigmoid(a0 + (xa @ a1) @ a2)
    g = jax.nn.sigmoid(xg @ g1) @ g2
    hs = lambda t: t.reshape(Bsz, S, RWKV_HEADS, HEAD_DIM)
    kk = hs(k * k_k)
    kk = kk / jnp.maximum(jnp.linalg.norm(kk, axis=-1, keepdims=True), 1e-12)
    k = k * (1.0 + (a - 1.0) * k_a)
    rh, kh, vh, ah = hs(r), hs(k), hs(v), hs(a)
    y = wkv7_scan(rh, hs(decay), kh, vh, -kk, kk * ah)
    mu = jnp.mean(y, axis=-1, keepdims=True)
    var = jnp.mean(jnp.square(y - mu), axis=-1, keepdims=True)
    y = ((y - mu) * lax.rsqrt(var + GROUPNORM_EPS)).reshape(Bsz, S, RWKV_WIDTH) * ln_w + ln_b
    bonus = jnp.sum(rh * kh * r_k, axis=-1, keepdims=True) * vh
    y = (y + bonus.reshape(Bsz, S, RWKV_WIDTH)) * g
    return y


def t5_bucket(dist):
    max_exact = N_BUCKETS // 2
    large = max_exact + (jnp.log(jnp.maximum(dist, max_exact).astype(jnp.float32) / max_exact)
                         / math.log(MAX_DISTANCE / max_exact) * (N_BUCKETS - max_exact)).astype(jnp.int32)
    return jnp.where(dist < max_exact, dist, jnp.minimum(large, N_BUCKETS - 1))


def dilated_group_attention(q, k, v, bias_table, window, dilation):
    Bsz, S, H, Dh = q.shape
    L = S // dilation
    nb = -(-L // ATTN_BLOCK)
    Lp = nb * ATTN_BLOCK

    def to_blocks(t):
        t = t.reshape(Bsz, L, dilation, H, Dh).transpose(0, 2, 1, 3, 4)
        t = jnp.pad(t, ((0, 0), (0, 0), (0, Lp - L), (0, 0), (0, 0)))
        return t.reshape(Bsz, dilation, nb, ATTN_BLOCK, H, Dh)

    def with_prev(t):
        prev = jnp.pad(t, ((0, 0), (0, 0), (1, 0), (0, 0), (0, 0), (0, 0)))[:, :, :-1]
        return jnp.concatenate([prev, t], axis=3)

    def from_blocks(t):
        t = t.reshape((Bsz, dilation, Lp) + t.shape[4:])[:, :, :L]
        t = jnp.moveaxis(t, 1, 2)
        return t.reshape((Bsz, S) + t.shape[3:])

    qb = to_blocks(q)
    kw, vw = with_prev(to_blocks(k)), with_prev(to_blocks(v))
    qi = jnp.arange(ATTN_BLOCK)[:, None]
    kj = jnp.arange(2 * ATTN_BLOCK)[None, :]
    steps = qi - kj + ATTN_BLOCK
    bias = jnp.transpose(bias_table[t5_bucket(jnp.maximum(steps, 0) * dilation)], (2, 0, 1))
    band = (steps >= 0) & (steps <= window // dilation)
    valid = band[None] & ((jnp.arange(nb)[:, None, None] > 0) | (kj >= ATTN_BLOCK)[None])
    s = jnp.einsum('bznqhd,bznkhd->bznhqk', qb, kw) + bias.astype(jnp.float32)[None, None, None]
    s = jnp.where(valid[None, None, :, None], s, NEG_INF)
    m = jnp.max(s, axis=-1, keepdims=True)
    p = jnp.exp(s - m)
    den = jnp.sum(p, axis=-1, keepdims=True)
    o = jnp.einsum('bznhqk,bznkhd->bznqhd', p, vw) / jnp.transpose(den, (0, 1, 2, 4, 3, 5))
    lse = jnp.transpose((m + jnp.log(den))[..., 0], (0, 1, 2, 4, 3))
    return from_blocks(o), from_blocks(lse)


def dilated_attention(q, k, v, qn_g, kn_g, rel_bias):
    Bsz, S, _ = q.shape
    hs = lambda t: t.astype(jnp.float32).reshape(Bsz, S, ATTN_HEADS, HEAD_DIM)
    q = rms_norm(hs(q), qn_g) * (HEAD_DIM ** -0.5)
    k = rms_norm(hs(k), kn_g)
    v = hs(v)
    outs, lses = [], []
    for gi, (window, dilation) in enumerate(ATTN_GROUPS):
        sl = slice(gi * HEADS_PER_GROUP, (gi + 1) * HEADS_PER_GROUP)
        o, lse = dilated_group_attention(q[:, :, sl], k[:, :, sl], v[:, :, sl], rel_bias[:, sl],
                                         window, dilation)
        outs.append(o)
        lses.append(lse)
    alpha = jax.nn.softmax(jnp.stack(lses, axis=0), axis=0)
    o = jnp.sum(alpha[..., None] * jnp.stack(outs, axis=0), axis=0)
    return o.reshape(Bsz, S, ATTN_OUT_WIDTH)


def clamped_swiglu(u):
    x_glu, x_lin = u[..., ::2], u[..., 1::2]
    x_glu = jnp.minimum(x_glu, SWIGLU_LIMIT)
    x_lin = jnp.clip(x_lin, -SWIGLU_LIMIT, SWIGLU_LIMIT)
    return x_glu * jax.nn.sigmoid(SWIGLU_ALPHA * x_glu) * (x_lin + 1.0)


def moe_ffn(h, router_w, router_b, w1, b1, w2, b2):
    Bsz, S, D = h.shape
    N = Bsz * S
    hf = h.reshape(N, D)
    logits = (hf @ router_w + router_b).astype(jnp.float32)
    top_v, top_i = lax.top_k(logits, TOP_K)
    gates = jax.nn.softmax(top_v, axis=-1)
    A = N * TOP_K
    flat_e = top_i.reshape(A)
    flat_t = jnp.arange(A, dtype=jnp.int32) // TOP_K
    flat_g = gates.reshape(A)
    order = jnp.argsort(flat_e)
    se, st, sg = flat_e[order], flat_t[order], flat_g[order]
    counts = jnp.bincount(flat_e, length=N_EXPERTS)
    padded = (counts + MOE_BLOCK - 1) // MOE_BLOCK * MOE_BLOCK
    start = jnp.cumsum(counts) - counts
    pend = jnp.cumsum(padded)
    pstart = pend - padded
    dest = pstart[se] + jnp.arange(A, dtype=jnp.int32) - start[se]
    n_blocks = -(-A // MOE_BLOCK) + N_EXPERTS
    rows = n_blocks * MOE_BLOCK
    tok_buf = jnp.zeros((rows,), jnp.int32).at[dest].set(st)
    gate_buf = jnp.zeros((rows,), jnp.float32).at[dest].set(sg)
    blk_e = jnp.minimum(jnp.searchsorted(pend, jnp.arange(n_blocks, dtype=jnp.int32) * MOE_BLOCK,
                                         side='right'), N_EXPERTS - 1)

    def block(acc, inp):
        tok, g, e = inp
        u = hf[tok] @ w1[e] + b1[e]
        o = clamped_swiglu(u) @ w2[e] + b2[e]
        return acc.at[tok].add(o.astype(jnp.float32) * g[:, None]), None

    acc, _ = lax.scan(block, jnp.zeros((N, D), jnp.float32),
                      (tok_buf.reshape(n_blocks, MOE_BLOCK), gate_buf.reshape(n_blocks, MOE_BLOCK), blk_e))
    return acc.reshape(Bsz, S, D).astype(h.dtype)


def hybrid_layer(x, c, ada_w, ada_b, norm1_g, norm2_g, w_in, rwkv_mu_rkv, rwkv_mu_wag,
                 rwkv_w0, rwkv_w1, rwkv_w2, rwkv_a0, rwkv_a1, rwkv_a2, rwkv_g1, rwkv_g2,
                 rwkv_k_k, rwkv_k_a, rwkv_r_k, rwkv_ln_w, rwkv_ln_b, attn_qn_g, attn_kn_g,
                 rel_bias, w_br_rwkv, w_br_attn, w_out, router_w, router_b,
                 exp_w1, exp_b1, exp_w2, exp_b2):
    mod = jax.nn.silu(c) @ ada_w + ada_b
    shift1, scale1, gate1, shift2, scale2, gate2 = [m[:, None, :] for m in jnp.split(mod, 6, axis=-1)]
    h = rms_norm(x, norm1_g) * (1.0 + scale1) + shift1
    proj = h @ w_in
    r, k, v, qa, ka, va, gr, ga = jnp.split(proj, IN_SPLITS, axis=-1)
    y_rwkv = rwkv7_time_mix(h, r, k, v, rwkv_mu_rkv, rwkv_mu_wag, rwkv_w0, rwkv_w1, rwkv_w2,
                            rwkv_a0, rwkv_a1, rwkv_a2, rwkv_g1, rwkv_g2, rwkv_k_k, rwkv_k_a,
                            rwkv_r_k, rwkv_ln_w, rwkv_ln_b).astype(x.dtype) @ w_br_rwkv
    y_attn = dilated_attention(qa, ka, va, attn_qn_g, attn_kn_g, rel_bias).astype(x.dtype) @ w_br_attn
    mixed = jax.nn.sigmoid(gr) * y_rwkv + jax.nn.sigmoid(ga) * y_attn
    x = x + gate1 * (mixed @ w_out)
    h2 = rms_norm(x, norm2_g) * (1.0 + scale2) + shift2
    x = x + gate2 * moe_ffn(h2, router_w, router_b, exp_w1, exp_b1, exp_w2, exp_b2)
    return x


def setup_inputs(seed: int = 0) -> dict:
    key = jax.random.key(seed)
    ks = iter(jax.random.split(key, 40))
    f32 = jnp.float32
    nrm = lambda shape, scale: jax.random.normal(next(ks), shape, f32) * scale
    uni = lambda shape, lo, hi: jax.random.uniform(next(ks), shape, f32, lo, hi)
    L, D, RW = DEPTH, D_MODEL, RWKV_WIDTH
    return {
        'x': nrm((BATCH, SEQ, D), 1.0),
        'c': nrm((BATCH, D), 1.0),
        'ada_w': nrm((L, D, 6 * D), D ** -0.5),
        'ada_b': nrm((L, 6 * D), 0.02),
        'norm1_g': 1.0 + nrm((L, D), 0.02),
        'norm2_g': 1.0 + nrm((L, D), 0.02),
        'w_in': nrm((L, D, IN_COLS), D ** -0.5),
        'rwkv_mu_rkv': uni((L, 3, RW), 0.0, 1.0),
        'rwkv_mu_wag': uni((L, 3, D), 0.0, 1.0),
        'rwkv_w0': uni((L, RW), -6.0, 1.0),
        'rwkv_w1': nrm((L, D, DECAY_LORA), D ** -0.5),
        'rwkv_w2': nrm((L, DECAY_LORA, RW), DECAY_LORA ** -0.5),
        'rwkv_a0': nrm((L, RW), 0.1),
        'rwkv_a1': nrm((L, D, ICLR_LORA), D ** -0.5),
        'rwkv_a2': nrm((L, ICLR_LORA, RW), ICLR_LORA ** -0.5),
        'rwkv_g1': nrm((L, D, GATE_LORA), D ** -0.5),
        'rwkv_g2': nrm((L, GATE_LORA, RW), GATE_LORA ** -0.5),
        'rwkv_k_k': 0.85 + nrm((L, RW), 0.02),
        'rwkv_k_a': 1.0 + nrm((L, RW), 0.02),
        'rwkv_r_k': nrm((L, RWKV_HEADS, HEAD_DIM), 0.1),
        'rwkv_ln_w': 1.0 + nrm((L, RW), 0.02),
        'rwkv_ln_b': nrm((L, RW), 0.02),
        'attn_qn_g': 1.0 + nrm((L, HEAD_DIM), 0.02),
        'attn_kn_g': 1.0 + nrm((L, HEAD_DIM), 0.02),
        'rel_bias': nrm((N_BUCKETS, ATTN_HEADS), 0.5),
        'w_br_rwkv': nrm((L, RW, D), RW ** -0.5),
        'w_br_attn': nrm((L, ATTN_OUT_WIDTH, D), ATTN_OUT_WIDTH ** -0.5),
        'w_out': nrm((L, D, D), D ** -0.5),
        'router_w': nrm((L, D, N_EXPERTS), D ** -0.5),
        'router_b': nrm((L, N_EXPERTS), 0.01),
        'exp_w1': nrm((L, N_EXPERTS, D, 2 * D_EXPERT), D ** -0.5),
        'exp_b1': nrm((L, N_EXPERTS, 2 * D_EXPERT), 0.02),
        'exp_w2': nrm((L, N_EXPERTS, D_EXPERT, D), D_EXPERT ** -0.5),
        'exp_b2': nrm((L, N_EXPERTS, D), 0.02),
    }


def reference(x, c, ada_w, ada_b, norm1_g, norm2_g, w_in, rwkv_mu_rkv, rwkv_mu_wag,
              rwkv_w0, rwkv_w1, rwkv_w2, rwkv_a0, rwkv_a1, rwkv_a2, rwkv_g1, rwkv_g2,
              rwkv_k_k, rwkv_k_a, rwkv_r_k, rwkv_ln_w, rwkv_ln_b, attn_qn_g, attn_kn_g,
              rel_bias, w_br_rwkv, w_br_attn, w_out, router_w, router_b,
              exp_w1, exp_b1, exp_w2, exp_b2):
    for l in range(DEPTH):
        x = hybrid_layer(x, c, ada_w[l], ada_b[l], norm1_g[l], norm2_g[l], w_in[l],
                         rwkv_mu_rkv[l], rwkv_mu_wag[l], rwkv_w0[l], rwkv_w1[l], rwkv_w2[l],
                         rwkv_a0[l], rwkv_a1[l], rwkv_a2[l], rwkv_g1[l], rwkv_g2[l],
                         rwkv_k_k[l], rwkv_k_a[l], rwkv_r_k[l], rwkv_ln_w[l], rwkv_ln_b[l],
                         attn_qn_g[l], attn_kn_g[l], rel_bias, w_br_rwkv[l], w_br_attn[l],
                         w_out[l], router_w[l], router_b[l], exp_w1[l], exp_b1[l],
                         exp_w2[l], exp_b2[l])
    return x
```

```python
import functools
import math

import jax
import jax.numpy as jnp
from jax import lax
from jax.experimental import pallas as pl
from jax.experimental.pallas import tpu as pltpu

F32 = jnp.float32
BF16 = jnp.bfloat16

D_MODEL = 1024
HEAD_DIM = 64
RWKV_HEADS = 8
RWKV_WIDTH = RWKV_HEADS * HEAD_DIM
DECAY_LORA = 64
ICLR_LORA = 64
GATE_LORA = 128
LORA_WIDTH = DECAY_LORA + ICLR_LORA + GATE_LORA
GROUPNORM_EPS = 64e-5
ATTN_GROUPS = ((128, 1), (512, 4), (2048, 16))
HEADS_PER_GROUP = 4
ATTN_HEADS = HEADS_PER_GROUP * len(ATTN_GROUPS)
ATTN_WIDTH = ATTN_HEADS * HEAD_DIM
GROUP_WIDTH = HEADS_PER_GROUP * HEAD_DIM
ATTN_BLOCK = 128
N_BUCKETS = 32
MAX_DISTANCE = 2048
N_EXPERTS = 32
TOP_K = 4
D_EXPERT = D_MODEL
SWIGLU_ALPHA = 1.702
SWIGLU_LIMIT = 7.0
NORM_EPS = 1e-6
NEG_INF = -1e30

RWKV_COLS = 3 * RWKV_WIDTH + 2 * LORA_WIDTH
ATTN_COLS = 3 * ATTN_WIDTH
GATE_COLS = 2 * D_MODEL

CHUNK = 64
TOKEN_TILE = 512
MOE_ROWS = 512
VMEM_LIMIT = 56 * 1024 * 1024


def _cparams(sem, vmem=VMEM_LIMIT):
    return pltpu.CompilerParams(dimension_semantics=sem, vmem_limit_bytes=vmem)


def _split2(a):
    hi = a.astype(BF16)
    lo = (a - hi.astype(F32)).astype(BF16)
    return hi, lo


def _split3(a):
    hi = a.astype(BF16)
    r = a - hi.astype(F32)
    mid = r.astype(BF16)
    lo = (r - mid.astype(F32)).astype(BF16)
    return hi, mid, lo


def _dot(a, b):
    return jnp.dot(a, b, preferred_element_type=F32)


def _dot_nt(a, b):
    return lax.dot_general(a, b, (((1,), (1,)), ((), ())), preferred_element_type=F32)


def _dot_tn(a, b):
    return lax.dot_general(a, b, (((0,), (0,)), ((), ())), preferred_element_type=F32)


def _sigmoid(x):
    return 1.0 / (1.0 + jnp.exp(-x))


def _mod_kernel(c_ref, w_ref, b_ref, o_ref):
    c = c_ref[...]
    s = c * _sigmoid(c)
    s_hi, s_lo = _split2(s)
    w_hi, w_lo = _split2(w_ref[...])
    o_ref[...] = _dot(s_hi, w_hi) + _dot(s_hi, w_lo) + _dot(s_lo, w_hi) + b_ref[...]


def _modulation(c, ada_w, ada_b):
    bsz, d = c.shape
    cols = ada_w.shape[1]
    tn = 1024
    return pl.pallas_call(
        _mod_kernel,
        grid=(cols // tn,),
        in_specs=[pl.BlockSpec((bsz, d), lambda j: (0, 0)),
                  pl.BlockSpec((d, tn), lambda j: (0, j)),
                  pl.BlockSpec((1, tn), lambda j: (0, j))],
        out_specs=pl.BlockSpec((bsz, tn), lambda j: (0, j)),
        out_shape=jax.ShapeDtypeStruct((bsz, cols), F32),
        compiler_params=_cparams(("arbitrary",)),
        name="mod",
    )(c, ada_w, ada_b.reshape(1, cols))


def _inproj_kernel(x_ref, g_ref, scale_ref, shift_ref, wr_ref, wa_ref, wg_ref,
                   pr_ref, pa_ref, pg_ref):
    x = x_ref[...]
    ms = jnp.mean(x * x, axis=-1, keepdims=True)
    h = x * lax.rsqrt(ms + NORM_EPS) * (g_ref[...] * (1.0 + scale_ref[0])) + shift_ref[0]
    hb = h.astype(BF16)
    step = 512
    for w_ref, o_ref in ((wr_ref, pr_ref), (wa_ref, pa_ref), (wg_ref, pg_ref)):
        cols = w_ref.shape[1]
        for j in range(0, cols, step):
            width = min(step, cols - j)
            o_ref[:, j:j + width] = _dot(hb, w_ref[:, j:j + width]).astype(BF16)


def _inproj(x2, norm_g, scale, shift, w_r, w_a, w_g, seq):
    n, d = x2.shape
    tm = TOKEN_TILE
    per = seq // tm
    const = lambda shape: pl.BlockSpec(shape, lambda i: (0, 0), pipeline_mode=pl.Buffered(1))
    return pl.pallas_call(
        _inproj_kernel,
        grid=(n // tm,),
        in_specs=[pl.BlockSpec((tm, d), lambda i: (i, 0)),
                  pl.BlockSpec((1, d), lambda i: (0, 0)),
                  pl.BlockSpec((1, 1, d), lambda i: (i // per, 0, 0)),
                  pl.BlockSpec((1, 1, d), lambda i: (i // per, 0, 0)),
                  const(w_r.shape), const(w_a.shape), const(w_g.shape)],
        out_specs=[pl.BlockSpec((tm, RWKV_COLS), lambda i: (i, 0)),
                   pl.BlockSpec((tm, ATTN_COLS), lambda i: (i, 0)),
                   pl.BlockSpec((tm, GATE_COLS), lambda i: (i, 0))],
        out_shape=[jax.ShapeDtypeStruct((n, RWKV_COLS), BF16),
                   jax.ShapeDtypeStruct((n, ATTN_COLS), BF16),
                   jax.ShapeDtypeStruct((n, GATE_COLS), BF16)],
        compiler_params=_cparams(("parallel",)),
        name="inproj",
    )(x2, norm_g, scale, shift, w_r, w_a, w_g)


_P_W0, _P_A0, _P_KK, _P_KA, _P_RK, _P_LNW, _P_LNB, _P_MUR, _P_MUK, _P_MUV = range(10)


def _rwkv_kernel(pr_ref, par_ref, wl_ref, ones_ref, o_ref, state_ref, prev_ref, y_ref):
    c_idx = pl.program_id(1)

    @pl.when(c_idx == 0)
    def _():
        state_ref[...] = jnp.zeros_like(state_ref)
        prev_ref[...] = jnp.zeros_like(prev_ref)

    rw = RWKV_WIDTH
    par = lambda i: par_ref[i:i + 1, :]
    ones_bd = ones_ref[...]

    def headsum(a):
        hi, lo = _split2(a)
        return _dot(hi, ones_bd) + _dot(lo, ones_bd)

    pr = pr_ref[...].astype(F32)
    row = lax.broadcasted_iota(jnp.int32, (CHUNK, 1), 0)
    prev_row = prev_ref[0:1, :]

    def shifted(lo, hi):
        return jnp.where(row == 0, prev_row[:, lo:hi], pltpu.roll(pr[:, lo:hi], 1, 0))

    sh_rkv = shifted(0, 3 * rw)
    sh_lora = shifted(3 * rw + LORA_WIDTH, 3 * rw + 2 * LORA_WIDTH)
    prev_ref[0:1, :] = pr[CHUNK - 1:CHUNK, :]

    lora = pr[:, 3 * rw:3 * rw + LORA_WIDTH] + sh_lora
    lane = lax.broadcasted_iota(jnp.int32, (1, LORA_WIDTH), 1)
    act = jnp.where(lane < DECAY_LORA, jnp.tanh(lora),
                    jnp.where(lane < DECAY_LORA + ICLR_LORA, lora, _sigmoid(lora)))
    lo_out = _dot(act.astype(BF16), wl_ref[...])
    w_pre = par(_P_W0) + lo_out[:, :rw]
    z = -w_pre
    softplus = jnp.maximum(z, 0.0) + jnp.log(1.0 + jnp.exp(-jnp.abs(z)))
    logd = -jnp.exp(-softplus - 0.5)
    a = _sigmoid(par(_P_A0) + lo_out[:, rw:2 * rw])
    g = lo_out[:, 2 * rw:3 * rw]

    r0, k0, v0 = pr[:, :rw], pr[:, rw:2 * rw], pr[:, 2 * rw:3 * rw]
    r = r0 + (sh_rkv[:, :rw] - r0) * par(_P_MUR)
    k = k0 + (sh_rkv[:, rw:2 * rw] - k0) * par(_P_MUK)
    v = v0 + (sh_rkv[:, 2 * rw:] - v0) * par(_P_MUV)

    kk = k * par(_P_KK)
    kk = kk / jnp.maximum(jnp.sqrt(headsum(kk * kk)), 1e-12)
    k2 = k * (1.0 + (a - 1.0) * par(_P_KA))

    ti = lax.broadcasted_iota(jnp.int32, (CHUNK, CHUNK), 0)
    tj = lax.broadcasted_iota(jnp.int32, (CHUNK, CHUNK), 1)
    low_incl = ti >= tj
    low_strict = ti > tj
    tril = jnp.where(low_incl, 1.0, 0.0).astype(BF16)
    d_hi, d_mid, d_lo = _split3(logd)
    cum = _dot(tril, d_hi) + _dot(tril, d_mid) + _dot(tril, d_lo)
    p_incl = jnp.exp(cum)
    p_excl = jnp.exp(cum - logd)
    p_inv = jnp.exp(-cum)

    a_t = (-kk * p_excl).astype(BF16)
    b_t = (kk * a * p_inv).astype(BF16)
    k_t = (k2 * p_inv).astype(BF16)
    r_t = (r * p_incl).astype(BF16)
    v_b = v.astype(BF16)
    eye = jnp.where(ti == tj, 1.0, 0.0)

    for h in range(RWKV_HEADS):
        sl = slice(h * HEAD_DIM, (h + 1) * HEAD_DIM)
        ar = jnp.concatenate([a_t[:, sl], r_t[:, sl]], axis=0)
        bk = jnp.concatenate([b_t[:, sl], k_t[:, sl]], axis=0)
        m = _dot_nt(ar, bk)
        a_ab = jnp.where(low_strict, m[:CHUNK, :CHUNK], 0.0)
        a_ak = jnp.where(low_strict, m[:CHUNK, CHUNK:], 0.0)
        a_rb = jnp.where(low_incl, m[CHUNK:, :CHUNK], 0.0)
        a_rk = jnp.where(low_incl, m[CHUNK:, CHUNK:], 0.0)
        t_inv = eye + a_ab
        pw = a_ab
        for _ in range(int(math.log2(CHUNK)) - 1):
            pwb = pw.astype(BF16)
            pw = _dot(pwb, pwb)
            t_inv = t_inv + _dot(pw.astype(BF16), t_inv.astype(BF16))

        s0 = state_ref[h]
        ars = _dot_nt(ar, s0.astype(BF16))
        vh = v_b[:, sl]
        rhs = ars[:CHUNK] + _dot(a_ak.astype(BF16), vh)
        u = _dot(t_inv.astype(BF16), rhs.astype(BF16))
        ub = u.astype(BF16)
        y = ars[CHUNK:] + _dot(a_rb.astype(BF16), ub) + _dot(a_rk.astype(BF16), vh)
        y_ref[:, sl] = y
        uv = jnp.concatenate([ub, vh], axis=0)
        ds = _dot_tn(uv, bk)
        state_ref[h] = (s0 + ds) * p_incl[CHUNK - 1:CHUNK, sl]

    y = y_ref[...]
    inv_n = 1.0 / HEAD_DIM
    mu = headsum(y) * inv_n
    dlt = y - mu
    var = headsum(dlt * dlt) * inv_n
    yn = dlt * lax.rsqrt(var + GROUPNORM_EPS) * par(_P_LNW) + par(_P_LNB)
    bonus = headsum(r * k2 * par(_P_RK)) * v
    o_ref[...] = ((yn + bonus) * g).astype(BF16)


def _rwkv(pr, par, w_lora, ones_bd, bsz, seq):
    n = pr.shape[0]
    nchunk = seq // CHUNK
    const = lambda shape: pl.BlockSpec(shape, lambda b, c: (0, 0))
    return pl.pallas_call(
        _rwkv_kernel,
        grid=(bsz, nchunk),
        in_specs=[pl.BlockSpec((CHUNK, RWKV_COLS), lambda b, c: (b * nchunk + c, 0)),
                  const(par.shape), const(w_lora.shape), const(ones_bd.shape)],
        out_specs=pl.BlockSpec((CHUNK, RWKV_WIDTH), lambda b, c: (b * nchunk + c, 0)),
        out_shape=jax.ShapeDtypeStruct((n, RWKV_WIDTH), BF16),
        scratch_shapes=[pltpu.VMEM((RWKV_HEADS, HEAD_DIM, HEAD_DIM), F32),
                        pltpu.VMEM((8, RWKV_COLS), F32),
                        pltpu.VMEM((CHUNK, RWKV_WIDTH), F32)],
        compiler_params=_cparams(("parallel", "arbitrary")),
        name="rwkv",
    )(pr, par, w_lora, ones_bd)


def _attn_kernel(cur_ref, prev_ref, bias_ref, gain_ref, o_ref, lse_ref):
    first = pl.program_id(2) == 0
    cur = cur_ref[0]
    prv = prev_ref[0]
    gw = GROUP_WIDTH
    qg = gain_ref[0:1, :HEAD_DIM]
    kg = gain_ref[1:2, :HEAD_DIM]
    col = lax.broadcasted_iota(jnp.int32, (ATTN_BLOCK, 2 * ATTN_BLOCK), 1)
    dead = jnp.logical_and(first, col < ATTN_BLOCK)

    def rms(t, gain):
        return t * lax.rsqrt(jnp.mean(t * t, axis=-1, keepdims=True) + NORM_EPS) * gain

    for h in range(HEADS_PER_GROUP):
        sl = lambda base: slice(base + h * HEAD_DIM, base + (h + 1) * HEAD_DIM)
        q = rms(cur[:, sl(0)].astype(F32), qg) * (HEAD_DIM ** -0.5)
        kc = rms(cur[:, sl(gw)].astype(F32), kg)
        kp = rms(prv[:, sl(gw)].astype(F32), kg)
        kw = jnp.concatenate([kp, kc], axis=0).astype(BF16)
        vw = jnp.concatenate([prv[:, sl(2 * gw)], cur[:, sl(2 * gw)]], axis=0)
        bias = bias_ref[h]
        s = _dot_nt(q.astype(BF16), kw) + bias
        s = jnp.where(jnp.logical_or(dead, bias <= 0.5 * NEG_INF), NEG_INF, s)
        m = jnp.max(s, axis=-1, keepdims=True)
        p = jnp.exp(s - m)
        den = jnp.sum(p, axis=-1, keepdims=True)
        o = _dot(p.astype(BF16), vw) / den
        o_ref[0, :, h * HEAD_DIM:(h + 1) * HEAD_DIM] = o.astype(o_ref.dtype)
        lse_ref[0, :, h * HEAD_DIM:(h + 1) * HEAD_DIM] = jnp.broadcast_to(
            m + jnp.log(den), (ATTN_BLOCK, HEAD_DIM))


def _attn_group(pa, bias, gains, gi, bsz, seq):
    _, dil = ATTN_GROUPS[gi]
    sub = seq // dil
    nb = sub // ATTN_BLOCK
    view = pa.reshape(bsz, sub, dil * ATTN_COLS)
    chunk = 3 * GROUP_WIDTH
    ngr = len(ATTN_GROUPS)
    o, lse = pl.pallas_call(
        _attn_kernel,
        grid=(bsz, dil, nb),
        in_specs=[pl.BlockSpec((1, ATTN_BLOCK, chunk), lambda b, z, n: (b, n, ngr * z + gi)),
                  pl.BlockSpec((1, ATTN_BLOCK, chunk),
                               lambda b, z, n: (b, jnp.maximum(n - 1, 0), ngr * z + gi)),
                  pl.BlockSpec(bias.shape, lambda b, z, n: (0, 0, 0)),
                  pl.BlockSpec(gains.shape, lambda b, z, n: (0, 0))],
        out_specs=[pl.BlockSpec((1, ATTN_BLOCK, GROUP_WIDTH), lambda b, z, n: (b, n, z)),
                   pl.BlockSpec((1, ATTN_BLOCK, GROUP_WIDTH), lambda b, z, n: (b, n, z))],
        out_shape=[jax.ShapeDtypeStruct((bsz, sub, dil * GROUP_WIDTH), BF16),
                   jax.ShapeDtypeStruct((bsz, sub, dil * GROUP_WIDTH), F32)],
        compiler_params=_cparams(("parallel", "parallel", "arbitrary")),
        name=f"attn{gi}",
    )(view, view, bias, gains)
    n = bsz * seq
    return o.reshape(n, GROUP_WIDTH), lse.reshape(n, GROUP_WIDTH)


def _t5_bucket(dist):
    max_exact = N_BUCKETS // 2
    large = max_exact + (jnp.log(jnp.maximum(dist, max_exact).astype(F32) / max_exact)
                         / math.log(MAX_DISTANCE / max_exact) * (N_BUCKETS - max_exact)).astype(jnp.int32)
    return jnp.where(dist < max_exact, dist, jnp.minimum(large, N_BUCKETS - 1))


def _band_bias(rel_bias, gi):
    window, dil = ATTN_GROUPS[gi]
    qi = jnp.arange(ATTN_BLOCK)[:, None]
    kj = jnp.arange(2 * ATTN_BLOCK)[None, :]
    steps = qi - kj + ATTN_BLOCK
    heads = slice(gi * HEADS_PER_GROUP, (gi + 1) * HEADS_PER_GROUP)
    bias = jnp.transpose(rel_bias[:, heads][_t5_bucket(jnp.maximum(steps, 0) * dil)], (2, 0, 1))
    band = (steps >= 0) & (steps <= window // dil)
    return jnp.where(band[None], bias.astype(F32), NEG_INF)


def _post_kernel(x_ref, yr_ref, o0_ref, o1_ref, o2_ref, l0_ref, l1_ref, l2_ref, pg_ref,
                 gate1_ref, scale2_ref, shift2_ref, g2_ref, wbr_ref, wba_ref, wout_ref,
                 rwh_ref, rwl_ref, rb_ref,
                 x1_ref, h2_ref, gates_ref, eidx_ref, rank_ref, counts_ref, run_ref):
    i = pl.program_id(0)
    tm = x_ref.shape[0]

    @pl.when(i == 0)
    def _():
        run_ref[...] = jnp.zeros_like(run_ref)

    l0, l1, l2 = l0_ref[...], l1_ref[...], l2_ref[...]
    mx = jnp.maximum(jnp.maximum(l0, l1), l2)
    e0, e1, e2 = jnp.exp(l0 - mx), jnp.exp(l1 - mx), jnp.exp(l2 - mx)
    att = (e0 * o0_ref[...].astype(F32) + e1 * o1_ref[...].astype(F32)
           + e2 * o2_ref[...].astype(F32)) / (e0 + e1 + e2)

    y_r = _dot(yr_ref[...], wbr_ref[...])
    y_a = _dot(att.astype(BF16), wba_ref[...])
    d = D_MODEL
    mixed = (_sigmoid(pg_ref[:, :d].astype(F32)) * y_r
             + _sigmoid(pg_ref[:, d:].astype(F32)) * y_a)
    x1 = x_ref[...] + gate1_ref[0] * _dot(mixed.astype(BF16), wout_ref[...])
    x1_ref[...] = x1

    ms = jnp.mean(x1 * x1, axis=-1, keepdims=True)
    h2 = x1 * lax.rsqrt(ms + NORM_EPS) * (g2_ref[...] * (1.0 + scale2_ref[0])) + shift2_ref[0]
    h2_ref[...] = h2

    h_hi, h_lo = _split2(h2)
    rw_hi, rw_lo = rwh_ref[...], rwl_ref[...]
    logits = (_dot_nt(rw_hi, h_hi) + _dot_nt(rw_hi, h_lo) + _dot_nt(rw_lo, h_hi)
              + rb_ref[:, 0:1])
    eid = lax.broadcasted_iota(jnp.int32, (N_EXPERTS, tm), 0)
    vals, hots = [], []
    lg = logits
    for k in range(TOP_K):
        m = jnp.max(lg, axis=0, keepdims=True)
        idx = jnp.min(jnp.where(lg == m, eid, N_EXPERTS), axis=0, keepdims=True)
        hot = eid == idx
        vals.append(m)
        hots.append(hot)
        eidx_ref[k:k + 1, :] = idx
        lg = jnp.where(hot, -jnp.inf, lg)
    exps = [jnp.exp(v - vals[0]) for v in vals]
    tot = exps[0] + exps[1] + exps[2] + exps[3]
    for k in range(TOP_K):
        gates_ref[k:k + 1, :] = exps[k] / tot

    chosen = jnp.zeros((N_EXPERTS, tm), F32)
    for hot in hots:
        chosen = chosen + jnp.where(hot, 1.0, 0.0)
    ti = lax.broadcasted_iota(jnp.int32, (tm, tm), 0)
    tj = lax.broadcasted_iota(jnp.int32, (tm, tm), 1)
    before = jnp.where(ti < tj, 1.0, 0.0).astype(BF16)
    run = run_ref[:, 0:1]
    base = _dot(chosen.astype(BF16), before) + run
    for k in range(TOP_K):
        rank_ref[k:k + 1, :] = jnp.sum(jnp.where(hots[k], base, 0.0), axis=0,
                                       keepdims=True).astype(jnp.int32)
    run_new = run + jnp.sum(chosen, axis=1, keepdims=True)
    run_ref[...] = jnp.broadcast_to(run_new, run_ref.shape)
    counts_ref[...] = jnp.broadcast_to(run_new, counts_ref.shape).astype(jnp.int32)


def _post(x2, y_rwkv, outs, lses, pg, gate1, scale2, shift2, norm2_g, w_br_r, w_br_a, w_out,
          rw_hi, rw_lo, rb, seq):
    n, d = x2.shape
    tm = TOKEN_TILE
    per = seq // tm
    row = lambda width: pl.BlockSpec((tm, width), lambda i: (i, 0))
    const = lambda shape: pl.BlockSpec(shape, lambda i: (0,) * len(shape))
    bvec = pl.BlockSpec((1, 1, d), lambda i: (i // per, 0, 0))
    tcol = pl.BlockSpec((TOP_K, tm), lambda i: (0, i))
    return pl.pallas_call(
        _post_kernel,
        grid=(n // tm,),
        in_specs=[row(d), row(RWKV_WIDTH)] + [row(GROUP_WIDTH)] * 6 + [row(GATE_COLS),
                  bvec, bvec, bvec, const((1, d)), const(w_br_r.shape), const(w_br_a.shape),
                  const(w_out.shape), const(rw_hi.shape), const(rw_lo.shape), const(rb.shape)],
        out_specs=[row(d), row(d), tcol, tcol, tcol, const((N_EXPERTS, 128))],
        out_shape=[jax.ShapeDtypeStruct((n, d), F32),
                   jax.ShapeDtypeStruct((n, d), F32),
                   jax.ShapeDtypeStruct((TOP_K, n), F32),
                   jax.ShapeDtypeStruct((TOP_K, n), jnp.int32),
                   jax.ShapeDtypeStruct((TOP_K, n), jnp.int32),
                   jax.ShapeDtypeStruct((N_EXPERTS, 128), jnp.int32)],
        scratch_shapes=[pltpu.VMEM((N_EXPERTS, 128), F32)],
        compiler_params=_cparams(("arbitrary",)),
        name="post",
    )(x2, y_rwkv, *outs, *lses, pg, gate1, scale2, shift2, norm2_g, w_br_r, w_br_a, w_out,
      rw_hi, rw_lo, rb)


def _row_copy(src, src_row, dst, dst_row, sem):
    return pltpu.make_async_copy(src.at[pl.ds(src_row, 1)], dst.at[pl.ds(dst_row, 1)], sem)


def _dispatch_kernel(meta_ref, pos_ref, h2_ref, xs_ref, zero_ref, sem):
    i = pl.program_id(0)
    tm = h2_ref.shape[0]
    nblk = xs_ref.shape[0] // MOE_ROWS

    @pl.when(i == 0)
    def _():
        zero_ref[...] = jnp.zeros_like(zero_ref)

        def fill(start):
            cp = pltpu.make_async_copy(zero_ref, xs_ref.at[pl.ds(start, MOE_ROWS)], sem)
            cp.start()
            cp.wait()

        for e in range(N_EXPERTS):
            end = meta_ref[1 + e]
            begin = meta_ref[e] if e else 0

            @pl.when(end > begin)
            def _():
                fill(pl.multiple_of(end - MOE_ROWS, MOE_ROWS))

        def tail(j, carry):
            fill(pl.multiple_of(j * MOE_ROWS, MOE_ROWS))
            return carry

        lax.fori_loop(meta_ref[0], nblk, tail, 0)

    def issue(t, carry):
        for k in range(TOP_K):
            _row_copy(h2_ref, t, xs_ref, pos_ref[k, t], sem).start()
        return carry

    lax.fori_loop(0, tm, issue, 0)
    for k in range(TOP_K):
        pltpu.make_async_copy(h2_ref, xs_ref.at[pl.ds(0, tm)], sem).wait()


def _dispatch(meta, pos, h2, nblk):
    n, d = h2.shape
    tm = TOKEN_TILE
    return pl.pallas_call(
        _dispatch_kernel,
        grid_spec=pltpu.PrefetchScalarGridSpec(
            num_scalar_prefetch=1,
            grid=(n // tm,),
            in_specs=[pl.BlockSpec((TOP_K, tm), lambda i, meta: (0, i), memory_space=pltpu.SMEM),
                      pl.BlockSpec((tm, d), lambda i, meta: (i, 0))],
            out_specs=pl.BlockSpec(memory_space=pl.ANY),
            scratch_shapes=[pltpu.VMEM((MOE_ROWS, d), h2.dtype),
                            pltpu.SemaphoreType.DMA(())]),
        out_shape=jax.ShapeDtypeStruct((nblk * MOE_ROWS, d), h2.dtype),
        compiler_params=_cparams(("arbitrary",)),
        name="dispatch",
    )(meta, pos, h2)


def _expert_kernel(blk_ref, nv_ref, xs_ref, w1g_ref, w1l_ref, b1g_ref, b1l_ref, w2_ref, b2_ref,
                   o_ref):
    j = pl.program_id(0)

    @pl.when(j < nv_ref[0])
    def _():
        x = xs_ref[...].astype(BF16)
        glu = jnp.minimum(_dot(x, w1g_ref[...]) + b1g_ref[...], SWIGLU_LIMIT)
        lin = jnp.clip(_dot(x, w1l_ref[...]) + b1l_ref[...], -SWIGLU_LIMIT, SWIGLU_LIMIT)
        act = glu * _sigmoid(SWIGLU_ALPHA * glu) * (lin + 1.0)
        o_ref[...] = _dot(act.astype(BF16), w2_ref[...]) + b2_ref[...]

    @pl.when(j >= nv_ref[0])
    def _():
        o_ref[...] = jnp.zeros_like(o_ref)


def _experts(blk_e, nvalid, xs, w1g, w1l, b1g, b1l, w2, b2):
    rows, d = xs.shape
    nblk = rows // MOE_ROWS
    de = w1g.shape[2]
    xmap = lambda j, be, nv: (jnp.minimum(j, nv[0] - 1), 0)
    emap = lambda j, be, nv: (be[j], 0, 0)
    return pl.pallas_call(
        _expert_kernel,
        grid_spec=pltpu.PrefetchScalarGridSpec(
            num_scalar_prefetch=2,
            grid=(nblk,),
            in_specs=[pl.BlockSpec((MOE_ROWS, d), xmap),
                      pl.BlockSpec((None, d, de), emap), pl.BlockSpec((None, d, de), emap),
                      pl.BlockSpec((None, 1, de), emap), pl.BlockSpec((None, 1, de), emap),
                      pl.BlockSpec((None, de, d), emap), pl.BlockSpec((None, 1, d), emap)],
            out_specs=pl.BlockSpec((MOE_ROWS, d), lambda j, be, nv: (j, 0))),
        out_shape=jax.ShapeDtypeStruct((rows, d), F32),
        compiler_params=_cparams(("arbitrary",)),
        name="experts",
    )(blk_e, nvalid, xs, w1g, w1l, b1g, b1l, w2, b2)


def _combine_kernel(pos_ref, x1_ref, gt_ref, gate2_ref, o_hbm, out_ref, buf_ref, sem):
    tm = x1_ref.shape[0]

    def issue(t, carry):
        for k in range(TOP_K):
            _row_copy(o_hbm, pos_ref[k, t], buf_ref.at[k], t, sem).start()
        return carry

    lax.fori_loop(0, tm, issue, 0)
    for k in range(TOP_K):
        pltpu.make_async_copy(o_hbm.at[pl.ds(0, tm)], buf_ref.at[k], sem).wait()
    gt = gt_ref[...]
    acc = buf_ref[0] * gt[:, 0:1]
    for k in range(1, TOP_K):
        acc = acc + buf_ref[k] * gt[:, k:k + 1]
    out_ref[...] = x1_ref[...] + gate2_ref[0] * acc


def _combine(pos, x1, gates_t, gate2, o_rows, seq):
    n, d = x1.shape
    tm = TOKEN_TILE
    per = seq // tm
    return pl.pallas_call(
        _combine_kernel,
        grid=(n // tm,),
        in_specs=[pl.BlockSpec((TOP_K, tm), lambda i: (0, i), memory_space=pltpu.SMEM),
                  pl.BlockSpec((tm, d), lambda i: (i, 0)),
                  pl.BlockSpec((tm, TOP_K), lambda i: (i, 0)),
                  pl.BlockSpec((1, 1, d), lambda i: (i // per, 0, 0)),
                  pl.BlockSpec(memory_space=pl.ANY)],
        out_specs=pl.BlockSpec((tm, d), lambda i: (i, 0)),
        out_shape=jax.ShapeDtypeStruct((n, d), F32),
        scratch_shapes=[pltpu.VMEM((TOP_K, tm, d), F32), pltpu.SemaphoreType.DMA(())],
        compiler_params=_cparams(("arbitrary",)),
        name="combine",
    )(pos, x1, gates_t, gate2, o_rows)


def _layer(x, c, ada_w, ada_b, norm1_g, norm2_g, w_in, mu_rkv, mu_wag, w0, w1, w2, a0, a1, a2,
           g1, g2, k_k, k_a, r_k, ln_w, ln_b, qn_g, kn_g, rel_bias, w_br_rwkv, w_br_attn, w_out,
           router_w, router_b, exp_w1, exp_b1, exp_w2, exp_b2):
    bsz, seq, d = x.shape
    n = bsz * seq
    rw = RWKV_WIDTH
    assert d == D_MODEL and seq % (ATTN_BLOCK * ATTN_GROUPS[-1][1]) == 0 and seq % TOKEN_TILE == 0

    mod = _modulation(c, ada_w, ada_b)
    shift1, scale1, gate1, shift2, scale2, gate2 = [
        m.reshape(bsz, 1, d) for m in jnp.split(mod, 6, axis=-1)]

    lora_w = jnp.concatenate([w1, a1, g1], axis=1)
    lora_mu = jnp.concatenate([jnp.broadcast_to(mu_wag[0][:, None], w1.shape),
                               jnp.broadcast_to(mu_wag[1][:, None], a1.shape),
                               jnp.broadcast_to(mu_wag[2][:, None], g1.shape)], axis=1)
    w_r = jnp.concatenate([w_in[:, :3 * rw], lora_w * (1.0 - lora_mu), lora_w * lora_mu],
                          axis=1).astype(BF16)
    qa0 = 3 * rw
    acols = []
    for gi in range(len(ATTN_GROUPS)):
        for part in range(3):
            lo = qa0 + part * ATTN_WIDTH + gi * GROUP_WIDTH
            acols.append(w_in[:, lo:lo + GROUP_WIDTH])
    w_a = jnp.concatenate(acols, axis=1).astype(BF16)
    w_g = w_in[:, qa0 + 3 * ATTN_WIDTH:].astype(BF16)

    x2 = x.reshape(n, d)
    pr, pa, pg = _inproj(x2, norm1_g.reshape(1, d), scale1, shift1, w_r, w_a, w_g, seq)

    par = jnp.zeros((16, rw), F32)
    par = par.at[_P_W0].set(w0).at[_P_A0].set(a0).at[_P_KK].set(k_k).at[_P_KA].set(k_a)
    par = par.at[_P_RK].set(r_k.reshape(rw)).at[_P_LNW].set(ln_w).at[_P_LNB].set(ln_b)
    par = par.at[_P_MUR].set(mu_rkv[0]).at[_P_MUK].set(mu_rkv[1]).at[_P_MUV].set(mu_rkv[2])
    w_lora = jnp.zeros((LORA_WIDTH, 3 * rw), F32)
    w_lora = w_lora.at[:DECAY_LORA, :rw].set(w2)
    w_lora = w_lora.at[DECAY_LORA:DECAY_LORA + ICLR_LORA, rw:2 * rw].set(a2)
    w_lora = w_lora.at[DECAY_LORA + ICLR_LORA:, 2 * rw:].set(g2).astype(BF16)
    hid = jnp.arange(rw) // HEAD_DIM
    ones_bd = (hid[:, None] == hid[None, :]).astype(BF16)
    y_rwkv = _rwkv(pr, par, w_lora, ones_bd, bsz, seq)

    gains = jnp.zeros((8, 128), F32).at[0, :HEAD_DIM].set(qn_g).at[1, :HEAD_DIM].set(kn_g)
    outs, lses = [], []
    for gi in range(len(ATTN_GROUPS)):
        o, lse = _attn_group(pa, _band_bias(rel_bias, gi), gains, gi, bsz, seq)
        outs.append(o)
        lses.append(lse)

    rw_t = router_w.T
    rw_hi = rw_t.astype(BF16)
    rw_lo = (rw_t - rw_hi.astype(F32)).astype(BF16)
    rb = jnp.broadcast_to(router_b[:, None], (N_EXPERTS, 128))
    x1, h2, gates, eidx, rank, counts = _post(
        x2, y_rwkv, outs, lses, pg, gate1, scale2, shift2, norm2_g.reshape(1, d),
        w_br_rwkv.astype(BF16), w_br_attn.astype(BF16), w_out.astype(BF16), rw_hi, rw_lo, rb, seq)

    cnt = counts[:, 0]
    padded = (cnt + MOE_ROWS - 1) // MOE_ROWS * MOE_ROWS
    pend = jnp.cumsum(padded)
    pstart = pend - padded
    nblk = (n * TOP_K) // MOE_ROWS + N_EXPERTS
    nvalid = pend[-1] // MOE_ROWS
    blk_e = jnp.minimum(jnp.searchsorted(pend, jnp.arange(nblk, dtype=jnp.int32) * MOE_ROWS,
                                         side='right'), N_EXPERTS - 1).astype(jnp.int32)
    blk_e = jnp.where(jnp.arange(nblk) < nvalid, blk_e, blk_e[jnp.maximum(nvalid - 1, 0)])
    pos = jnp.sum(jnp.where(eidx[:, :, None] == jnp.arange(N_EXPERTS)[None, None, :],
                            pstart[None, None, :], 0), axis=-1) + rank
    pos = pos.astype(jnp.int32)
    meta = jnp.concatenate([nvalid[None], pend]).astype(jnp.int32)

    xs = _dispatch(meta, pos, h2, nblk)
    o_rows = _experts(blk_e, nvalid[None].astype(jnp.int32), xs,
                      exp_w1[:, :, 0::2].astype(BF16), exp_w1[:, :, 1::2].astype(BF16),
                      exp_b1[:, None, 0::2], exp_b1[:, None, 1::2],
                      exp_w2.astype(BF16), exp_b2[:, None, :])
    out = _combine(pos, x1, gates.T, gate2, o_rows, seq)
    return out.reshape(bsz, seq, d)


def kernel(x, c, ada_w, ada_b, norm1_g, norm2_g, w_in, rwkv_mu_rkv, rwkv_mu_wag, rwkv_w0, rwkv_w1, rwkv_w2, rwkv_a0, rwkv_a1, rwkv_a2, rwkv_g1, rwkv_g2, rwkv_k_k, rwkv_k_a, rwkv_r_k, rwkv_ln_w, rwkv_ln_b, attn_qn_g, attn_kn_g, rel_bias, w_br_rwkv, w_br_attn, w_out, router_w, router_b, exp_w1, exp_b1, exp_w2, exp_b2):
    per_layer = (ada_w, ada_b, norm1_g, norm2_g, w_in, rwkv_mu_rkv, rwkv_mu_wag, rwkv_w0, rwkv_w1,
                 rwkv_w2, rwkv_a0, rwkv_a1, rwkv_a2, rwkv_g1, rwkv_g2, rwkv_k_k, rwkv_k_a,
                 rwkv_r_k, rwkv_ln_w, rwkv_ln_b, attn_qn_g, attn_kn_g)
    tail = (w_br_rwkv, w_br_attn, w_out, router_w, router_b, exp_w1, exp_b1, exp_w2, exp_b2)
    for layer in range(ada_w.shape[0]):
        head = [p[layer] for p in per_layer]
        rest = [p[layer] for p in tail]
        x = _layer(x, c, *head, rel_bias, *rest)
    return x
```

```python
import functools
import math

import jax
import jax.numpy as jnp
from jax import lax
from jax.experimental import pallas as pl
from jax.experimental.pallas import tpu as pltpu

F32 = jnp.float32
BF16 = jnp.bfloat16

D_MODEL = 1024
HEAD_DIM = 64
RWKV_HEADS = 8
RWKV_WIDTH = RWKV_HEADS * HEAD_DIM
DECAY_LORA = 64
ICLR_LORA = 64
GATE_LORA = 128
LORA_WIDTH = DECAY_LORA + ICLR_LORA + GATE_LORA
GROUPNORM_EPS = 64e-5
ATTN_GROUPS = ((128, 1), (512, 4), (2048, 16))
HEADS_PER_GROUP = 4
ATTN_HEADS = HEADS_PER_GROUP * len(ATTN_GROUPS)
ATTN_WIDTH = ATTN_HEADS * HEAD_DIM
GROUP_WIDTH = HEADS_PER_GROUP * HEAD_DIM
ATTN_BLOCK = 128
N_BUCKETS = 32
MAX_DISTANCE = 2048
N_EXPERTS = 32
TOP_K = 4
D_EXPERT = D_MODEL
SWIGLU_ALPHA = 1.702
SWIGLU_LIMIT = 7.0
NORM_EPS = 1e-6
NEG_INF = -1e30

RWKV_COLS = 3 * RWKV_WIDTH + 2 * LORA_WIDTH
ATTN_COLS = 3 * ATTN_WIDTH
GATE_COLS = 2 * D_MODEL

CHUNK = 64
TOKEN_TILE = 512
MOE_ROWS = 512
MXU_TILE = 256
VMEM_LIMIT = 56 * 1024 * 1024


def _cparams(sem, vmem=VMEM_LIMIT):
    return pltpu.CompilerParams(dimension_semantics=sem, vmem_limit_bytes=vmem)


def _split2(a):
    hi = a.astype(BF16)
    lo = (a - hi.astype(F32)).astype(BF16)
    return hi, lo


def _split3(a):
    hi = a.astype(BF16)
    r = a - hi.astype(F32)
    mid = r.astype(BF16)
    lo = (r - mid.astype(F32)).astype(BF16)
    return hi, mid, lo


def _dot(a, b):
    return jnp.dot(a, b, preferred_element_type=F32)


def _dot_nt(a, b):
    return lax.dot_general(a, b, (((1,), (1,)), ((), ())), preferred_element_type=F32)


def _dot_tn(a, b):
    return lax.dot_general(a, b, (((0,), (0,)), ((), ())), preferred_element_type=F32)


def _sigmoid(x):
    return 1.0 / (1.0 + jnp.exp(-x))


def _mod_kernel(c_ref, w_ref, b_ref, o_ref):
    c = c_ref[...]
    s = c * _sigmoid(c)
    s_hi, s_lo = _split2(s)
    w_hi, w_lo = _split2(w_ref[...])
    o_ref[...] = _dot(s_hi, w_hi) + _dot(s_hi, w_lo) + _dot(s_lo, w_hi) + b_ref[...]


def _modulation(c, ada_w, ada_b):
    bsz, d = c.shape
    cols = ada_w.shape[1]
    tn = 1024
    return pl.pallas_call(
        _mod_kernel,
        grid=(cols // tn,),
        in_specs=[pl.BlockSpec((bsz, d), lambda j: (0, 0)),
                  pl.BlockSpec((d, tn), lambda j: (0, j)),
                  pl.BlockSpec((1, tn), lambda j: (0, j))],
        out_specs=pl.BlockSpec((bsz, tn), lambda j: (0, j)),
        out_shape=jax.ShapeDtypeStruct((bsz, cols), F32),
        compiler_params=_cparams(("arbitrary",)),
        name="mod",
    )(c, ada_w, ada_b.reshape(1, cols))


def _inproj_kernel(x_ref, g_ref, scale_ref, shift_ref, wr_ref, wa_ref, wg_ref,
                   pr_ref, pa_ref, pg_ref):
    x = x_ref[...]
    ms = jnp.mean(x * x, axis=-1, keepdims=True)
    h = x * lax.rsqrt(ms + NORM_EPS) * (g_ref[...] * (1.0 + scale_ref[0])) + shift_ref[0]
    hb = h.astype(BF16)
    step = 512
    for w_ref, o_ref in ((wr_ref, pr_ref), (wa_ref, pa_ref), (wg_ref, pg_ref)):
        cols = w_ref.shape[1]
        for j in range(0, cols, step):
            width = min(step, cols - j)
            o_ref[:, j:j + width] = _dot(hb, w_ref[:, j:j + width]).astype(BF16)


def _inproj(x2, norm_g, scale, shift, w_r, w_a, w_g, seq):
    n, d = x2.shape
    tm = TOKEN_TILE
    per = seq // tm
    const = lambda shape: pl.BlockSpec(shape, lambda i: (0, 0), pipeline_mode=pl.Buffered(1))
    return pl.pallas_call(
        _inproj_kernel,
        grid=(n // tm,),
        in_specs=[pl.BlockSpec((tm, d), lambda i: (i, 0)),
                  pl.BlockSpec((1, d), lambda i: (0, 0)),
                  pl.BlockSpec((1, 1, d), lambda i: (i // per, 0, 0)),
                  pl.BlockSpec((1, 1, d), lambda i: (i // per, 0, 0)),
                  const(w_r.shape), const(w_a.shape), const(w_g.shape)],
        out_specs=[pl.BlockSpec((tm, RWKV_COLS), lambda i: (i, 0)),
                   pl.BlockSpec((tm, ATTN_COLS), lambda i: (i, 0)),
                   pl.BlockSpec((tm, GATE_COLS), lambda i: (i, 0))],
        out_shape=[jax.ShapeDtypeStruct((n, RWKV_COLS), BF16),
                   jax.ShapeDtypeStruct((n, ATTN_COLS), BF16),
                   jax.ShapeDtypeStruct((n, GATE_COLS), BF16)],
        compiler_params=_cparams(("parallel",)),
        name="inproj",
    )(x2, norm_g, scale, shift, w_r, w_a, w_g)


_P_W0, _P_A0, _P_KK, _P_KA, _P_RK, _P_LNW, _P_LNB, _P_MUR, _P_MUK, _P_MUV = range(10)


def _rwkv_kernel(pr_ref, par_ref, wl_ref, ones_ref, o_ref, state_ref, prev_ref, y_ref):
    c_idx = pl.program_id(1)

    @pl.when(c_idx == 0)
    def _():
        state_ref[...] = jnp.zeros_like(state_ref)
        prev_ref[...] = jnp.zeros_like(prev_ref)

    rw = RWKV_WIDTH
    par = lambda i: par_ref[i:i + 1, :]
    ones_bd = ones_ref[...]

    def headsum(a):
        hi, lo = _split2(a)
        return _dot(hi, ones_bd) + _dot(lo, ones_bd)

    pr = pr_ref[...].astype(F32)
    row = lax.broadcasted_iota(jnp.int32, (CHUNK, 1), 0)
    prev_row = prev_ref[0:1, :]

    def shifted(lo, hi):
        return jnp.where(row == 0, prev_row[:, lo:hi], pltpu.roll(pr[:, lo:hi], 1, 0))

    sh_rkv = shifted(0, 3 * rw)
    sh_lora = shifted(3 * rw + LORA_WIDTH, 3 * rw + 2 * LORA_WIDTH)
    prev_ref[0:1, :] = pr[CHUNK - 1:CHUNK, :]

    lora = pr[:, 3 * rw:3 * rw + LORA_WIDTH] + sh_lora
    lane = lax.broadcasted_iota(jnp.int32, (1, LORA_WIDTH), 1)
    act = jnp.where(lane < DECAY_LORA, jnp.tanh(lora),
                    jnp.where(lane < DECAY_LORA + ICLR_LORA, lora, _sigmoid(lora)))
    lo_out = _dot(act.astype(BF16), wl_ref[...])
    w_pre = par(_P_W0) + lo_out[:, :rw]
    z = -w_pre
    softplus = jnp.maximum(z, 0.0) + jnp.log(1.0 + jnp.exp(-jnp.abs(z)))
    logd = -jnp.exp(-softplus - 0.5)
    a = _sigmoid(par(_P_A0) + lo_out[:, rw:2 * rw])
    g = lo_out[:, 2 * rw:3 * rw]

    r0, k0, v0 = pr[:, :rw], pr[:, rw:2 * rw], pr[:, 2 * rw:3 * rw]
    r = r0 + (sh_rkv[:, :rw] - r0) * par(_P_MUR)
    k = k0 + (sh_rkv[:, rw:2 * rw] - k0) * par(_P_MUK)
    v = v0 + (sh_rkv[:, 2 * rw:] - v0) * par(_P_MUV)

    kk = k * par(_P_KK)
    kk = kk / jnp.maximum(jnp.sqrt(headsum(kk * kk)), 1e-12)
    k2 = k * (1.0 + (a - 1.0) * par(_P_KA))

    ti = lax.broadcasted_iota(jnp.int32, (CHUNK, CHUNK), 0)
    tj = lax.broadcasted_iota(jnp.int32, (CHUNK, CHUNK), 1)
    low_incl = ti >= tj
    low_strict = ti > tj
    tril = jnp.where(low_incl, 1.0, 0.0).astype(BF16)
    d_hi, d_mid, d_lo = _split3(logd)
    cum = _dot(tril, d_hi) + _dot(tril, d_mid) + _dot(tril, d_lo)
    p_incl = jnp.exp(cum)
    p_excl = jnp.exp(cum - logd)
    p_inv = jnp.exp(-cum)

    a_t = (-kk * p_excl).astype(BF16)
    b_t = (kk * a * p_inv).astype(BF16)
    k_t = (k2 * p_inv).astype(BF16)
    r_t = (r * p_incl).astype(BF16)
    v_b = v.astype(BF16)
    eye = jnp.where(ti == tj, 1.0, 0.0)

    heads = range(RWKV_HEADS)
    sls = [slice(h * HEAD_DIM, (h + 1) * HEAD_DIM) for h in heads]
    ar = [jnp.concatenate([a_t[:, sl], r_t[:, sl]], axis=0) for sl in sls]
    bk = [jnp.concatenate([b_t[:, sl], k_t[:, sl]], axis=0) for sl in sls]
    vh = [v_b[:, sl] for sl in sls]
    s0 = [state_ref[h] for h in heads]
    m = [_dot_nt(ar[h], bk[h]) for h in heads]
    ars = [_dot_nt(ar[h], s0[h].astype(BF16)) for h in heads]
    a_ab = [jnp.where(low_strict, m[h][:CHUNK, :CHUNK], 0.0) for h in heads]
    a_ak = [jnp.where(low_strict, m[h][:CHUNK, CHUNK:], 0.0).astype(BF16) for h in heads]
    a_rb = [jnp.where(low_incl, m[h][CHUNK:, :CHUNK], 0.0).astype(BF16) for h in heads]
    a_rk = [jnp.where(low_incl, m[h][CHUNK:, CHUNK:], 0.0).astype(BF16) for h in heads]
    t_inv = [eye + a_ab[h] for h in heads]
    pw = [a_ab[h].astype(BF16) for h in heads]
    for _ in range(int(math.log2(CHUNK)) - 1):
        pw = [_dot(pw[h], pw[h]).astype(BF16) for h in heads]
        t_inv = [t_inv[h] + _dot(pw[h], t_inv[h].astype(BF16)) for h in heads]
    rhs = [ars[h][:CHUNK] + _dot(a_ak[h], vh[h]) for h in heads]
    ub = [_dot(t_inv[h].astype(BF16), rhs[h].astype(BF16)).astype(BF16) for h in heads]
    for h in heads:
        y_ref[:, sls[h]] = ars[h][CHUNK:] + _dot(a_rb[h], ub[h]) + _dot(a_rk[h], vh[h])
    for h in heads:
        uv = jnp.concatenate([ub[h], vh[h]], axis=0)
        state_ref[h] = (s0[h] + _dot_tn(uv, bk[h])) * p_incl[CHUNK - 1:CHUNK, sls[h]]

    y = y_ref[...]
    inv_n = 1.0 / HEAD_DIM
    mu = headsum(y) * inv_n
    dlt = y - mu
    var = headsum(dlt * dlt) * inv_n
    yn = dlt * lax.rsqrt(var + GROUPNORM_EPS) * par(_P_LNW) + par(_P_LNB)
    bonus = headsum(r * k2 * par(_P_RK)) * v
    o_ref[...] = ((yn + bonus) * g).astype(BF16)


def _rwkv(pr, par, w_lora, ones_bd, bsz, seq):
    n = pr.shape[0]
    nchunk = seq // CHUNK
    const = lambda shape: pl.BlockSpec(shape, lambda b, c: (0, 0))
    return pl.pallas_call(
        _rwkv_kernel,
        grid=(bsz, nchunk),
        in_specs=[pl.BlockSpec((CHUNK, RWKV_COLS), lambda b, c: (b * nchunk + c, 0)),
                  const(par.shape), const(w_lora.shape), const(ones_bd.shape)],
        out_specs=pl.BlockSpec((CHUNK, RWKV_WIDTH), lambda b, c: (b * nchunk + c, 0)),
        out_shape=jax.ShapeDtypeStruct((n, RWKV_WIDTH), BF16),
        scratch_shapes=[pltpu.VMEM((RWKV_HEADS, HEAD_DIM, HEAD_DIM), F32),
                        pltpu.VMEM((8, RWKV_COLS), F32),
                        pltpu.VMEM((CHUNK, RWKV_WIDTH), F32)],
        compiler_params=_cparams(("parallel", "arbitrary")),
        name="rwkv",
    )(pr, par, w_lora, ones_bd)


def _attn_kernel(cur_ref, prev_ref, bias_ref, gain_ref, o_ref, lse_ref):
    first = pl.program_id(2) == 0
    cur = cur_ref[0]
    prv = prev_ref[0]
    gw = GROUP_WIDTH
    qg = gain_ref[0:1, :HEAD_DIM]
    kg = gain_ref[1:2, :HEAD_DIM]
    col = lax.broadcasted_iota(jnp.int32, (ATTN_BLOCK, 2 * ATTN_BLOCK), 1)
    dead = jnp.logical_and(first, col < ATTN_BLOCK)

    def rms(t, gain):
        return t * lax.rsqrt(jnp.mean(t * t, axis=-1, keepdims=True) + NORM_EPS) * gain

    for h in range(HEADS_PER_GROUP):
        sl = lambda base: slice(base + h * HEAD_DIM, base + (h + 1) * HEAD_DIM)
        q = rms(cur[:, sl(0)].astype(F32), qg) * (HEAD_DIM ** -0.5)
        kc = rms(cur[:, sl(gw)].astype(F32), kg)
        kp = rms(prv[:, sl(gw)].astype(F32), kg)
        kw = jnp.concatenate([kp, kc], axis=0).astype(BF16)
        vw = jnp.concatenate([prv[:, sl(2 * gw)], cur[:, sl(2 * gw)]], axis=0)
        bias = bias_ref[h]
        s = _dot_nt(q.astype(BF16), kw) + bias
        s = jnp.where(jnp.logical_or(dead, bias <= 0.5 * NEG_INF), NEG_INF, s)
        m = jnp.max(s, axis=-1, keepdims=True)
        p = jnp.exp(s - m)
        den = jnp.sum(p, axis=-1, keepdims=True)
        o = _dot(p.astype(BF16), vw) / den
        o_ref[0, :, h * HEAD_DIM:(h + 1) * HEAD_DIM] = o.astype(o_ref.dtype)
        lse_ref[0, :, h * HEAD_DIM:(h + 1) * HEAD_DIM] = jnp.broadcast_to(
            m + jnp.log(den), (ATTN_BLOCK, HEAD_DIM))


def _attn_group(pa, bias, gains, gi, bsz, seq):
    _, dil = ATTN_GROUPS[gi]
    sub = seq // dil
    nb = sub // ATTN_BLOCK
    view = pa.reshape(bsz, sub, dil * ATTN_COLS)
    chunk = 3 * GROUP_WIDTH
    ngr = len(ATTN_GROUPS)
    o, lse = pl.pallas_call(
        _attn_kernel,
        grid=(bsz, dil, nb),
        in_specs=[pl.BlockSpec((1, ATTN_BLOCK, chunk), lambda b, z, n: (b, n, ngr * z + gi)),
                  pl.BlockSpec((1, ATTN_BLOCK, chunk),
                               lambda b, z, n: (b, jnp.maximum(n - 1, 0), ngr * z + gi)),
                  pl.BlockSpec(bias.shape, lambda b, z, n: (0, 0, 0)),
                  pl.BlockSpec(gains.shape, lambda b, z, n: (0, 0))],
        out_specs=[pl.BlockSpec((1, ATTN_BLOCK, GROUP_WIDTH), lambda b, z, n: (b, n, z)),
                   pl.BlockSpec((1, ATTN_BLOCK, GROUP_WIDTH), lambda b, z, n: (b, n, z))],
        out_shape=[jax.ShapeDtypeStruct((bsz, sub, dil * GROUP_WIDTH), BF16),
                   jax.ShapeDtypeStruct((bsz, sub, dil * GROUP_WIDTH), F32)],
        compiler_params=_cparams(("parallel", "parallel", "arbitrary")),
        name=f"attn{gi}",
    )(view, view, bias, gains)
    n = bsz * seq
    return o.reshape(n, GROUP_WIDTH), lse.reshape(n, GROUP_WIDTH)


def _t5_bucket(dist):
    max_exact = N_BUCKETS // 2
    large = max_exact + (jnp.log(jnp.maximum(dist, max_exact).astype(F32) / max_exact)
                         / math.log(MAX_DISTANCE / max_exact) * (N_BUCKETS - max_exact)).astype(jnp.int32)
    return jnp.where(dist < max_exact, dist, jnp.minimum(large, N_BUCKETS - 1))


def _band_bias(rel_bias, gi):
    window, dil = ATTN_GROUPS[gi]
    qi = jnp.arange(ATTN_BLOCK)[:, None]
    kj = jnp.arange(2 * ATTN_BLOCK)[None, :]
    steps = qi - kj + ATTN_BLOCK
    heads = slice(gi * HEADS_PER_GROUP, (gi + 1) * HEADS_PER_GROUP)
    bucket = _t5_bucket(jnp.maximum(steps, 0) * dil)
    onehot = (bucket[None] == jnp.arange(N_BUCKETS)[:, None, None]).astype(F32)
    bias = jnp.sum(onehot[:, None] * rel_bias[:, heads][:, :, None, None], axis=0)
    band = (steps >= 0) & (steps <= window // dil)
    return jnp.where(band[None], bias.astype(F32), NEG_INF)


def _post_kernel(x_ref, yr_ref, o0_ref, o1_ref, o2_ref, l0_ref, l1_ref, l2_ref, pg_ref,
                 gate1_ref, scale2_ref, shift2_ref, g2_ref, wbr_ref, wba_ref, wout_ref,
                 rwh_ref, rwl_ref, rb_ref,
                 x1_ref, h2_ref, gates_ref, eidx_ref, rank_ref, counts_ref, run_ref):
    i = pl.program_id(0)
    tm = x_ref.shape[0]

    @pl.when(i == 0)
    def _():
        run_ref[...] = jnp.zeros_like(run_ref)

    l0, l1, l2 = l0_ref[...], l1_ref[...], l2_ref[...]
    mx = jnp.maximum(jnp.maximum(l0, l1), l2)
    e0, e1, e2 = jnp.exp(l0 - mx), jnp.exp(l1 - mx), jnp.exp(l2 - mx)
    att = (e0 * o0_ref[...].astype(F32) + e1 * o1_ref[...].astype(F32)
           + e2 * o2_ref[...].astype(F32)) / (e0 + e1 + e2)

    y_r = _dot(yr_ref[...], wbr_ref[...])
    y_a = _dot(att.astype(BF16), wba_ref[...])
    d = D_MODEL
    mixed = (_sigmoid(pg_ref[:, :d].astype(F32)) * y_r
             + _sigmoid(pg_ref[:, d:].astype(F32)) * y_a)
    x1 = x_ref[...] + gate1_ref[0] * _dot(mixed.astype(BF16), wout_ref[...])
    x1_ref[...] = x1

    ms = jnp.mean(x1 * x1, axis=-1, keepdims=True)
    h2 = x1 * lax.rsqrt(ms + NORM_EPS) * (g2_ref[...] * (1.0 + scale2_ref[0])) + shift2_ref[0]
    h2_ref[...] = h2

    h_hi, h_lo = _split2(h2)
    rw_hi, rw_lo = rwh_ref[...], rwl_ref[...]
    logits = (_dot_nt(rw_hi, h_hi) + _dot_nt(rw_hi, h_lo) + _dot_nt(rw_lo, h_hi)
              + rb_ref[:, 0:1])
    eid = lax.broadcasted_iota(jnp.int32, (N_EXPERTS, tm), 0)
    vals, hots = [], []
    lg = logits
    for k in range(TOP_K):
        m = jnp.max(lg, axis=0, keepdims=True)
        idx = jnp.min(jnp.where(lg == m, eid, N_EXPERTS), axis=0, keepdims=True)
        hot = eid == idx
        vals.append(m)
        hots.append(hot)
        eidx_ref[k:k + 1, :] = idx
        lg = jnp.where(hot, -jnp.inf, lg)
    exps = [jnp.exp(v - vals[0]) for v in vals]
    tot = exps[0] + exps[1] + exps[2] + exps[3]
    for k in range(TOP_K):
        gates_ref[k:k + 1, :] = exps[k] / tot

    chosen = jnp.zeros((N_EXPERTS, tm), F32)
    for hot in hots:
        chosen = chosen + jnp.where(hot, 1.0, 0.0)
    ti = lax.broadcasted_iota(jnp.int32, (tm, tm), 0)
    tj = lax.broadcasted_iota(jnp.int32, (tm, tm), 1)
    before = jnp.where(ti < tj, 1.0, 0.0).astype(BF16)
    run = run_ref[:, 0:1]
    base = _dot(chosen.astype(BF16), before) + run
    for k in range(TOP_K):
        rank_ref[k:k + 1, :] = jnp.sum(jnp.where(hots[k], base, 0.0), axis=0,
                                       keepdims=True).astype(jnp.int32)
    run_new = run + jnp.sum(chosen, axis=1, keepdims=True)
    run_ref[...] = jnp.broadcast_to(run_new, run_ref.shape)
    counts_ref[...] = jnp.broadcast_to(run_new, counts_ref.shape).astype(jnp.int32)


def _post(x2, y_rwkv, outs, lses, pg, gate1, scale2, shift2, norm2_g, w_br_r, w_br_a, w_out,
          rw_hi, rw_lo, rb, seq):
    n, d = x2.shape
    tm = TOKEN_TILE
    per = seq // tm
    row = lambda width: pl.BlockSpec((tm, width), lambda i: (i, 0))
    const = lambda shape: pl.BlockSpec(shape, lambda i: (0,) * len(shape))
    bvec = pl.BlockSpec((1, 1, d), lambda i: (i // per, 0, 0))
    tcol = pl.BlockSpec((TOP_K, tm), lambda i: (0, i))
    return pl.pallas_call(
        _post_kernel,
        grid=(n // tm,),
        in_specs=[row(d), row(RWKV_WIDTH)] + [row(GROUP_WIDTH)] * 6 + [row(GATE_COLS),
                  bvec, bvec, bvec, const((1, d)), const(w_br_r.shape), const(w_br_a.shape),
                  const(w_out.shape), const(rw_hi.shape), const(rw_lo.shape), const(rb.shape)],
        out_specs=[row(d), row(d), tcol, tcol, tcol, const((N_EXPERTS, 128))],
        out_shape=[jax.ShapeDtypeStruct((n, d), F32),
                   jax.ShapeDtypeStruct((n, d), F32),
                   jax.ShapeDtypeStruct((TOP_K, n), F32),
                   jax.ShapeDtypeStruct((TOP_K, n), jnp.int32),
                   jax.ShapeDtypeStruct((TOP_K, n), jnp.int32),
                   jax.ShapeDtypeStruct((N_EXPERTS, 128), jnp.int32)],
        scratch_shapes=[pltpu.VMEM((N_EXPERTS, 128), F32)],
        compiler_params=_cparams(("arbitrary",)),
        name="post",
    )(x2, y_rwkv, *outs, *lses, pg, gate1, scale2, shift2, norm2_g, w_br_r, w_br_a, w_out,
      rw_hi, rw_lo, rb)


def _row_copy(src, src_row, dst, dst_row, sem):
    return pltpu.make_async_copy(src.at[pl.ds(src_row, 1)], dst.at[pl.ds(dst_row, 1)], sem)


def _dispatch_kernel(meta_ref, pos_ref, h2_ref, xs_ref, zero_ref, sem):
    i = pl.program_id(0)
    tm = h2_ref.shape[0]
    nblk = xs_ref.shape[0] // MOE_ROWS

    @pl.when(i == 0)
    def _():
        zero_ref[...] = jnp.zeros_like(zero_ref)

        def fill(start):
            cp = pltpu.make_async_copy(zero_ref, xs_ref.at[pl.ds(start, MOE_ROWS)], sem)
            cp.start()
            cp.wait()

        for e in range(N_EXPERTS):
            end = meta_ref[1 + e]
            begin = meta_ref[e] if e else 0

            @pl.when(end > begin)
            def _():
                fill(pl.multiple_of(end - MOE_ROWS, MOE_ROWS))

        def tail(j, carry):
            fill(pl.multiple_of(j * MOE_ROWS, MOE_ROWS))
            return carry

        lax.fori_loop(meta_ref[0], nblk, tail, 0)

    def issue(t, carry):
        for k in range(TOP_K):
            _row_copy(h2_ref, t, xs_ref, pos_ref[k, t], sem).start()
        return carry

    lax.fori_loop(0, tm, issue, 0)
    for k in range(TOP_K):
        pltpu.make_async_copy(h2_ref, xs_ref.at[pl.ds(0, tm)], sem).wait()


def _dispatch(meta, pos, h2, nblk):
    n, d = h2.shape
    tm = TOKEN_TILE
    return pl.pallas_call(
        _dispatch_kernel,
        grid_spec=pltpu.PrefetchScalarGridSpec(
            num_scalar_prefetch=1,
            grid=(n // tm,),
            in_specs=[pl.BlockSpec((TOP_K, tm), lambda i, meta: (0, i), memory_space=pltpu.SMEM),
                      pl.BlockSpec((tm, d), lambda i, meta: (i, 0))],
            out_specs=pl.BlockSpec(memory_space=pl.ANY),
            scratch_shapes=[pltpu.VMEM((MOE_ROWS, d), h2.dtype),
                            pltpu.SemaphoreType.DMA(())]),
        out_shape=jax.ShapeDtypeStruct((nblk * MOE_ROWS, d), h2.dtype),
        compiler_params=_cparams(("arbitrary",)),
        name="dispatch",
    )(meta, pos, h2)


def _expert_kernel(blk_ref, nv_ref, xs_ref, w1g_ref, w1l_ref, b1g_ref, b1l_ref, w2_ref, b2_ref,
                   o_ref):
    j = pl.program_id(0)

    @pl.when(j < nv_ref[0])
    def _():
        x = xs_ref[...].astype(BF16)
        glu = jnp.minimum(_dot(x, w1g_ref[...]) + b1g_ref[...], SWIGLU_LIMIT)
        lin = jnp.clip(_dot(x, w1l_ref[...]) + b1l_ref[...], -SWIGLU_LIMIT, SWIGLU_LIMIT)
        act = glu * _sigmoid(SWIGLU_ALPHA * glu) * (lin + 1.0)
        o_ref[...] = _dot(act.astype(BF16), w2_ref[...]) + b2_ref[...]

    @pl.when(j >= nv_ref[0])
    def _():
        o_ref[...] = jnp.zeros_like(o_ref)


def _experts(blk_e, nvalid, xs, w1g, w1l, b1g, b1l, w2, b2):
    rows, d = xs.shape
    nblk = rows // MOE_ROWS
    de = w1g.shape[2]
    xmap = lambda j, be, nv: (jnp.minimum(j, nv[0] - 1), 0)
    emap = lambda j, be, nv: (be[j], 0, 0)
    return pl.pallas_call(
        _expert_kernel,
        grid_spec=pltpu.PrefetchScalarGridSpec(
            num_scalar_prefetch=2,
            grid=(nblk,),
            in_specs=[pl.BlockSpec((MOE_ROWS, d), xmap),
                      pl.BlockSpec((None, d, de), emap), pl.BlockSpec((None, d, de), emap),
                      pl.BlockSpec((None, 1, de), emap), pl.BlockSpec((None, 1, de), emap),
                      pl.BlockSpec((None, de, d), emap), pl.BlockSpec((None, 1, d), emap)],
            out_specs=pl.BlockSpec((MOE_ROWS, d), lambda j, be, nv: (j, 0))),
        out_shape=jax.ShapeDtypeStruct((rows, d), F32),
        compiler_params=_cparams(("arbitrary",)),
        name="experts",
    )(blk_e, nvalid, xs, w1g, w1l, b1g, b1l, w2, b2)


def _split_w1_kernel(w_ref, perm_ref, g_ref, l_ref):
    t = MXU_TILE
    for cb in range(w_ref.shape[1] // t):
        res = _dot(w_ref[:, cb * t:(cb + 1) * t].astype(BF16), perm_ref[...])
        g_ref[:, cb * (t // 2):(cb + 1) * (t // 2)] = res[:, :t // 2].astype(BF16)
        l_ref[:, cb * (t // 2):(cb + 1) * (t // 2)] = res[:, t // 2:].astype(BF16)


def _split_w1(w1):
    ne, d, two_de = w1.shape
    t = MXU_TILE
    src = jnp.arange(t)[:, None]
    dst = jnp.arange(t)[None, :]
    perm = (src == jnp.where(dst < t // 2, 2 * dst, 2 * (dst - t // 2) + 1)).astype(BF16)
    return pl.pallas_call(
        _split_w1_kernel,
        grid=(ne,),
        in_specs=[pl.BlockSpec((None, d, two_de), lambda e: (e, 0, 0)),
                  pl.BlockSpec((t, t), lambda e: (0, 0))],
        out_specs=[pl.BlockSpec((None, d, two_de // 2), lambda e: (e, 0, 0)),
                   pl.BlockSpec((None, d, two_de // 2), lambda e: (e, 0, 0))],
        out_shape=[jax.ShapeDtypeStruct((ne, d, two_de // 2), BF16),
                   jax.ShapeDtypeStruct((ne, d, two_de // 2), BF16)],
        compiler_params=_cparams(("parallel",)),
        name="split_w1",
    )(w1, perm)


def _combine_kernel(pos_ref, x1_ref, gt_ref, gate2_ref, o_hbm, out_ref, buf_ref, sem):
    tm = x1_ref.shape[0]

    def issue(t, carry):
        for k in range(TOP_K):
            _row_copy(o_hbm, pos_ref[k, t], buf_ref.at[k], t, sem).start()
        return carry

    lax.fori_loop(0, tm, issue, 0)
    for k in range(TOP_K):
        pltpu.make_async_copy(o_hbm.at[pl.ds(0, tm)], buf_ref.at[k], sem).wait()
    gt = gt_ref[...]
    acc = buf_ref[0] * gt[:, 0:1]
    for k in range(1, TOP_K):
        acc = acc + buf_ref[k] * gt[:, k:k + 1]
    out_ref[...] = x1_ref[...] + gate2_ref[0] * acc


def _combine(pos, x1, gates_t, gate2, o_rows, seq):
    n, d = x1.shape
    tm = TOKEN_TILE
    per = seq // tm
    return pl.pallas_call(
        _combine_kernel,
        grid=(n // tm,),
        in_specs=[pl.BlockSpec((TOP_K, tm), lambda i: (0, i), memory_space=pltpu.SMEM),
                  pl.BlockSpec((tm, d), lambda i: (i, 0)),
                  pl.BlockSpec((tm, TOP_K), lambda i: (i, 0)),
                  pl.BlockSpec((1, 1, d), lambda i: (i // per, 0, 0)),
                  pl.BlockSpec(memory_space=pl.ANY)],
        out_specs=pl.BlockSpec((tm, d), lambda i: (i, 0)),
        out_shape=jax.ShapeDtypeStruct((n, d), F32),
        scratch_shapes=[pltpu.VMEM((TOP_K, tm, d), F32), pltpu.SemaphoreType.DMA(())],
        compiler_params=_cparams(("arbitrary",)),
        name="combine",
    )(pos, x1, gates_t, gate2, o_rows)


def _layer(x, c, ada_w, ada_b, norm1_g, norm2_g, w_in, mu_rkv, mu_wag, w0, w1, w2, a0, a1, a2,
           g1, g2, k_k, k_a, r_k, ln_w, ln_b, qn_g, kn_g, rel_bias, w_br_rwkv, w_br_attn, w_out,
           router_w, router_b, exp_w1, exp_b1, exp_w2, exp_b2):
    bsz, seq, d = x.shape
    n = bsz * seq
    rw = RWKV_WIDTH
    assert d == D_MODEL and seq % (ATTN_BLOCK * ATTN_GROUPS[-1][1]) == 0 and seq % TOKEN_TILE == 0

    mod = _modulation(c, ada_w, ada_b)
    shift1, scale1, gate1, shift2, scale2, gate2 = [
        m.reshape(bsz, 1, d) for m in jnp.split(mod, 6, axis=-1)]

    lora_w = jnp.concatenate([w1, a1, g1], axis=1)
    lora_mu = jnp.concatenate([jnp.broadcast_to(mu_wag[0][:, None], w1.shape),
                               jnp.broadcast_to(mu_wag[1][:, None], a1.shape),
                               jnp.broadcast_to(mu_wag[2][:, None], g1.shape)], axis=1)
    w_r = jnp.concatenate([w_in[:, :3 * rw], lora_w * (1.0 - lora_mu), lora_w * lora_mu],
                          axis=1).astype(BF16)
    qa0 = 3 * rw
    acols = []
    for gi in range(len(ATTN_GROUPS)):
        for part in range(3):
            lo = qa0 + part * ATTN_WIDTH + gi * GROUP_WIDTH
            acols.append(w_in[:, lo:lo + GROUP_WIDTH])
    w_a = jnp.concatenate(acols, axis=1).astype(BF16)
    w_g = w_in[:, qa0 + 3 * ATTN_WIDTH:].astype(BF16)

    x2 = x.reshape(n, d)
    pr, pa, pg = _inproj(x2, norm1_g.reshape(1, d), scale1, shift1, w_r, w_a, w_g, seq)

    par = jnp.stack([w0, a0, k_k, k_a, r_k.reshape(rw), ln_w, ln_b,
                     mu_rkv[0], mu_rkv[1], mu_rkv[2]]
                    + [jnp.zeros((rw,), F32)] * 6)
    w_lora = jnp.concatenate([jnp.pad(w2, ((0, 0), (0, 2 * rw))),
                              jnp.pad(a2, ((0, 0), (rw, rw))),
                              jnp.pad(g2, ((0, 0), (2 * rw, 0)))], axis=0).astype(BF16)
    hid = jnp.arange(rw) // HEAD_DIM
    ones_bd = (hid[:, None] == hid[None, :]).astype(BF16)
    y_rwkv = _rwkv(pr, par, w_lora, ones_bd, bsz, seq)

    gains = jnp.pad(jnp.stack([qn_g, kn_g]), ((0, 6), (0, 128 - HEAD_DIM)))
    outs, lses = [], []
    for gi in range(len(ATTN_GROUPS)):
        o, lse = _attn_group(pa, _band_bias(rel_bias, gi), gains, gi, bsz, seq)
        outs.append(o)
        lses.append(lse)

    rw_t = router_w.T
    rw_hi = rw_t.astype(BF16)
    rw_lo = (rw_t - rw_hi.astype(F32)).astype(BF16)
    rb = jnp.broadcast_to(router_b[:, None], (N_EXPERTS, 128))
    x1, h2, gates, eidx, rank, counts = _post(
        x2, y_rwkv, outs, lses, pg, gate1, scale2, shift2, norm2_g.reshape(1, d),
        w_br_rwkv.astype(BF16), w_br_attn.astype(BF16), w_out.astype(BF16), rw_hi, rw_lo, rb, seq)

    cnt = counts[:, 0]
    padded = (cnt + MOE_ROWS - 1) // MOE_ROWS * MOE_ROWS
    pend = jnp.cumsum(padded)
    pstart = pend - padded
    nblk = (n * TOP_K) // MOE_ROWS + N_EXPERTS
    nvalid = pend[-1] // MOE_ROWS
    first_row = jnp.minimum(jnp.arange(nblk, dtype=jnp.int32), nvalid - 1) * MOE_ROWS
    blk_e = jnp.sum((pend[None, :] <= first_row[:, None]).astype(jnp.int32), axis=1)
    blk_e = jnp.clip(blk_e, 0, N_EXPERTS - 1)
    pos = jnp.sum(jnp.where(eidx[:, :, None] == jnp.arange(N_EXPERTS)[None, None, :],
                            pstart[None, None, :], 0), axis=-1) + rank
    pos = pos.astype(jnp.int32)
    meta = jnp.concatenate([nvalid[None], pend]).astype(jnp.int32)

    xs = _dispatch(meta, pos, h2, nblk)
    w1g, w1l = _split_w1(exp_w1)
    o_rows = _experts(blk_e, nvalid[None].astype(jnp.int32), xs, w1g, w1l,
                      exp_b1[:, None, 0::2], exp_b1[:, None, 1::2],
                      exp_w2.astype(BF16), exp_b2[:, None, :])
    out = _combine(pos, x1, gates.T, gate2, o_rows, seq)
    return out.reshape(bsz, seq, d)


def kernel(x, c, ada_w, ada_b, norm1_g, norm2_g, w_in, rwkv_mu_rkv, rwkv_mu_wag, rwkv_w0, rwkv_w1, rwkv_w2, rwkv_a0, rwkv_a1, rwkv_a2, rwkv_g1, rwkv_g2, rwkv_k_k, rwkv_k_a, rwkv_r_k, rwkv_ln_w, rwkv_ln_b, attn_qn_g, attn_kn_g, rel_bias, w_br_rwkv, w_br_attn, w_out, router_w, router_b, exp_w1, exp_b1, exp_w2, exp_b2):
    per_layer = (ada_w, ada_b, norm1_g, norm2_g, w_in, rwkv_mu_rkv, rwkv_mu_wag, rwkv_w0, rwkv_w1,
                 rwkv_w2, rwkv_a0, rwkv_a1, rwkv_a2, rwkv_g1, rwkv_g2, rwkv_k_k, rwkv_k_a,
                 rwkv_r_k, rwkv_ln_w, rwkv_ln_b, attn_qn_g, attn_kn_g)
    tail = (w_br_rwkv, w_br_attn, w_out, router_w, router_b, exp_w1, exp_b1, exp_w2, exp_b2)
    for layer in range(ada_w.shape[0]):
        head = [p[layer] for p in per_layer]
        rest = [p[layer] for p in tail]
        x = _layer(x, c, *head, rel_bias, *rest)
    return x
```

```python
import functools
import math

import jax
import jax.numpy as jnp
from jax import lax
from jax.experimental import pallas as pl
from jax.experimental.pallas import tpu as pltpu

F32 = jnp.float32
BF16 = jnp.bfloat16

D_MODEL = 1024
HEAD_DIM = 64
RWKV_HEADS = 8
RWKV_WIDTH = RWKV_HEADS * HEAD_DIM
DECAY_LORA = 64
ICLR_LORA = 64
GATE_LORA = 128
LORA_WIDTH = DECAY_LORA + ICLR_LORA + GATE_LORA
GROUPNORM_EPS = 64e-5
ATTN_GROUPS = ((128, 1), (512, 4), (2048, 16))
HEADS_PER_GROUP = 4
ATTN_HEADS = HEADS_PER_GROUP * len(ATTN_GROUPS)
ATTN_WIDTH = ATTN_HEADS * HEAD_DIM
GROUP_WIDTH = HEADS_PER_GROUP * HEAD_DIM
ATTN_BLOCK = 128
N_BUCKETS = 32
MAX_DISTANCE = 2048
N_EXPERTS = 32
TOP_K = 4
D_EXPERT = D_MODEL
SWIGLU_ALPHA = 1.702
SWIGLU_LIMIT = 7.0
NORM_EPS = 1e-6
NEG_INF = -1e30

RWKV_COLS = 3 * RWKV_WIDTH + 2 * LORA_WIDTH
ATTN_COLS = 3 * ATTN_WIDTH
GATE_COLS = 2 * D_MODEL

CHUNK = 64
TOKEN_TILE = 512
MOE_ROWS = 512
MXU_TILE = 256
LANES = 128
VMEM_LIMIT = 56 * 1024 * 1024


def _cparams(sem, vmem=VMEM_LIMIT):
    return pltpu.CompilerParams(dimension_semantics=sem, vmem_limit_bytes=vmem)


def _split2(a):
    hi = a.astype(BF16)
    lo = (a - hi.astype(F32)).astype(BF16)
    return hi, lo


def _split3(a):
    hi = a.astype(BF16)
    r = a - hi.astype(F32)
    mid = r.astype(BF16)
    lo = (r - mid.astype(F32)).astype(BF16)
    return hi, mid, lo


def _dot(a, b):
    return jnp.dot(a, b, preferred_element_type=F32)


def _dot_nt(a, b):
    return lax.dot_general(a, b, (((1,), (1,)), ((), ())), preferred_element_type=F32)


def _dot_tn(a, b):
    return lax.dot_general(a, b, (((0,), (0,)), ((), ())), preferred_element_type=F32)


def _sigmoid(x):
    return 1.0 / (1.0 + jnp.exp(-x))


def _mod_kernel(c_ref, w_ref, b_ref, o_ref):
    c = c_ref[...]
    s = c * _sigmoid(c)
    s_hi, s_lo = _split2(s)
    w_hi, w_lo = _split2(w_ref[...])
    o_ref[...] = _dot(s_hi, w_hi) + _dot(s_hi, w_lo) + _dot(s_lo, w_hi) + b_ref[...]


def _modulation(c, ada_w, ada_b):
    bsz, d = c.shape
    cols = ada_w.shape[1]
    tn = 1024
    return pl.pallas_call(
        _mod_kernel,
        grid=(cols // tn,),
        in_specs=[pl.BlockSpec((bsz, d), lambda j: (0, 0)),
                  pl.BlockSpec((d, tn), lambda j: (0, j)),
                  pl.BlockSpec((1, tn), lambda j: (0, j))],
        out_specs=pl.BlockSpec((bsz, tn), lambda j: (0, j)),
        out_shape=jax.ShapeDtypeStruct((bsz, cols), F32),
        compiler_params=_cparams(("arbitrary",)),
        name="mod",
    )(c, ada_w, ada_b.reshape(1, cols))


def _inproj_kernel(x_ref, g_ref, scale_ref, shift_ref, wr_ref, wa_ref, wg_ref,
                   pr_ref, pg_ref, a0_ref, a1_ref, a2_ref, h_ref):
    x = x_ref[...]
    tm = x.shape[0]
    ms = jnp.mean(x * x, axis=-1, keepdims=True)
    h = x * lax.rsqrt(ms + NORM_EPS) * (g_ref[...] * (1.0 + scale_ref[0])) + shift_ref[0]
    ntile = h_ref.shape[0]
    for c in range(ntile):
        h_ref[c] = h[:, c * LANES:(c + 1) * LANES]
    hb = h.astype(BF16)
    step = 512
    gcols = 3 * GROUP_WIDTH
    for w_ref, o_ref, cols in ((wr_ref, pr_ref, RWKV_COLS), (wg_ref, pg_ref, GATE_COLS),
                               (wa_ref, a0_ref, gcols)):
        for j in range(0, cols, step):
            width = min(step, cols - j)
            o_ref[:, j:j + width] = _dot(hb, w_ref[:, j:j + width]).astype(BF16)
    for gi, a_ref in ((1, a1_ref), (2, a2_ref)):
        dil = ATTN_GROUPS[gi][1]
        run = tm // dil
        hp = jnp.concatenate(
            [jnp.concatenate([h_ref[c, pl.ds(z, run, stride=dil), :] for z in range(dil)], axis=0)
             for c in range(ntile)], axis=1).astype(BF16)
        res = _dot(hp, wa_ref[:, gi * gcols:(gi + 1) * gcols]).astype(BF16)
        for z in range(dil):
            a_ref[z] = res[z * run:(z + 1) * run]


def _inproj(x2, norm_g, scale, shift, w_r, w_a, w_g, bsz, seq):
    n, d = x2.shape
    tm = TOKEN_TILE
    per = seq // tm
    gcols = 3 * GROUP_WIDTH
    const = lambda shape: pl.BlockSpec(shape, lambda i: (0, 0), pipeline_mode=pl.Buffered(1))
    row = lambda width: pl.BlockSpec((tm, width), lambda i: (i, 0))
    dil1, dil2 = ATTN_GROUPS[1][1], ATTN_GROUPS[2][1]
    res_spec = lambda dil: pl.BlockSpec((None, dil, None, tm // dil, gcols),
                                        lambda i: (i // per, 0, i % per, 0, 0))
    res_shape = lambda dil: jax.ShapeDtypeStruct((bsz, dil, per, tm // dil, gcols), BF16)
    pr, pg, a0, a1, a2 = pl.pallas_call(
        _inproj_kernel,
        grid=(n // tm,),
        in_specs=[row(d),
                  pl.BlockSpec((1, d), lambda i: (0, 0)),
                  pl.BlockSpec((1, 1, d), lambda i: (i // per, 0, 0)),
                  pl.BlockSpec((1, 1, d), lambda i: (i // per, 0, 0)),
                  const(w_r.shape), const(w_a.shape), const(w_g.shape)],
        out_specs=[row(RWKV_COLS), row(GATE_COLS), row(gcols), res_spec(dil1), res_spec(dil2)],
        out_shape=[jax.ShapeDtypeStruct((n, RWKV_COLS), BF16),
                   jax.ShapeDtypeStruct((n, GATE_COLS), BF16),
                   jax.ShapeDtypeStruct((n, gcols), BF16),
                   res_shape(dil1), res_shape(dil2)],
        scratch_shapes=[pltpu.VMEM((d // LANES, tm, LANES), F32)],
        compiler_params=_cparams(("parallel",)),
        name="inproj",
    )(x2, norm_g, scale, shift, w_r, w_a, w_g)
    qkv = [a.reshape(bsz, seq, gcols) for a in (a0, a1, a2)]
    return pr, pg, qkv


_P_W0, _P_A0, _P_KK, _P_KA, _P_RK, _P_LNW, _P_LNB, _P_MUR, _P_MUK, _P_MUV = range(10)


def _rwkv_kernel(pr_ref, par_ref, wl_ref, ones_ref, o_ref, state_ref, prev_ref, y_ref):
    c_idx = pl.program_id(1)

    @pl.when(c_idx == 0)
    def _():
        state_ref[...] = jnp.zeros_like(state_ref)
        prev_ref[...] = jnp.zeros_like(prev_ref)

    rw = RWKV_WIDTH
    par = lambda i: par_ref[i:i + 1, :]
    ones_bd = ones_ref[...]

    def headsum(a):
        hi, lo = _split2(a)
        return _dot(hi, ones_bd) + _dot(lo, ones_bd)

    pr = pr_ref[...].astype(F32)
    row = lax.broadcasted_iota(jnp.int32, (CHUNK, 1), 0)
    prev_row = prev_ref[0:1, :]

    def shifted(lo, hi):
        return jnp.where(row == 0, prev_row[:, lo:hi], pltpu.roll(pr[:, lo:hi], 1, 0))

    sh_rkv = shifted(0, 3 * rw)
    sh_lora = shifted(3 * rw + LORA_WIDTH, 3 * rw + 2 * LORA_WIDTH)
    prev_ref[0:1, :] = pr[CHUNK - 1:CHUNK, :]

    lora = pr[:, 3 * rw:3 * rw + LORA_WIDTH] + sh_lora
    lane = lax.broadcasted_iota(jnp.int32, (1, LORA_WIDTH), 1)
    act = jnp.where(lane < DECAY_LORA, jnp.tanh(lora),
                    jnp.where(lane < DECAY_LORA + ICLR_LORA, lora, _sigmoid(lora)))
    lo_out = _dot(act.astype(BF16), wl_ref[...])
    w_pre = par(_P_W0) + lo_out[:, :rw]
    z = -w_pre
    softplus = jnp.maximum(z, 0.0) + jnp.log(1.0 + jnp.exp(-jnp.abs(z)))
    logd = -jnp.exp(-softplus - 0.5)
    a = _sigmoid(par(_P_A0) + lo_out[:, rw:2 * rw])
    g = lo_out[:, 2 * rw:3 * rw]

    r0, k0, v0 = pr[:, :rw], pr[:, rw:2 * rw], pr[:, 2 * rw:3 * rw]
    r = r0 + (sh_rkv[:, :rw] - r0) * par(_P_MUR)
    k = k0 + (sh_rkv[:, rw:2 * rw] - k0) * par(_P_MUK)
    v = v0 + (sh_rkv[:, 2 * rw:] - v0) * par(_P_MUV)

    kk = k * par(_P_KK)
    kk = kk / jnp.maximum(jnp.sqrt(headsum(kk * kk)), 1e-12)
    k2 = k * (1.0 + (a - 1.0) * par(_P_KA))

    ti = lax.broadcasted_iota(jnp.int32, (CHUNK, CHUNK), 0)
    tj = lax.broadcasted_iota(jnp.int32, (CHUNK, CHUNK), 1)
    low_incl = ti >= tj
    low_strict = ti > tj
    tril = jnp.where(low_incl, 1.0, 0.0).astype(BF16)
    d_hi, d_mid, d_lo = _split3(logd)
    cum = _dot(tril, d_hi) + _dot(tril, d_mid) + _dot(tril, d_lo)
    p_incl = jnp.exp(cum)
    p_excl = jnp.exp(cum - logd)
    p_inv = jnp.exp(-cum)

    a_t = (-kk * p_excl).astype(BF16)
    b_t = (kk * a * p_inv).astype(BF16)
    k_t = (k2 * p_inv).astype(BF16)
    r_t = (r * p_incl).astype(BF16)
    v_b = v.astype(BF16)
    eye = jnp.where(ti == tj, 1.0, 0.0)

    heads = range(RWKV_HEADS)
    sls = [slice(h * HEAD_DIM, (h + 1) * HEAD_DIM) for h in heads]
    ar = [jnp.concatenate([a_t[:, sl], r_t[:, sl]], axis=0) for sl in sls]
    bk = [jnp.concatenate([b_t[:, sl], k_t[:, sl]], axis=0) for sl in sls]
    vh = [v_b[:, sl] for sl in sls]
    s0 = [state_ref[h] for h in heads]
    m = [_dot_nt(ar[h], bk[h]) for h in heads]
    ars = [_dot_nt(ar[h], s0[h].astype(BF16)) for h in heads]
    a_ab = [jnp.where(low_strict, m[h][:CHUNK, :CHUNK], 0.0) for h in heads]
    a_ak = [jnp.where(low_strict, m[h][:CHUNK, CHUNK:], 0.0).astype(BF16) for h in heads]
    a_rb = [jnp.where(low_incl, m[h][CHUNK:, :CHUNK], 0.0).astype(BF16) for h in heads]
    a_rk = [jnp.where(low_incl, m[h][CHUNK:, CHUNK:], 0.0).astype(BF16) for h in heads]
    t_inv = [eye + a_ab[h] for h in heads]
    pw = [a_ab[h].astype(BF16) for h in heads]
    for _ in range(int(math.log2(CHUNK)) - 1):
        pw = [_dot(pw[h], pw[h]).astype(BF16) for h in heads]
        t_inv = [t_inv[h] + _dot(pw[h], t_inv[h].astype(BF16)) for h in heads]
    rhs = [ars[h][:CHUNK] + _dot(a_ak[h], vh[h]) for h in heads]
    ub = [_dot(t_inv[h].astype(BF16), rhs[h].astype(BF16)).astype(BF16) for h in heads]
    for h in heads:
        y_ref[:, sls[h]] = ars[h][CHUNK:] + _dot(a_rb[h], ub[h]) + _dot(a_rk[h], vh[h])
    for h in heads:
        uv = jnp.concatenate([ub[h], vh[h]], axis=0)
        state_ref[h] = (s0[h] + _dot_tn(uv, bk[h])) * p_incl[CHUNK - 1:CHUNK, sls[h]]

    y = y_ref[...]
    inv_n = 1.0 / HEAD_DIM
    mu = headsum(y) * inv_n
    dlt = y - mu
    var = headsum(dlt * dlt) * inv_n
    yn = dlt * lax.rsqrt(var + GROUPNORM_EPS) * par(_P_LNW) + par(_P_LNB)
    bonus = headsum(r * k2 * par(_P_RK)) * v
    o_ref[...] = ((yn + bonus) * g).astype(BF16)


def _rwkv(pr, par, w_lora, ones_bd, bsz, seq):
    n = pr.shape[0]
    nchunk = seq // CHUNK
    const = lambda shape: pl.BlockSpec(shape, lambda b, c: (0, 0))
    return pl.pallas_call(
        _rwkv_kernel,
        grid=(bsz, nchunk),
        in_specs=[pl.BlockSpec((CHUNK, RWKV_COLS), lambda b, c: (b * nchunk + c, 0)),
                  const(par.shape), const(w_lora.shape), const(ones_bd.shape)],
        out_specs=pl.BlockSpec((CHUNK, RWKV_WIDTH), lambda b, c: (b * nchunk + c, 0)),
        out_shape=jax.ShapeDtypeStruct((n, RWKV_WIDTH), BF16),
        scratch_shapes=[pltpu.VMEM((RWKV_HEADS, HEAD_DIM, HEAD_DIM), F32),
                        pltpu.VMEM((8, RWKV_COLS), F32),
                        pltpu.VMEM((CHUNK, RWKV_WIDTH), F32)],
        compiler_params=_cparams(("parallel", "arbitrary")),
        name="rwkv",
    )(pr, par, w_lora, ones_bd)


def _attn_kernel(a0_ref, a1_ref, a2_ref, bias_ref, gain_ref, bd_ref, out_ref,
                 qkv_ref, m_ref, s_ref, o_ref):
    seq = a0_ref.shape[0]
    ntile = m_ref.shape[0]
    gw = GROUP_WIDTH
    blk = ATTN_BLOCK
    bd = bd_ref[...]
    gq = gain_ref[0:1, :] * (HEAD_DIM ** -0.5)
    gk = gain_ref[1:2, :]
    lane_head = lax.broadcasted_iota(jnp.int32, (1, gw), 1) // HEAD_DIM
    qkv_ref[0:blk, :] = jnp.zeros((blk, 3 * gw), BF16)

    def headwise(parts):
        acc = parts[0]
        for h in range(1, HEADS_PER_GROUP):
            acc = jnp.where(lane_head == h, parts[h], acc)
        return acc

    for gi, a_ref in enumerate((a0_ref, a1_ref, a2_ref)):
        dil = ATTN_GROUPS[gi][1]
        nb = seq // dil // blk

        def normalise(c, carry, a_ref=a_ref):
            r0 = pl.multiple_of(c * 256, 256)
            for part, gain in ((0, gq), (1, gk)):
                t = a_ref[pl.ds(r0, 256), part * gw:(part + 1) * gw].astype(F32)
                hi, lo = _split2(t * t)
                ms = _dot(hi, bd) + _dot(lo, bd)
                qkv_ref[pl.ds(blk + r0, 256), part * gw:(part + 1) * gw] = (
                    t * lax.rsqrt(ms + NORM_EPS) * gain).astype(BF16)
            qkv_ref[pl.ds(blk + r0, 256), 2 * gw:] = a_ref[pl.ds(r0, 256), 2 * gw:]
            return carry

        lax.fori_loop(0, seq // 256, normalise, 0)

        def block(t, carry, gi=gi, dil=dil, nb=nb):
            z = t // nb
            n = t % nb
            r0 = pl.multiple_of(t * blk, blk)
            first = jnp.where(n == 0, 1, 0)
            q = qkv_ref[pl.ds(blk + r0, blk), 0:gw]
            kwin = qkv_ref[pl.ds(r0, 2 * blk), gw:2 * gw]
            vwin = qkv_ref[pl.ds(r0, 2 * blk), 2 * gw:]
            heads = range(HEADS_PER_GROUP)
            qm = [jnp.where(lane_head == h, q, jnp.zeros_like(q)) for h in heads]
            s = [_dot_nt(qm[h], kwin) + bias_ref[gi, first, h] for h in heads]
            m = [jnp.max(s[h], axis=-1, keepdims=True) for h in heads]
            p = [jnp.exp(s[h] - m[h]) for h in heads]
            den = [jnp.sum(p[h], axis=-1, keepdims=True) for h in heads]
            pv = [_dot(p[h].astype(BF16), vwin) for h in heads]
            o_new = headwise(pv) / headwise([jnp.broadcast_to(den[h], (blk, gw)) for h in heads])
            lse = headwise([jnp.broadcast_to(m[h] + jnp.log(den[h]), (blk, gw)) for h in heads])
            rows = pl.ds(z + n * (blk * dil), blk, stride=dil) if dil > 1 else pl.ds(r0, blk)
            ld = lambda ref: jnp.concatenate([ref[c, rows, :] for c in range(ntile)], axis=1)

            def st(ref, val):
                for c in range(ntile):
                    ref[c, rows, :] = val[:, c * LANES:(c + 1) * LANES]

            if gi == 0:
                st(m_ref, lse)
                st(o_ref, o_new)
            else:
                m_old = ld(m_ref)
                m_new = jnp.maximum(m_old, lse)
                wa = jnp.exp(m_old - m_new)
                wb = jnp.exp(lse - m_new)
                mix = wa * ld(o_ref) + wb * o_new
                if gi == 1:
                    st(m_ref, m_new)
                    st(s_ref, wa + wb)
                    st(o_ref, mix)
                else:
                    st(o_ref, mix / (wa * ld(s_ref) + wb))
            return carry

        lax.fori_loop(0, seq // blk, block, 0)

    out_ref[...] = jnp.concatenate([o_ref[c] for c in range(ntile)], axis=1).astype(out_ref.dtype)


def _attention(qkv, bias, gains, bd, bsz, seq):
    gcols = 3 * GROUP_WIDTH
    const = lambda a: pl.BlockSpec(a.shape, lambda b: (0,) * a.ndim)
    seq_spec = pl.BlockSpec((None, seq, gcols), lambda b: (b, 0, 0))
    out = pl.pallas_call(
        _attn_kernel,
        grid=(bsz,),
        in_specs=[seq_spec, seq_spec, seq_spec, const(bias), const(gains), const(bd)],
        out_specs=pl.BlockSpec((None, seq, GROUP_WIDTH), lambda b: (b, 0, 0)),
        out_shape=jax.ShapeDtypeStruct((bsz, seq, GROUP_WIDTH), BF16),
        scratch_shapes=[pltpu.VMEM((ATTN_BLOCK + seq, gcols), BF16)]
        + [pltpu.VMEM((GROUP_WIDTH // LANES, seq, LANES), F32)] * 3,
        compiler_params=_cparams(("parallel",)),
        name="attn",
    )(*qkv, bias, gains, bd)
    return out.reshape(bsz * seq, GROUP_WIDTH)


def _t5_bucket(dist):
    max_exact = N_BUCKETS // 2
    large = max_exact + (jnp.log(jnp.maximum(dist, max_exact).astype(F32) / max_exact)
                         / math.log(MAX_DISTANCE / max_exact) * (N_BUCKETS - max_exact)).astype(jnp.int32)
    return jnp.where(dist < max_exact, dist, jnp.minimum(large, N_BUCKETS - 1))


def _band_bias(rel_bias, gi):
    window, dil = ATTN_GROUPS[gi]
    qi = jnp.arange(ATTN_BLOCK)[:, None]
    kj = jnp.arange(2 * ATTN_BLOCK)[None, :]
    steps = qi - kj + ATTN_BLOCK
    heads = slice(gi * HEADS_PER_GROUP, (gi + 1) * HEADS_PER_GROUP)
    bucket = _t5_bucket(jnp.maximum(steps, 0) * dil)
    onehot = (bucket[None] == jnp.arange(N_BUCKETS)[:, None, None]).astype(F32)
    bias = jnp.sum(onehot[:, None] * rel_bias[:, heads][:, :, None, None], axis=0)
    band = (steps >= 0) & (steps <= window // dil)
    later = jnp.where(band[None], bias.astype(F32), NEG_INF)
    first = jnp.where((kj >= ATTN_BLOCK)[None], later, NEG_INF)
    return jnp.stack([later, first])


def _post_kernel(x_ref, yr_ref, att_ref, pg_ref,
                 gate1_ref, scale2_ref, shift2_ref, g2_ref, wbr_ref, wba_ref, wout_ref,
                 rwh_ref, rwl_ref, rb_ref,
                 x1_ref, h2_ref, gates_ref, eidx_ref, rank_ref, counts_ref, run_ref):
    i = pl.program_id(0)
    tm = x_ref.shape[0]

    @pl.when(i == 0)
    def _():
        run_ref[...] = jnp.zeros_like(run_ref)

    y_r = _dot(yr_ref[...], wbr_ref[...])
    y_a = _dot(att_ref[...], wba_ref[...])
    d = D_MODEL
    mixed = (_sigmoid(pg_ref[:, :d].astype(F32)) * y_r
             + _sigmoid(pg_ref[:, d:].astype(F32)) * y_a)
    x1 = x_ref[...] + gate1_ref[0] * _dot(mixed.astype(BF16), wout_ref[...])
    x1_ref[...] = x1

    ms = jnp.mean(x1 * x1, axis=-1, keepdims=True)
    h2 = x1 * lax.rsqrt(ms + NORM_EPS) * (g2_ref[...] * (1.0 + scale2_ref[0])) + shift2_ref[0]
    h2_ref[...] = h2

    h_hi, h_lo = _split2(h2)
    rw_hi, rw_lo = rwh_ref[...], rwl_ref[...]
    logits = (_dot_nt(rw_hi, h_hi) + _dot_nt(rw_hi, h_lo) + _dot_nt(rw_lo, h_hi)
              + rb_ref[:, 0:1])
    eid = lax.broadcasted_iota(jnp.int32, (N_EXPERTS, tm), 0)
    vals, hots = [], []
    lg = logits
    for k in range(TOP_K):
        m = jnp.max(lg, axis=0, keepdims=True)
        idx = jnp.min(jnp.where(lg == m, eid, N_EXPERTS), axis=0, keepdims=True)
        hot = eid == idx
        vals.append(m)
        hots.append(hot)
        eidx_ref[k:k + 1, :] = idx
        lg = jnp.where(hot, -jnp.inf, lg)
    exps = [jnp.exp(v - vals[0]) for v in vals]
    tot = exps[0] + exps[1] + exps[2] + exps[3]
    for k in range(TOP_K):
        gates_ref[k:k + 1, :] = exps[k] / tot

    chosen = jnp.zeros((N_EXPERTS, tm), F32)
    for hot in hots:
        chosen = chosen + jnp.where(hot, 1.0, 0.0)
    ti = lax.broadcasted_iota(jnp.int32, (tm, tm), 0)
    tj = lax.broadcasted_iota(jnp.int32, (tm, tm), 1)
    before = jnp.where(ti < tj, 1.0, 0.0).astype(BF16)
    run = run_ref[:, 0:1]
    base = _dot(chosen.astype(BF16), before) + run
    for k in range(TOP_K):
        rank_ref[k:k + 1, :] = jnp.sum(jnp.where(hots[k], base, 0.0), axis=0,
                                       keepdims=True).astype(jnp.int32)
    run_new = run + jnp.sum(chosen, axis=1, keepdims=True)
    run_ref[...] = jnp.broadcast_to(run_new, run_ref.shape)
    counts_ref[...] = jnp.broadcast_to(run_new, counts_ref.shape).astype(jnp.int32)


def _post(x2, y_rwkv, att, pg, gate1, scale2, shift2, norm2_g, w_br_r, w_br_a, w_out,
          rw_hi, rw_lo, rb, seq):
    n, d = x2.shape
    tm = TOKEN_TILE
    per = seq // tm
    row = lambda width: pl.BlockSpec((tm, width), lambda i: (i, 0))
    const = lambda shape: pl.BlockSpec(shape, lambda i: (0,) * len(shape))
    bvec = pl.BlockSpec((1, 1, d), lambda i: (i // per, 0, 0))
    tcol = pl.BlockSpec((TOP_K, tm), lambda i: (0, i))
    return pl.pallas_call(
        _post_kernel,
        grid=(n // tm,),
        in_specs=[row(d), row(RWKV_WIDTH), row(GROUP_WIDTH), row(GATE_COLS),
                  bvec, bvec, bvec, const((1, d)), const(w_br_r.shape), const(w_br_a.shape),
                  const(w_out.shape), const(rw_hi.shape), const(rw_lo.shape), const(rb.shape)],
        out_specs=[row(d), row(d), tcol, tcol, tcol, const((N_EXPERTS, 128))],
        out_shape=[jax.ShapeDtypeStruct((n, d), F32),
                   jax.ShapeDtypeStruct((n, d), F32),
                   jax.ShapeDtypeStruct((TOP_K, n), F32),
                   jax.ShapeDtypeStruct((TOP_K, n), jnp.int32),
                   jax.ShapeDtypeStruct((TOP_K, n), jnp.int32),
                   jax.ShapeDtypeStruct((N_EXPERTS, 128), jnp.int32)],
        scratch_shapes=[pltpu.VMEM((N_EXPERTS, 128), F32)],
        compiler_params=_cparams(("arbitrary",)),
        name="post",
    )(x2, y_rwkv, att, pg, gate1, scale2, shift2, norm2_g, w_br_r, w_br_a, w_out,
      rw_hi, rw_lo, rb)


def _row_copy(src, src_row, dst, dst_row, sem):
    return pltpu.make_async_copy(src.at[pl.ds(src_row, 1)], dst.at[pl.ds(dst_row, 1)], sem)


def _dispatch_kernel(meta_ref, pos_ref, h2_ref, xs_ref, zero_ref, sem):
    i = pl.program_id(0)
    tm = h2_ref.shape[0]
    nblk = xs_ref.shape[0] // MOE_ROWS

    @pl.when(i == 0)
    def _():
        zero_ref[...] = jnp.zeros_like(zero_ref)

        def fill(start):
            cp = pltpu.make_async_copy(zero_ref, xs_ref.at[pl.ds(start, MOE_ROWS)], sem)
            cp.start()
            cp.wait()

        for e in range(N_EXPERTS):
            end = meta_ref[1 + e]
            begin = meta_ref[e] if e else 0

            @pl.when(end > begin)
            def _():
                fill(pl.multiple_of(end - MOE_ROWS, MOE_ROWS))

        def tail(j, carry):
            fill(pl.multiple_of(j * MOE_ROWS, MOE_ROWS))
            return carry

        lax.fori_loop(meta_ref[0], nblk, tail, 0)

    def issue(t, carry):
        for k in range(TOP_K):
            _row_copy(h2_ref, t, xs_ref, pos_ref[k, t], sem).start()
        return carry

    lax.fori_loop(0, tm, issue, 0)
    for k in range(TOP_K):
        pltpu.make_async_copy(h2_ref, xs_ref.at[pl.ds(0, tm)], sem).wait()


def _dispatch(meta, pos, h2, nblk):
    n, d = h2.shape
    tm = TOKEN_TILE
    return pl.pallas_call(
        _dispatch_kernel,
        grid_spec=pltpu.PrefetchScalarGridSpec(
            num_scalar_prefetch=1,
            grid=(n // tm,),
            in_specs=[pl.BlockSpec((TOP_K, tm), lambda i, meta: (0, i), memory_space=pltpu.SMEM),
                      pl.BlockSpec((tm, d), lambda i, meta: (i, 0))],
            out_specs=pl.BlockSpec(memory_space=pl.ANY),
            scratch_shapes=[pltpu.VMEM((MOE_ROWS, d), h2.dtype),
                            pltpu.SemaphoreType.DMA(())]),
        out_shape=jax.ShapeDtypeStruct((nblk * MOE_ROWS, d), h2.dtype),
        compiler_params=_cparams(("arbitrary",)),
        name="dispatch",
    )(meta, pos, h2)


def _expert_kernel(blk_ref, nv_ref, xs_ref, w1g_ref, w1l_ref, b1g_ref, b1l_ref, w2_ref, b2_ref,
                   o_ref):
    j = pl.program_id(0)

    @pl.when(j < nv_ref[0])
    def _():
        x = xs_ref[...].astype(BF16)
        glu = jnp.minimum(_dot(x, w1g_ref[...]) + b1g_ref[...], SWIGLU_LIMIT)
        lin = jnp.clip(_dot(x, w1l_ref[...]) + b1l_ref[...], -SWIGLU_LIMIT, SWIGLU_LIMIT)
        act = glu * _sigmoid(SWIGLU_ALPHA * glu) * (lin + 1.0)
        o_ref[...] = _dot(act.astype(BF16), w2_ref[...]) + b2_ref[...]

    @pl.when(j >= nv_ref[0])
    def _():
        o_ref[...] = jnp.zeros_like(o_ref)


def _experts(blk_e, nvalid, xs, w1g, w1l, b1g, b1l, w2, b2):
    rows, d = xs.shape
    nblk = rows // MOE_ROWS
    de = w1g.shape[2]
    xmap = lambda j, be, nv: (jnp.minimum(j, nv[0] - 1), 0)
    emap = lambda j, be, nv: (be[j], 0, 0)
    return pl.pallas_call(
        _expert_kernel,
        grid_spec=pltpu.PrefetchScalarGridSpec(
            num_scalar_prefetch=2,
            grid=(nblk,),
            in_specs=[pl.BlockSpec((MOE_ROWS, d), xmap),
                      pl.BlockSpec((None, d, de), emap), pl.BlockSpec((None, d, de), emap),
                      pl.BlockSpec((None, 1, de), emap), pl.BlockSpec((None, 1, de), emap),
                      pl.BlockSpec((None, de, d), emap), pl.BlockSpec((None, 1, d), emap)],
            out_specs=pl.BlockSpec((MOE_ROWS, d), lambda j, be, nv: (j, 0))),
        out_shape=jax.ShapeDtypeStruct((rows, d), F32),
        compiler_params=_cparams(("arbitrary",)),
        name="experts",
    )(blk_e, nvalid, xs, w1g, w1l, b1g, b1l, w2, b2)


def _split_w1_kernel(w_ref, perm_ref, g_ref, l_ref):
    t = MXU_TILE
    for cb in range(w_ref.shape[1] // t):
        res = _dot(w_ref[:, cb * t:(cb + 1) * t].astype(BF16), perm_ref[...])
        g_ref[:, cb * (t // 2):(cb + 1) * (t // 2)] = res[:, :t // 2].astype(BF16)
        l_ref[:, cb * (t // 2):(cb + 1) * (t // 2)] = res[:, t // 2:].astype(BF16)


def _split_w1(w1):
    ne, d, two_de = w1.shape
    t = MXU_TILE
    src = jnp.arange(t)[:, None]
    dst = jnp.arange(t)[None, :]
    perm = (src == jnp.where(dst < t // 2, 2 * dst, 2 * (dst - t // 2) + 1)).astype(BF16)
    return pl.pallas_call(
        _split_w1_kernel,
        grid=(ne,),
        in_specs=[pl.BlockSpec((None, d, two_de), lambda e: (e, 0, 0)),
                  pl.BlockSpec((t, t), lambda e: (0, 0))],
        out_specs=[pl.BlockSpec((None, d, two_de // 2), lambda e: (e, 0, 0)),
                   pl.BlockSpec((None, d, two_de // 2), lambda e: (e, 0, 0))],
        out_shape=[jax.ShapeDtypeStruct((ne, d, two_de // 2), BF16),
                   jax.ShapeDtypeStruct((ne, d, two_de // 2), BF16)],
        compiler_params=_cparams(("parallel",)),
        name="split_w1",
    )(w1, perm)


def _combine_kernel(pos_ref, x1_ref, gt_ref, gate2_ref, o_hbm, out_ref, buf_ref, sem):
    tm = x1_ref.shape[0]

    def issue(t, carry):
        for k in range(TOP_K):
            _row_copy(o_hbm, pos_ref[k, t], buf_ref.at[k], t, sem).start()
        return carry

    lax.fori_loop(0, tm, issue, 0)
    for k in range(TOP_K):
        pltpu.make_async_copy(o_hbm.at[pl.ds(0, tm)], buf_ref.at[k], sem).wait()
    gt = gt_ref[...]
    acc = buf_ref[0] * gt[:, 0:1]
    for k in range(1, TOP_K):
        acc = acc + buf_ref[k] * gt[:, k:k + 1]
    out_ref[...] = x1_ref[...] + gate2_ref[0] * acc


def _combine(pos, x1, gates_t, gate2, o_rows, seq):
    n, d = x1.shape
    tm = TOKEN_TILE
    per = seq // tm
    return pl.pallas_call(
        _combine_kernel,
        grid=(n // tm,),
        in_specs=[pl.BlockSpec((TOP_K, tm), lambda i: (0, i), memory_space=pltpu.SMEM),
                  pl.BlockSpec((tm, d), lambda i: (i, 0)),
                  pl.BlockSpec((tm, TOP_K), lambda i: (i, 0)),
                  pl.BlockSpec((1, 1, d), lambda i: (i // per, 0, 0)),
                  pl.BlockSpec(memory_space=pl.ANY)],
        out_specs=pl.BlockSpec((tm, d), lambda i: (i, 0)),
        out_shape=jax.ShapeDtypeStruct((n, d), F32),
        scratch_shapes=[pltpu.VMEM((TOP_K, tm, d), F32), pltpu.SemaphoreType.DMA(())],
        compiler_params=_cparams(("arbitrary",)),
        name="combine",
    )(pos, x1, gates_t, gate2, o_rows)


def _layer(x, c, ada_w, ada_b, norm1_g, norm2_g, w_in, mu_rkv, mu_wag, w0, w1, w2, a0, a1, a2,
           g1, g2, k_k, k_a, r_k, ln_w, ln_b, qn_g, kn_g, rel_bias, w_br_rwkv, w_br_attn, w_out,
           router_w, router_b, exp_w1, exp_b1, exp_w2, exp_b2):
    bsz, seq, d = x.shape
    n = bsz * seq
    rw = RWKV_WIDTH
    assert d == D_MODEL and seq % (ATTN_BLOCK * ATTN_GROUPS[-1][1]) == 0 and seq % TOKEN_TILE == 0

    mod = _modulation(c, ada_w, ada_b)
    shift1, scale1, gate1, shift2, scale2, gate2 = [
        m.reshape(bsz, 1, d) for m in jnp.split(mod, 6, axis=-1)]

    lora_w = jnp.concatenate([w1, a1, g1], axis=1)
    lora_mu = jnp.concatenate([jnp.broadcast_to(mu_wag[0][:, None], w1.shape),
                               jnp.broadcast_to(mu_wag[1][:, None], a1.shape),
                               jnp.broadcast_to(mu_wag[2][:, None], g1.shape)], axis=1)
    w_r = jnp.concatenate([w_in[:, :3 * rw], lora_w * (1.0 - lora_mu), lora_w * lora_mu],
                          axis=1).astype(BF16)
    qa0 = 3 * rw
    acols = []
    for gi in range(len(ATTN_GROUPS)):
        for part in range(3):
            lo = qa0 + part * ATTN_WIDTH + gi * GROUP_WIDTH
            acols.append(w_in[:, lo:lo + GROUP_WIDTH])
    w_a = jnp.concatenate(acols, axis=1).astype(BF16)
    w_g = w_in[:, qa0 + 3 * ATTN_WIDTH:].astype(BF16)

    x2 = x.reshape(n, d)
    pr, pg, qkv = _inproj(x2, norm1_g.reshape(1, d), scale1, shift1, w_r, w_a, w_g, bsz, seq)

    par = jnp.stack([w0, a0, k_k, k_a, r_k.reshape(rw), ln_w, ln_b,
                     mu_rkv[0], mu_rkv[1], mu_rkv[2]]
                    + [jnp.zeros((rw,), F32)] * 6)
    w_lora = jnp.concatenate([jnp.pad(w2, ((0, 0), (0, 2 * rw))),
                              jnp.pad(a2, ((0, 0), (rw, rw))),
                              jnp.pad(g2, ((0, 0), (2 * rw, 0)))], axis=0).astype(BF16)
    hid = jnp.arange(rw) // HEAD_DIM
    ones_bd = (hid[:, None] == hid[None, :]).astype(BF16)
    y_rwkv = _rwkv(pr, par, w_lora, ones_bd, bsz, seq)

    gains = jnp.pad(jnp.stack([jnp.tile(qn_g, HEADS_PER_GROUP), jnp.tile(kn_g, HEADS_PER_GROUP)]),
                    ((0, 6), (0, 0)))
    bias = jnp.stack([_band_bias(rel_bias, gi) for gi in range(len(ATTN_GROUPS))])
    ghead = jnp.arange(GROUP_WIDTH) // HEAD_DIM
    mean_bd = ((ghead[:, None] == ghead[None, :]).astype(F32) / HEAD_DIM).astype(BF16)
    att = _attention(qkv, bias, gains, mean_bd, bsz, seq)

    rw_t = router_w.T
    rw_hi = rw_t.astype(BF16)
    rw_lo = (rw_t - rw_hi.astype(F32)).astype(BF16)
    rb = jnp.broadcast_to(router_b[:, None], (N_EXPERTS, 128))
    x1, h2, gates, eidx, rank, counts = _post(
        x2, y_rwkv, att, pg, gate1, scale2, shift2, norm2_g.reshape(1, d),
        w_br_rwkv.astype(BF16), w_br_attn.astype(BF16), w_out.astype(BF16), rw_hi, rw_lo, rb, seq)

    cnt = counts[:, 0]
    padded = (cnt + MOE_ROWS - 1) // MOE_ROWS * MOE_ROWS
    pend = jnp.cumsum(padded)
    pstart = pend - padded
    nblk = (n * TOP_K) // MOE_ROWS + N_EXPERTS
    nvalid = pend[-1] // MOE_ROWS
    first_row = jnp.minimum(jnp.arange(nblk, dtype=jnp.int32), nvalid - 1) * MOE_ROWS
    blk_e = jnp.sum((pend[None, :] <= first_row[:, None]).astype(jnp.int32), axis=1)
    blk_e = jnp.clip(blk_e, 0, N_EXPERTS - 1)
    pos = jnp.sum(jnp.where(eidx[:, :, None] == jnp.arange(N_EXPERTS)[None, None, :],
                            pstart[None, None, :], 0), axis=-1) + rank
    pos = pos.astype(jnp.int32)
    meta = jnp.concatenate([nvalid[None], pend]).astype(jnp.int32)

    xs = _dispatch(meta, pos, h2, nblk)
    w1g, w1l = _split_w1(exp_w1)
    o_rows = _experts(blk_e, nvalid[None].astype(jnp.int32), xs, w1g, w1l,
                      exp_b1[:, None, 0::2], exp_b1[:, None, 1::2],
                      exp_w2.astype(BF16), exp_b2[:, None, :])
    out = _combine(pos, x1, gates.T, gate2, o_rows, seq)
    return out.reshape(bsz, seq, d)


def kernel(x, c, ada_w, ada_b, norm1_g, norm2_g, w_in, rwkv_mu_rkv, rwkv_mu_wag, rwkv_w0, rwkv_w1, rwkv_w2, rwkv_a0, rwkv_a1, rwkv_a2, rwkv_g1, rwkv_g2, rwkv_k_k, rwkv_k_a, rwkv_r_k, rwkv_ln_w, rwkv_ln_b, attn_qn_g, attn_kn_g, rel_bias, w_br_rwkv, w_br_attn, w_out, router_w, router_b, exp_w1, exp_b1, exp_w2, exp_b2):
    per_layer = (ada_w, ada_b, norm1_g, norm2_g, w_in, rwkv_mu_rkv, rwkv_mu_wag, rwkv_w0, rwkv_w1,
                 rwkv_w2, rwkv_a0, rwkv_a1, rwkv_a2, rwkv_g1, rwkv_g2, rwkv_k_k, rwkv_k_a,
                 rwkv_r_k, rwkv_ln_w, rwkv_ln_b, attn_qn_g, attn_kn_g)
    tail = (w_br_rwkv, w_br_attn, w_out, router_w, router_b, exp_w1, exp_b1, exp_w2, exp_b2)
    for layer in range(ada_w.shape[0]):
        head = [p[layer] for p in per_layer]
        rest = [p[layer] for p in tail]
        x = _layer(x, c, *head, rel_bias, *rest)
    return x
```

```python
import functools
import math

import jax
import jax.numpy as jnp
from jax import lax
from jax.experimental import pallas as pl
from jax.experimental.pallas import tpu as pltpu

F32 = jnp.float32
BF16 = jnp.bfloat16

D_MODEL = 1024
HEAD_DIM = 64
RWKV_HEADS = 8
RWKV_WIDTH = RWKV_HEADS * HEAD_DIM
DECAY_LORA = 64
ICLR_LORA = 64
GATE_LORA = 128
LORA_WIDTH = DECAY_LORA + ICLR_LORA + GATE_LORA
GROUPNORM_EPS = 64e-5
ATTN_GROUPS = ((128, 1), (512, 4), (2048, 16))
HEADS_PER_GROUP = 4
ATTN_HEADS = HEADS_PER_GROUP * len(ATTN_GROUPS)
ATTN_WIDTH = ATTN_HEADS * HEAD_DIM
GROUP_WIDTH = HEADS_PER_GROUP * HEAD_DIM
ATTN_BLOCK = 128
N_BUCKETS = 32
MAX_DISTANCE = 2048
N_EXPERTS = 32
TOP_K = 4
D_EXPERT = D_MODEL
SWIGLU_ALPHA = 1.702
SWIGLU_LIMIT = 7.0
NORM_EPS = 1e-6
NEG_INF = -1e30

RWKV_COLS = 3 * RWKV_WIDTH + 2 * LORA_WIDTH
ATTN_COLS = 3 * ATTN_WIDTH
GATE_COLS = 2 * D_MODEL

CHUNK = 64
TOKEN_TILE = 512
MOE_ROWS = 512
ROUTE_TILE = 256
SLAB = 64
ROW_ALIGN = 8
EXPERT_GROUP = 8
MXU_TILE = 256
LANES = 128
VMEM_LIMIT = 56 * 1024 * 1024


def _cparams(sem, vmem=VMEM_LIMIT):
    return pltpu.CompilerParams(dimension_semantics=sem, vmem_limit_bytes=vmem)


def _split2(a):
    hi = a.astype(BF16)
    lo = (a - hi.astype(F32)).astype(BF16)
    return hi, lo


def _split3(a):
    hi = a.astype(BF16)
    r = a - hi.astype(F32)
    mid = r.astype(BF16)
    lo = (r - mid.astype(F32)).astype(BF16)
    return hi, mid, lo


def _dot(a, b):
    return jnp.dot(a, b, preferred_element_type=F32)


def _dot_nt(a, b):
    return lax.dot_general(a, b, (((1,), (1,)), ((), ())), preferred_element_type=F32)


def _dot_tn(a, b):
    return lax.dot_general(a, b, (((0,), (0,)), ((), ())), preferred_element_type=F32)


def _sigmoid(x):
    return 1.0 / (1.0 + jnp.exp(-x))


def _mod_kernel(c_ref, w_ref, b_ref, o_ref):
    c = c_ref[...]
    s = c * _sigmoid(c)
    s_hi, s_lo = _split2(s)
    w_hi, w_lo = _split2(w_ref[...])
    o_ref[...] = _dot(s_hi, w_hi) + _dot(s_hi, w_lo) + _dot(s_lo, w_hi) + b_ref[...]


def _modulation(c, ada_w, ada_b):
    bsz, d = c.shape
    cols = ada_w.shape[1]
    tn = 1024
    return pl.pallas_call(
        _mod_kernel,
        grid=(cols // tn,),
        in_specs=[pl.BlockSpec((bsz, d), lambda j: (0, 0)),
                  pl.BlockSpec((d, tn), lambda j: (0, j)),
                  pl.BlockSpec((1, tn), lambda j: (0, j))],
        out_specs=pl.BlockSpec((bsz, tn), lambda j: (0, j)),
        out_shape=jax.ShapeDtypeStruct((bsz, cols), F32),
        compiler_params=_cparams(("arbitrary",)),
        name="mod",
    )(c, ada_w, ada_b.reshape(1, cols))


def _inproj_kernel(x_ref, g_ref, scale_ref, shift_ref, wr_ref, wa_ref, wg_ref,
                   pr_ref, pg_ref, a0_ref, a1_ref, a2_ref, h_ref):
    x = x_ref[...]
    tm = x.shape[0]
    ms = jnp.mean(x * x, axis=-1, keepdims=True)
    h = x * lax.rsqrt(ms + NORM_EPS) * (g_ref[...] * (1.0 + scale_ref[0])) + shift_ref[0]
    ntile = h_ref.shape[0]
    for c in range(ntile):
        h_ref[c] = h[:, c * LANES:(c + 1) * LANES]
    hb = h.astype(BF16)
    step = 512
    gcols = 3 * GROUP_WIDTH
    for w_ref, o_ref, cols in ((wr_ref, pr_ref, RWKV_COLS), (wg_ref, pg_ref, GATE_COLS),
                               (wa_ref, a0_ref, gcols)):
        for j in range(0, cols, step):
            width = min(step, cols - j)
            o_ref[:, j:j + width] = _dot(hb, w_ref[:, j:j + width]).astype(BF16)
    for gi, a_ref in ((1, a1_ref), (2, a2_ref)):
        dil = ATTN_GROUPS[gi][1]
        run = tm // dil
        hp = jnp.concatenate(
            [jnp.concatenate([h_ref[c, pl.ds(z, run, stride=dil), :] for z in range(dil)], axis=0)
             for c in range(ntile)], axis=1).astype(BF16)
        res = _dot(hp, wa_ref[:, gi * gcols:(gi + 1) * gcols]).astype(BF16)
        for z in range(dil):
            a_ref[z] = res[z * run:(z + 1) * run]


def _inproj(x2, norm_g, scale, shift, w_r, w_a, w_g, bsz, seq):
    n, d = x2.shape
    tm = TOKEN_TILE
    per = seq // tm
    gcols = 3 * GROUP_WIDTH
    const = lambda shape: pl.BlockSpec(shape, lambda i: (0, 0), pipeline_mode=pl.Buffered(1))
    row = lambda width: pl.BlockSpec((tm, width), lambda i: (i, 0))
    dil1, dil2 = ATTN_GROUPS[1][1], ATTN_GROUPS[2][1]
    res_spec = lambda dil: pl.BlockSpec((None, dil, None, tm // dil, gcols),
                                        lambda i: (i // per, 0, i % per, 0, 0))
    res_shape = lambda dil: jax.ShapeDtypeStruct((bsz, dil, per, tm // dil, gcols), BF16)
    pr, pg, a0, a1, a2 = pl.pallas_call(
        _inproj_kernel,
        grid=(n // tm,),
        in_specs=[row(d),
                  pl.BlockSpec((1, d), lambda i: (0, 0)),
                  pl.BlockSpec((1, 1, d), lambda i: (i // per, 0, 0)),
                  pl.BlockSpec((1, 1, d), lambda i: (i // per, 0, 0)),
                  const(w_r.shape), const(w_a.shape), const(w_g.shape)],
        out_specs=[row(RWKV_COLS), row(GATE_COLS), row(gcols), res_spec(dil1), res_spec(dil2)],
        out_shape=[jax.ShapeDtypeStruct((n, RWKV_COLS), BF16),
                   jax.ShapeDtypeStruct((n, GATE_COLS), BF16),
                   jax.ShapeDtypeStruct((n, gcols), BF16),
                   res_shape(dil1), res_shape(dil2)],
        scratch_shapes=[pltpu.VMEM((d // LANES, tm, LANES), F32)],
        compiler_params=_cparams(("parallel",)),
        name="inproj",
    )(x2, norm_g, scale, shift, w_r, w_a, w_g)
    qkv = [a.reshape(bsz, seq, gcols) for a in (a0, a1, a2)]
    return pr, pg, qkv


_P_W0, _P_A0, _P_KK, _P_KA, _P_RK, _P_LNW, _P_LNB, _P_MUR, _P_MUK, _P_MUV = range(10)


def _rwkv_kernel(pr_ref, par_ref, wl_ref, ones_ref, o_ref, state_ref, prev_ref, y_ref):
    c_idx = pl.program_id(1)

    @pl.when(c_idx == 0)
    def _():
        state_ref[...] = jnp.zeros_like(state_ref)
        prev_ref[...] = jnp.zeros_like(prev_ref)

    rw = RWKV_WIDTH
    par = lambda i: par_ref[i:i + 1, :]
    ones_bd = ones_ref[...]

    def headsum(a):
        hi, lo = _split2(a)
        return _dot(hi, ones_bd) + _dot(lo, ones_bd)

    pr = pr_ref[...].astype(F32)
    row = lax.broadcasted_iota(jnp.int32, (CHUNK, 1), 0)
    prev_row = prev_ref[0:1, :]

    def shifted(lo, hi):
        return jnp.where(row == 0, prev_row[:, lo:hi], pltpu.roll(pr[:, lo:hi], 1, 0))

    sh_rkv = shifted(0, 3 * rw)
    sh_lora = shifted(3 * rw + LORA_WIDTH, 3 * rw + 2 * LORA_WIDTH)
    prev_ref[0:1, :] = pr[CHUNK - 1:CHUNK, :]

    lora = pr[:, 3 * rw:3 * rw + LORA_WIDTH] + sh_lora
    lane = lax.broadcasted_iota(jnp.int32, (1, LORA_WIDTH), 1)
    act = jnp.where(lane < DECAY_LORA, jnp.tanh(lora),
                    jnp.where(lane < DECAY_LORA + ICLR_LORA, lora, _sigmoid(lora)))
    lo_out = _dot(act.astype(BF16), wl_ref[...])
    w_pre = par(_P_W0) + lo_out[:, :rw]
    z = -w_pre
    softplus = jnp.maximum(z, 0.0) + jnp.log(1.0 + jnp.exp(-jnp.abs(z)))
    logd = -jnp.exp(-softplus - 0.5)
    a = _sigmoid(par(_P_A0) + lo_out[:, rw:2 * rw])
    g = lo_out[:, 2 * rw:3 * rw]

    r0, k0, v0 = pr[:, :rw], pr[:, rw:2 * rw], pr[:, 2 * rw:3 * rw]
    r = r0 + (sh_rkv[:, :rw] - r0) * par(_P_MUR)
    k = k0 + (sh_rkv[:, rw:2 * rw] - k0) * par(_P_MUK)
    v = v0 + (sh_rkv[:, 2 * rw:] - v0) * par(_P_MUV)

    kk = k * par(_P_KK)
    kk = kk / jnp.maximum(jnp.sqrt(headsum(kk * kk)), 1e-12)
    k2 = k * (1.0 + (a - 1.0) * par(_P_KA))

    ti = lax.broadcasted_iota(jnp.int32, (CHUNK, CHUNK), 0)
    tj = lax.broadcasted_iota(jnp.int32, (CHUNK, CHUNK), 1)
    low_incl = ti >= tj
    low_strict = ti > tj
    tril = jnp.where(low_incl, 1.0, 0.0).astype(BF16)
    d_hi, d_mid, d_lo = _split3(logd)
    cum = _dot(tril, d_hi) + _dot(tril, d_mid) + _dot(tril, d_lo)
    p_incl = jnp.exp(cum)
    p_excl = jnp.exp(cum - logd)
    p_inv = jnp.exp(-cum)

    a_t = (-kk * p_excl).astype(BF16)
    b_t = (kk * a * p_inv).astype(BF16)
    k_t = (k2 * p_inv).astype(BF16)
    r_t = (r * p_incl).astype(BF16)
    v_b = v.astype(BF16)
    eye = jnp.where(ti == tj, 1.0, 0.0)

    heads = range(RWKV_HEADS)
    sls = [slice(h * HEAD_DIM, (h + 1) * HEAD_DIM) for h in heads]
    ar = [jnp.concatenate([a_t[:, sl], r_t[:, sl]], axis=0) for sl in sls]
    bk = [jnp.concatenate([b_t[:, sl], k_t[:, sl]], axis=0) for sl in sls]
    vh = [v_b[:, sl] for sl in sls]
    s0 = [state_ref[h] for h in heads]
    m = [_dot_nt(ar[h], bk[h]) for h in heads]
    ars = [_dot_nt(ar[h], s0[h].astype(BF16)) for h in heads]
    a_ab = [jnp.where(low_strict, m[h][:CHUNK, :CHUNK], 0.0) for h in heads]
    a_ak = [jnp.where(low_strict, m[h][:CHUNK, CHUNK:], 0.0).astype(BF16) for h in heads]
    a_rb = [jnp.where(low_incl, m[h][CHUNK:, :CHUNK], 0.0).astype(BF16) for h in heads]
    a_rk = [jnp.where(low_incl, m[h][CHUNK:, CHUNK:], 0.0).astype(BF16) for h in heads]
    t_inv = [eye + a_ab[h] for h in heads]
    pw = [a_ab[h].astype(BF16) for h in heads]
    for _ in range(int(math.log2(CHUNK)) - 1):
        pw = [_dot(pw[h], pw[h]).astype(BF16) for h in heads]
        t_inv = [t_inv[h] + _dot(pw[h], t_inv[h].astype(BF16)) for h in heads]
    rhs = [ars[h][:CHUNK] + _dot(a_ak[h], vh[h]) for h in heads]
    ub = [_dot(t_inv[h].astype(BF16), rhs[h].astype(BF16)).astype(BF16) for h in heads]
    for h in heads:
        y_ref[:, sls[h]] = ars[h][CHUNK:] + _dot(a_rb[h], ub[h]) + _dot(a_rk[h], vh[h])
    for h in heads:
        uv = jnp.concatenate([ub[h], vh[h]], axis=0)
        state_ref[h] = (s0[h] + _dot_tn(uv, bk[h])) * p_incl[CHUNK - 1:CHUNK, sls[h]]

    y = y_ref[...]
    inv_n = 1.0 / HEAD_DIM
    mu = headsum(y) * inv_n
    dlt = y - mu
    var = headsum(dlt * dlt) * inv_n
    yn = dlt * lax.rsqrt(var + GROUPNORM_EPS) * par(_P_LNW) + par(_P_LNB)
    bonus = headsum(r * k2 * par(_P_RK)) * v
    o_ref[...] = ((yn + bonus) * g).astype(BF16)


def _rwkv(pr, par, w_lora, ones_bd, bsz, seq):
    n = pr.shape[0]
    nchunk = seq // CHUNK
    const = lambda shape: pl.BlockSpec(shape, lambda b, c: (0, 0))
    return pl.pallas_call(
        _rwkv_kernel,
        grid=(bsz, nchunk),
        in_specs=[pl.BlockSpec((CHUNK, RWKV_COLS), lambda b, c: (b * nchunk + c, 0)),
                  const(par.shape), const(w_lora.shape), const(ones_bd.shape)],
        out_specs=pl.BlockSpec((CHUNK, RWKV_WIDTH), lambda b, c: (b * nchunk + c, 0)),
        out_shape=jax.ShapeDtypeStruct((n, RWKV_WIDTH), BF16),
        scratch_shapes=[pltpu.VMEM((RWKV_HEADS, HEAD_DIM, HEAD_DIM), F32),
                        pltpu.VMEM((8, RWKV_COLS), F32),
                        pltpu.VMEM((CHUNK, RWKV_WIDTH), F32)],
        compiler_params=_cparams(("parallel", "arbitrary")),
        name="rwkv",
    )(pr, par, w_lora, ones_bd)


def _attn_kernel(a0_ref, a1_ref, a2_ref, bias_ref, gain_ref, bd_ref, out_ref,
                 qkv_ref, m_ref, s_ref, o_ref):
    seq = a0_ref.shape[0]
    ntile = m_ref.shape[0]
    gw = GROUP_WIDTH
    blk = ATTN_BLOCK
    bd = bd_ref[...]
    gq = gain_ref[0:1, :] * (HEAD_DIM ** -0.5)
    gk = gain_ref[1:2, :]
    lane_head = lax.broadcasted_iota(jnp.int32, (1, gw), 1) // HEAD_DIM
    qkv_ref[0:blk, :] = jnp.zeros((blk, 3 * gw), BF16)

    def headwise(parts):
        acc = parts[0]
        for h in range(1, HEADS_PER_GROUP):
            acc = jnp.where(lane_head == h, parts[h], acc)
        return acc

    for gi, a_ref in enumerate((a0_ref, a1_ref, a2_ref)):
        dil = ATTN_GROUPS[gi][1]
        nb = seq // dil // blk

        def normalise(c, carry, a_ref=a_ref):
            r0 = pl.multiple_of(c * 256, 256)
            for part, gain in ((0, gq), (1, gk)):
                t = a_ref[pl.ds(r0, 256), part * gw:(part + 1) * gw].astype(F32)
                hi, lo = _split2(t * t)
                ms = _dot(hi, bd) + _dot(lo, bd)
                qkv_ref[pl.ds(blk + r0, 256), part * gw:(part + 1) * gw] = (
                    t * lax.rsqrt(ms + NORM_EPS) * gain).astype(BF16)
            qkv_ref[pl.ds(blk + r0, 256), 2 * gw:] = a_ref[pl.ds(r0, 256), 2 * gw:]
            return carry

        lax.fori_loop(0, seq // 256, normalise, 0)

        def block(t, carry, gi=gi, dil=dil, nb=nb):
            z = t // nb
            n = t % nb
            r0 = pl.multiple_of(t * blk, blk)
            first = jnp.where(n == 0, 1, 0)
            q = qkv_ref[pl.ds(blk + r0, blk), 0:gw]
            kwin = qkv_ref[pl.ds(r0, 2 * blk), gw:2 * gw]
            vwin = qkv_ref[pl.ds(r0, 2 * blk), 2 * gw:]
            heads = range(HEADS_PER_GROUP)
            qm = [jnp.where(lane_head == h, q, jnp.zeros_like(q)) for h in heads]
            s = [_dot_nt(qm[h], kwin) + bias_ref[gi, first, h] for h in heads]
            m = [jnp.max(s[h], axis=-1, keepdims=True) for h in heads]
            p = [jnp.exp(s[h] - m[h]) for h in heads]
            den = [jnp.sum(p[h], axis=-1, keepdims=True) for h in heads]
            pv = [_dot(p[h].astype(BF16), vwin) for h in heads]
            o_new = headwise(pv) / headwise([jnp.broadcast_to(den[h], (blk, gw)) for h in heads])
            lse = headwise([jnp.broadcast_to(m[h] + jnp.log(den[h]), (blk, gw)) for h in heads])
            rows = pl.ds(z + n * (blk * dil), blk, stride=dil) if dil > 1 else pl.ds(r0, blk)
            ld = lambda ref: jnp.concatenate([ref[c, rows, :] for c in range(ntile)], axis=1)

            def st(ref, val):
                for c in range(ntile):
                    ref[c, rows, :] = val[:, c * LANES:(c + 1) * LANES]

            if gi == 0:
                st(m_ref, lse)
                st(o_ref, o_new)
            else:
                m_old = ld(m_ref)
                m_new = jnp.maximum(m_old, lse)
                wa = jnp.exp(m_old - m_new)
                wb = jnp.exp(lse - m_new)
                mix = wa * ld(o_ref) + wb * o_new
                if gi == 1:
                    st(m_ref, m_new)
                    st(s_ref, wa + wb)
                    st(o_ref, mix)
                else:
                    st(o_ref, mix / (wa * ld(s_ref) + wb))
            return carry

        lax.fori_loop(0, seq // blk, block, 0)

    out_ref[...] = jnp.concatenate([o_ref[c] for c in range(ntile)], axis=1).astype(out_ref.dtype)


def _attention(qkv, bias, gains, bd, bsz, seq):
    gcols = 3 * GROUP_WIDTH
    const = lambda a: pl.BlockSpec(a.shape, lambda b: (0,) * a.ndim)
    seq_spec = pl.BlockSpec((None, seq, gcols), lambda b: (b, 0, 0))
    out = pl.pallas_call(
        _attn_kernel,
        grid=(bsz,),
        in_specs=[seq_spec, seq_spec, seq_spec, const(bias), const(gains), const(bd)],
        out_specs=pl.BlockSpec((None, seq, GROUP_WIDTH), lambda b: (b, 0, 0)),
        out_shape=jax.ShapeDtypeStruct((bsz, seq, GROUP_WIDTH), BF16),
        scratch_shapes=[pltpu.VMEM((ATTN_BLOCK + seq, gcols), BF16)]
        + [pltpu.VMEM((GROUP_WIDTH // LANES, seq, LANES), F32)] * 3,
        compiler_params=_cparams(("parallel",)),
        name="attn",
    )(*qkv, bias, gains, bd)
    return out.reshape(bsz * seq, GROUP_WIDTH)


def _t5_bucket(dist):
    max_exact = N_BUCKETS // 2
    large = max_exact + (jnp.log(jnp.maximum(dist, max_exact).astype(F32) / max_exact)
                         / math.log(MAX_DISTANCE / max_exact) * (N_BUCKETS - max_exact)).astype(jnp.int32)
    return jnp.where(dist < max_exact, dist, jnp.minimum(large, N_BUCKETS - 1))


def _band_bias(rel_bias, gi):
    window, dil = ATTN_GROUPS[gi]
    qi = jnp.arange(ATTN_BLOCK)[:, None]
    kj = jnp.arange(2 * ATTN_BLOCK)[None, :]
    steps = qi - kj + ATTN_BLOCK
    heads = slice(gi * HEADS_PER_GROUP, (gi + 1) * HEADS_PER_GROUP)
    bucket = _t5_bucket(jnp.maximum(steps, 0) * dil)
    onehot = (bucket[None] == jnp.arange(N_BUCKETS)[:, None, None]).astype(F32)
    bias = jnp.sum(onehot[:, None] * rel_bias[:, heads][:, :, None, None], axis=0)
    band = (steps >= 0) & (steps <= window // dil)
    later = jnp.where(band[None], bias.astype(F32), NEG_INF)
    first = jnp.where((kj >= ATTN_BLOCK)[None], later, NEG_INF)
    return jnp.stack([later, first])


def _post_kernel(x_ref, yr_ref, att_ref, pg_ref,
                 gate1_ref, scale2_ref, shift2_ref, g2_ref, wbr_ref, wba_ref, wout_ref,
                 rwh_ref, rwl_ref, rb_ref,
                 x1_ref, h2_ref, gates_ref, eidx_ref, rank_ref, counts_ref, base_ref, run_ref):
    i = pl.program_id(0)
    tm = x_ref.shape[0]

    @pl.when(i == 0)
    def _():
        run_ref[...] = jnp.zeros_like(run_ref)

    y_r = _dot(yr_ref[...], wbr_ref[...])
    y_a = _dot(att_ref[...], wba_ref[...])
    d = D_MODEL
    mixed = (_sigmoid(pg_ref[:, :d].astype(F32)) * y_r
             + _sigmoid(pg_ref[:, d:].astype(F32)) * y_a)
    x1 = x_ref[...] + gate1_ref[0] * _dot(mixed.astype(BF16), wout_ref[...])
    x1_ref[...] = x1

    ms = jnp.mean(x1 * x1, axis=-1, keepdims=True)
    h2 = x1 * lax.rsqrt(ms + NORM_EPS) * (g2_ref[...] * (1.0 + scale2_ref[0])) + shift2_ref[0]
    h2_ref[...] = h2.astype(h2_ref.dtype)

    h_hi, h_lo = _split2(h2)
    rw_hi, rw_lo = rwh_ref[...], rwl_ref[...]
    logits = (_dot_nt(rw_hi, h_hi) + _dot_nt(rw_hi, h_lo) + _dot_nt(rw_lo, h_hi)
              + rb_ref[:, 0:1])
    eid = lax.broadcasted_iota(jnp.int32, (N_EXPERTS, tm), 0)
    vals, hots = [], []
    lg = logits
    for k in range(TOP_K):
        m = jnp.max(lg, axis=0, keepdims=True)
        idx = jnp.min(jnp.where(lg == m, eid, N_EXPERTS), axis=0, keepdims=True)
        hot = eid == idx
        vals.append(m)
        hots.append(hot)
        eidx_ref[k:k + 1, :] = idx
        lg = jnp.where(hot, -jnp.inf, lg)
    exps = [jnp.exp(v - vals[0]) for v in vals]
    tot = exps[0] + exps[1] + exps[2] + exps[3]
    for k in range(TOP_K):
        gates_ref[k:k + 1, :] = exps[k] / tot

    chosen = jnp.zeros((N_EXPERTS, tm), F32)
    for hot in hots:
        chosen = chosen + jnp.where(hot, 1.0, 0.0)
    ti = lax.broadcasted_iota(jnp.int32, (tm, tm), 0)
    tj = lax.broadcasted_iota(jnp.int32, (tm, tm), 1)
    before = jnp.where(ti < tj, 1.0, 0.0).astype(BF16)
    run = run_ref[:, 0:1]
    base = _dot(chosen.astype(BF16), before) + run
    for k in range(TOP_K):
        rank_ref[k:k + 1, :] = jnp.sum(jnp.where(hots[k], base, 0.0), axis=0,
                                       keepdims=True).astype(jnp.int32)
    run_new = run + jnp.sum(chosen, axis=1, keepdims=True)
    run_ref[...] = jnp.broadcast_to(run_new, run_ref.shape)
    counts_ref[...] = jnp.broadcast_to(run_new, counts_ref.shape).astype(jnp.int32)
    lane = lax.broadcasted_iota(jnp.int32, base_ref.shape, 1)
    starts = jnp.zeros(base_ref.shape, F32)
    for j in range(tm // ROUTE_TILE):
        starts = jnp.where(lane == j, base[:, j * ROUTE_TILE:j * ROUTE_TILE + 1], starts)
    base_ref[...] = starts.astype(jnp.int32)


def _post(x2, y_rwkv, att, pg, gate1, scale2, shift2, norm2_g, w_br_r, w_br_a, w_out,
          rw_hi, rw_lo, rb, seq):
    n, d = x2.shape
    tm = TOKEN_TILE
    per = seq // tm
    row = lambda width: pl.BlockSpec((tm, width), lambda i: (i, 0))
    const = lambda shape: pl.BlockSpec(shape, lambda i: (0,) * len(shape))
    bvec = pl.BlockSpec((1, 1, d), lambda i: (i // per, 0, 0))
    tcol = pl.BlockSpec((TOP_K, tm), lambda i: (0, i))
    return pl.pallas_call(
        _post_kernel,
        grid=(n // tm,),
        in_specs=[row(d), row(RWKV_WIDTH), row(GROUP_WIDTH), row(GATE_COLS),
                  bvec, bvec, bvec, const((1, d)), const(w_br_r.shape), const(w_br_a.shape),
                  const(w_out.shape), const(rw_hi.shape), const(rw_lo.shape), const(rb.shape)],
        out_specs=[row(d), row(d), tcol, tcol, tcol, const((N_EXPERTS, LANES)),
                   pl.BlockSpec((None, N_EXPERTS, LANES), lambda i: (i, 0, 0))],
        out_shape=[jax.ShapeDtypeStruct((n, d), F32),
                   jax.ShapeDtypeStruct((n, d), BF16),
                   jax.ShapeDtypeStruct((TOP_K, n), F32),
                   jax.ShapeDtypeStruct((TOP_K, n), jnp.int32),
                   jax.ShapeDtypeStruct((TOP_K, n), jnp.int32),
                   jax.ShapeDtypeStruct((N_EXPERTS, LANES), jnp.int32),
                   jax.ShapeDtypeStruct((n // tm, N_EXPERTS, LANES), jnp.int32)],
        scratch_shapes=[pltpu.VMEM((N_EXPERTS, 128), F32)],
        compiler_params=_cparams(("arbitrary",)),
        name="post",
    )(x2, y_rwkv, att, pg, gate1, scale2, shift2, norm2_g, w_br_r, w_br_a, w_out,
      rw_hi, rw_lo, rb)


def _pack_halves(x):
    w = x.shape[1] // 2
    lo = lax.bitcast_convert_type(x[:, :w], jnp.uint32)
    hi = lax.bitcast_convert_type(x[:, w:], jnp.uint32)
    return (hi & jnp.uint32(0xFFFF0000)) | (lo >> 16)


def _unpack_halves(u):
    lo = lax.bitcast_convert_type(u << 16, F32).astype(BF16)
    hi = lax.bitcast_convert_type(u & jnp.uint32(0xFFFF0000), F32).astype(BF16)
    return lo, hi


def _slab_rows(eidx_ref, rank_ref, base_ref, sub):
    tt = eidx_ref.shape[1]
    eid = lax.broadcasted_iota(jnp.int32, (N_EXPERTS, tt), 0)
    which = sub % (TOKEN_TILE // ROUTE_TILE)
    basevec = base_ref[:, 0:1]
    for j in range(1, TOKEN_TILE // ROUTE_TILE):
        basevec = jnp.where(which == j, base_ref[:, j:j + 1], basevec)
    lead = basevec & (ROW_ALIGN - 1)
    rows = jnp.full((N_EXPERTS, tt), -1, jnp.int32)
    for k in range(TOP_K):
        rows = jnp.where(eidx_ref[k:k + 1, :] == eid, rank_ref[k:k + 1, :] - (basevec - lead), rows)
    return rows, lead


def _dispatch_kernel(sbase_ref, scnt_ref, pstart_ref, fill_ref, eidx_ref, rank_ref, base_ref,
                     h2_ref, xs_ref, stage_ref, extra_ref, zero_ref, carry_ref, sem, xsem):
    s = pl.program_id(0)
    nblk = xs_ref.shape[0] // MOE_ROWS

    @pl.when(s == 0)
    def _():
        zero_ref[...] = jnp.zeros_like(zero_ref)

        def fill(j, carry):
            @pl.when(fill_ref[j] != 0)
            def _():
                row0 = pl.multiple_of(j * MOE_ROWS, MOE_ROWS)
                cp = pltpu.make_async_copy(zero_ref, xs_ref.at[pl.ds(row0, MOE_ROWS)], xsem)
                cp.start()
                cp.wait()
            return carry

        lax.fori_loop(0, nblk, fill, 0)

    @pl.when(s == 0)
    def _():
        carry_ref[...] = jnp.zeros_like(carry_ref)

    rows, leadvec = _slab_rows(eidx_ref, rank_ref, base_ref, s)
    h2 = h2_ref[...]
    rid = lax.broadcasted_iota(jnp.int32, (SLAB, 1), 0)
    align = ROW_ALIGN

    def onehot(e, chunk):
        return jnp.where(rows[e:e + 1, :] - chunk * SLAB == rid, 1.0, 0.0).astype(BF16)

    def lead(e):
        return sbase_ref[s * N_EXPERTS + e] & (align - 1)

    def used(e):
        return lead(e) + scnt_ref[s * N_EXPERTS + e]

    def window_start(e):
        return pl.multiple_of(pstart_ref[e] + sbase_ref[s * N_EXPERTS + e] - lead(e), align)

    for g0 in range(0, N_EXPERTS, EXPERT_GROUP):
        sel = jnp.concatenate([onehot(e, 0) for e in range(g0, g0 + EXPERT_GROUP)], axis=0)
        packed = _pack_halves(_dot(sel, h2))
        for e in range(g0, g0 + EXPERT_GROUP):
            stage_ref[e] = packed[(e - g0) * SLAB:(e - g0 + 1) * SLAB]
            stage_ref[e, 0:align, :] = stage_ref[e, 0:align, :] | carry_ref[e]
    slab_copy = lambda e: pltpu.make_async_copy(
        stage_ref.at[e], xs_ref.at[pl.ds(window_start(e), SLAB)], sem)
    for e in range(N_EXPERTS):
        slab_copy(e).start()
    for e in range(N_EXPERTS):
        def more(c, carry, e=e):
            extra_ref[...] = _pack_halves(_dot(onehot(e, c), h2))
            cp = pltpu.make_async_copy(
                extra_ref, xs_ref.at[pl.ds(window_start(e) + c * SLAB, SLAB)], xsem)
            cp.start()
            cp.wait()
            return carry

        lax.fori_loop(1, (used(e) + SLAB - 1) // SLAB, more, 0)
    total = leadvec + jnp.sum(jnp.where(rows >= 0, 1, 0), axis=1, keepdims=True)
    group0 = total - (total & (align - 1))
    gid = lax.broadcasted_iota(jnp.int32, (align, 1), 0)
    tails = [jnp.where(rows[e:e + 1, :] - group0[e:e + 1, :] == gid, 1.0, 0.0).astype(BF16)
             for e in range(N_EXPERTS)]
    tail_rows = _pack_halves(_dot(jnp.concatenate(tails, axis=0), h2))
    for e in range(N_EXPERTS):
        keep = used(e) < align
        new = tail_rows[e * align:(e + 1) * align]
        carry_ref[e] = jnp.where(keep, new | carry_ref[e], new)
    for e in range(N_EXPERTS):
        slab_copy(e).wait()


def _dispatch(sbase, scnt, pstart, fill, eidx, rank, base, h2, nblk):
    n, d = h2.shape
    tt = ROUTE_TILE
    subs = TOKEN_TILE // tt
    tcol = pl.BlockSpec((TOP_K, tt), lambda s, *_: (0, s))
    return pl.pallas_call(
        _dispatch_kernel,
        grid_spec=pltpu.PrefetchScalarGridSpec(
            num_scalar_prefetch=4,
            grid=(n // tt,),
            in_specs=[tcol, tcol,
                      pl.BlockSpec((None, N_EXPERTS, LANES), lambda s, *_: (s // subs, 0, 0)),
                      pl.BlockSpec((tt, d), lambda s, *_: (s, 0))],
            out_specs=pl.BlockSpec(memory_space=pl.ANY),
            scratch_shapes=[pltpu.VMEM((N_EXPERTS, SLAB, d // 2), jnp.uint32),
                            pltpu.VMEM((SLAB, d // 2), jnp.uint32),
                            pltpu.VMEM((MOE_ROWS, d // 2), jnp.uint32),
                            pltpu.VMEM((N_EXPERTS, ROW_ALIGN, d // 2), jnp.uint32),
                            pltpu.SemaphoreType.DMA(()), pltpu.SemaphoreType.DMA(())]),
        out_shape=jax.ShapeDtypeStruct((nblk * MOE_ROWS, d // 2), jnp.uint32),
        compiler_params=_cparams(("arbitrary",)),
        name="dispatch",
    )(sbase, scnt, pstart, fill, eidx, rank, base, h2)


def _expert_kernel(blk_ref, src_ref, valid_ref, xs_ref, w1g_ref, w1l_ref, b1g_ref, b1l_ref,
                   w2_ref, b2_ref, o_ref):
    j = pl.program_id(0)

    @pl.when(valid_ref[j] != 0)
    def _():
        lo, hi = _unpack_halves(xs_ref[...])
        half = lo.shape[1]

        def up(w_ref, b_ref):
            return _dot(lo, w_ref[:half, :]) + _dot(hi, w_ref[half:, :]) + b_ref[...]

        glu = jnp.minimum(up(w1g_ref, b1g_ref), SWIGLU_LIMIT)
        lin = jnp.clip(up(w1l_ref, b1l_ref), -SWIGLU_LIMIT, SWIGLU_LIMIT)
        act = glu * _sigmoid(SWIGLU_ALPHA * glu) * (lin + 1.0)
        o = _dot(act.astype(BF16), w2_ref[...]) + b2_ref[...]
        o_ref[...] = _pack_halves(o.astype(BF16).astype(F32))

    @pl.when(valid_ref[j] == 0)
    def _():
        o_ref[...] = jnp.zeros_like(o_ref)


def _experts(blk_e, src, valid, xs, w1g, w1l, b1g, b1l, w2, b2):
    rows, half = xs.shape
    nblk = rows // MOE_ROWS
    d, de = w1g.shape[1], w1g.shape[2]
    emap = lambda j, be, src, valid: (be[j], 0, 0)
    return pl.pallas_call(
        _expert_kernel,
        grid_spec=pltpu.PrefetchScalarGridSpec(
            num_scalar_prefetch=3,
            grid=(nblk,),
            in_specs=[pl.BlockSpec((MOE_ROWS, half), lambda j, be, src, valid: (src[j], 0)),
                      pl.BlockSpec((None, d, de), emap), pl.BlockSpec((None, d, de), emap),
                      pl.BlockSpec((None, 1, de), emap), pl.BlockSpec((None, 1, de), emap),
                      pl.BlockSpec((None, de, d), emap), pl.BlockSpec((None, 1, d), emap)],
            out_specs=pl.BlockSpec((MOE_ROWS, half), lambda j, be, src, valid: (j, 0))),
        out_shape=jax.ShapeDtypeStruct((rows, half), jnp.uint32),
        compiler_params=_cparams(("arbitrary",)),
        name="experts",
    )(blk_e, src, valid, xs, w1g, w1l, b1g, b1l, w2, b2)


def _split_w1_kernel(w_ref, perm_ref, g_ref, l_ref):
    t = MXU_TILE
    for cb in range(w_ref.shape[1] // t):
        res = _dot(w_ref[:, cb * t:(cb + 1) * t].astype(BF16), perm_ref[...])
        g_ref[:, cb * (t // 2):(cb + 1) * (t // 2)] = res[:, :t // 2].astype(BF16)
        l_ref[:, cb * (t // 2):(cb + 1) * (t // 2)] = res[:, t // 2:].astype(BF16)


def _split_w1(w1):
    ne, d, two_de = w1.shape
    t = MXU_TILE
    src = jnp.arange(t)[:, None]
    dst = jnp.arange(t)[None, :]
    perm = (src == jnp.where(dst < t // 2, 2 * dst, 2 * (dst - t // 2) + 1)).astype(BF16)
    return pl.pallas_call(
        _split_w1_kernel,
        grid=(ne,),
        in_specs=[pl.BlockSpec((None, d, two_de), lambda e: (e, 0, 0)),
                  pl.BlockSpec((t, t), lambda e: (0, 0))],
        out_specs=[pl.BlockSpec((None, d, two_de // 2), lambda e: (e, 0, 0)),
                   pl.BlockSpec((None, d, two_de // 2), lambda e: (e, 0, 0))],
        out_shape=[jax.ShapeDtypeStruct((ne, d, two_de // 2), BF16),
                   jax.ShapeDtypeStruct((ne, d, two_de // 2), BF16)],
        compiler_params=_cparams(("parallel",)),
        name="split_w1",
    )(w1, perm)


def _combine_kernel(sbase_ref, scnt_ref, pstart_ref, eidx_ref, rank_ref, gates_ref, base_ref,
                    x1_ref, gate2_ref, o_hbm, out_ref, slab_ref, extra_ref, acc_ref, sem, xsem):
    s = pl.program_id(0)
    half = acc_ref.shape[1] // 2

    def lead(e):
        return sbase_ref[s * N_EXPERTS + e] & (ROW_ALIGN - 1)

    def window_start(e):
        return pl.multiple_of(pstart_ref[e] + sbase_ref[s * N_EXPERTS + e] - lead(e), ROW_ALIGN)

    slab_copy = lambda e: pltpu.make_async_copy(
        o_hbm.at[pl.ds(window_start(e), SLAB)], slab_ref.at[e], sem)
    for e in range(N_EXPERTS):
        slab_copy(e).start()

    rows, _ = _slab_rows(eidx_ref, rank_ref, base_ref, s)
    eid = lax.broadcasted_iota(jnp.int32, rows.shape, 0)
    ge = jnp.zeros(rows.shape, F32)
    for k in range(TOP_K):
        ge = jnp.where(eidx_ref[k:k + 1, :] == eid, gates_ref[k:k + 1, :], ge)
    rid = lax.broadcasted_iota(jnp.int32, (SLAB, 1), 0)

    def weights(e, chunk):
        return jnp.where(rows[e:e + 1, :] - chunk * SLAB == rid, ge[e:e + 1, :], 0.0).astype(BF16)

    for e in range(N_EXPERTS):
        slab_copy(e).wait()
    acc_lo = acc_hi = None
    for g0 in range(0, N_EXPERTS, EXPERT_GROUP):
        wt = jnp.concatenate([weights(e, 0) for e in range(g0, g0 + EXPERT_GROUP)], axis=0)
        lo, hi = _unpack_halves(slab_ref[g0:g0 + EXPERT_GROUP].reshape(EXPERT_GROUP * SLAB, half))
        p_lo, p_hi = _dot_tn(wt, lo), _dot_tn(wt, hi)
        acc_lo = p_lo if acc_lo is None else acc_lo + p_lo
        acc_hi = p_hi if acc_hi is None else acc_hi + p_hi
    acc_ref[:, :half] = acc_lo
    acc_ref[:, half:] = acc_hi
    for e in range(N_EXPERTS):
        def more(c, carry, e=e):
            cp = pltpu.make_async_copy(
                o_hbm.at[pl.ds(window_start(e) + c * SLAB, SLAB)], extra_ref, xsem)
            cp.start()
            cp.wait()
            lo, hi = _unpack_halves(extra_ref[...])
            wt = weights(e, c)
            acc_ref[:, :half] += _dot_tn(wt, lo)
            acc_ref[:, half:] += _dot_tn(wt, hi)
            return carry

        lax.fori_loop(1, (lead(e) + scnt_ref[s * N_EXPERTS + e] + SLAB - 1) // SLAB, more, 0)
    out_ref[...] = x1_ref[...] + gate2_ref[0] * acc_ref[...]


def _combine(sbase, scnt, pstart, eidx, rank, gates, base, x1, gate2, o_rows, seq):
    n, d = x1.shape
    tt = ROUTE_TILE
    subs = TOKEN_TILE // tt
    per = seq // tt
    tcol = pl.BlockSpec((TOP_K, tt), lambda s, *_: (0, s))
    return pl.pallas_call(
        _combine_kernel,
        grid_spec=pltpu.PrefetchScalarGridSpec(
            num_scalar_prefetch=3,
            grid=(n // tt,),
            in_specs=[tcol, tcol, tcol,
                      pl.BlockSpec((None, N_EXPERTS, LANES), lambda s, *_: (s // subs, 0, 0)),
                      pl.BlockSpec((tt, d), lambda s, *_: (s, 0)),
                      pl.BlockSpec((1, 1, d), lambda s, *_: (s // per, 0, 0)),
                      pl.BlockSpec(memory_space=pl.ANY)],
            out_specs=pl.BlockSpec((tt, d), lambda s, *_: (s, 0)),
            scratch_shapes=[pltpu.VMEM((N_EXPERTS, SLAB, d // 2), jnp.uint32),
                            pltpu.VMEM((SLAB, d // 2), jnp.uint32),
                            pltpu.VMEM((tt, d), F32),
                            pltpu.SemaphoreType.DMA(()), pltpu.SemaphoreType.DMA(())]),
        out_shape=jax.ShapeDtypeStruct((n, d), F32),
        compiler_params=_cparams(("arbitrary",)),
        name="combine",
    )(sbase, scnt, pstart, eidx, rank, gates, base, x1, gate2, o_rows)


def _layer(x, c, ada_w, ada_b, norm1_g, norm2_g, w_in, mu_rkv, mu_wag, w0, w1, w2, a0, a1, a2,
           g1, g2, k_k, k_a, r_k, ln_w, ln_b, qn_g, kn_g, rel_bias, w_br_rwkv, w_br_attn, w_out,
           router_w, router_b, exp_w1, exp_b1, exp_w2, exp_b2):
    bsz, seq, d = x.shape
    n = bsz * seq
    rw = RWKV_WIDTH
    assert d == D_MODEL and seq % (ATTN_BLOCK * ATTN_GROUPS[-1][1]) == 0 and seq % TOKEN_TILE == 0

    mod = _modulation(c, ada_w, ada_b)
    shift1, scale1, gate1, shift2, scale2, gate2 = [
        m.reshape(bsz, 1, d) for m in jnp.split(mod, 6, axis=-1)]

    lora_w = jnp.concatenate([w1, a1, g1], axis=1)
    lora_mu = jnp.concatenate([jnp.broadcast_to(mu_wag[0][:, None], w1.shape),
                               jnp.broadcast_to(mu_wag[1][:, None], a1.shape),
                               jnp.broadcast_to(mu_wag[2][:, None], g1.shape)], axis=1)
    w_r = jnp.concatenate([w_in[:, :3 * rw], lora_w * (1.0 - lora_mu), lora_w * lora_mu],
                          axis=1).astype(BF16)
    qa0 = 3 * rw
    acols = []
    for gi in range(len(ATTN_GROUPS)):
        for part in range(3):
            lo = qa0 + part * ATTN_WIDTH + gi * GROUP_WIDTH
            acols.append(w_in[:, lo:lo + GROUP_WIDTH])
    w_a = jnp.concatenate(acols, axis=1).astype(BF16)
    w_g = w_in[:, qa0 + 3 * ATTN_WIDTH:].astype(BF16)

    x2 = x.reshape(n, d)
    pr, pg, qkv = _inproj(x2, norm1_g.reshape(1, d), scale1, shift1, w_r, w_a, w_g, bsz, seq)

    par = jnp.stack([w0, a0, k_k, k_a, r_k.reshape(rw), ln_w, ln_b,
                     mu_rkv[0], mu_rkv[1], mu_rkv[2]]
                    + [jnp.zeros((rw,), F32)] * 6)
    w_lora = jnp.concatenate([jnp.pad(w2, ((0, 0), (0, 2 * rw))),
                              jnp.pad(a2, ((0, 0), (rw, rw))),
                              jnp.pad(g2, ((0, 0), (2 * rw, 0)))], axis=0).astype(BF16)
    hid = jnp.arange(rw) // HEAD_DIM
    ones_bd = (hid[:, None] == hid[None, :]).astype(BF16)
    y_rwkv = _rwkv(pr, par, w_lora, ones_bd, bsz, seq)

    gains = jnp.pad(jnp.stack([jnp.tile(qn_g, HEADS_PER_GROUP), jnp.tile(kn_g, HEADS_PER_GROUP)]),
                    ((0, 6), (0, 0)))
    bias = jnp.stack([_band_bias(rel_bias, gi) for gi in range(len(ATTN_GROUPS))])
    ghead = jnp.arange(GROUP_WIDTH) // HEAD_DIM
    mean_bd = ((ghead[:, None] == ghead[None, :]).astype(F32) / HEAD_DIM).astype(BF16)
    att = _attention(qkv, bias, gains, mean_bd, bsz, seq)

    rw_t = router_w.T
    rw_hi = rw_t.astype(BF16)
    rw_lo = (rw_t - rw_hi.astype(F32)).astype(BF16)
    rb = jnp.broadcast_to(router_b[:, None], (N_EXPERTS, 128))
    x1, h2, gates, eidx, rank, counts, base = _post(
        x2, y_rwkv, att, pg, gate1, scale2, shift2, norm2_g.reshape(1, d),
        w_br_rwkv.astype(BF16), w_br_attn.astype(BF16), w_out.astype(BF16), rw_hi, rw_lo, rb, seq)

    i32 = jnp.int32
    cnt = counts[:, 0]
    real = (cnt + MOE_ROWS - 1) // MOE_ROWS
    reserved = (cnt + SLAB + ROW_ALIGN + MOE_ROWS - 1) // MOE_ROWS
    bend = jnp.cumsum(reserved)
    bstart = bend - reserved
    nblk = (n * TOP_K) // MOE_ROWS + 2 * N_EXPERTS
    j = jnp.arange(nblk, dtype=i32)
    owner = jnp.clip(jnp.sum((bend[None, :] <= j[:, None]).astype(i32), axis=1), 0, N_EXPERTS - 1)
    local = j - jnp.take(bstart, owner)
    valid = local < jnp.take(real, owner)
    src = lax.cummax(jnp.where(valid, j, 0))
    blk_e = jnp.take(owner, src)
    fill = jnp.logical_or(~valid, local == jnp.take(real, owner) - 1)
    pstart = bstart * MOE_ROWS
    subs = TOKEN_TILE // ROUTE_TILE
    sbase = jnp.transpose(base[:, :, :subs], (0, 2, 1)).reshape(n // ROUTE_TILE, N_EXPERTS)
    scnt = jnp.concatenate([sbase[1:], cnt[None, :]], axis=0) - sbase
    sbase, scnt = sbase.reshape(-1).astype(i32), scnt.reshape(-1).astype(i32)

    xs = _dispatch(sbase, scnt, pstart.astype(i32), fill.astype(i32), eidx, rank, base, h2, nblk)
    w1g, w1l = _split_w1(exp_w1)
    o_rows = _experts(blk_e.astype(i32), src.astype(i32), valid.astype(i32), xs, w1g, w1l,
                      exp_b1[:, None, 0::2], exp_b1[:, None, 1::2],
                      exp_w2.astype(BF16), exp_b2[:, None, :])
    out = _combine(sbase, scnt, pstart.astype(i32), eidx, rank, gates, base, x1, gate2, o_rows,
                   seq)
    return out.reshape(bsz, seq, d)


def kernel(x, c, ada_w, ada_b, norm1_g, norm2_g, w_in, rwkv_mu_rkv, rwkv_mu_wag, rwkv_w0, rwkv_w1, rwkv_w2, rwkv_a0, rwkv_a1, rwkv_a2, rwkv_g1, rwkv_g2, rwkv_k_k, rwkv_k_a, rwkv_r_k, rwkv_ln_w, rwkv_ln_b, attn_qn_g, attn_kn_g, rel_bias, w_br_rwkv, w_br_attn, w_out, router_w, router_b, exp_w1, exp_b1, exp_w2, exp_b2):
    per_layer = (ada_w, ada_b, norm1_g, norm2_g, w_in, rwkv_mu_rkv, rwkv_mu_wag, rwkv_w0, rwkv_w1,
                 rwkv_w2, rwkv_a0, rwkv_a1, rwkv_a2, rwkv_g1, rwkv_g2, rwkv_k_k, rwkv_k_a,
                 rwkv_r_k, rwkv_ln_w, rwkv_ln_b, attn_qn_g, attn_kn_g)
    tail = (w_br_rwkv, w_br_attn, w_out, router_w, router_b, exp_w1, exp_b1, exp_w2, exp_b2)
    for layer in range(ada_w.shape[0]):
        head = [p[layer] for p in per_layer]
        rest = [p[layer] for p in tail]
        x = _layer(x, c, *head, rel_bias, *rest)
    return x
```

```python
import functools
import math

import jax
import jax.numpy as jnp
from jax import lax
from jax.experimental import pallas as pl
from jax.experimental.pallas import tpu as pltpu

F32 = jnp.float32
BF16 = jnp.bfloat16

D_MODEL = 1024
HEAD_DIM = 64
RWKV_HEADS = 8
RWKV_WIDTH = RWKV_HEADS * HEAD_DIM
DECAY_LORA = 64
ICLR_LORA = 64
GATE_LORA = 128
LORA_WIDTH = DECAY_LORA + ICLR_LORA + GATE_LORA
GROUPNORM_EPS = 64e-5
ATTN_GROUPS = ((128, 1), (512, 4), (2048, 16))
HEADS_PER_GROUP = 4
ATTN_HEADS = HEADS_PER_GROUP * len(ATTN_GROUPS)
ATTN_WIDTH = ATTN_HEADS * HEAD_DIM
GROUP_WIDTH = HEADS_PER_GROUP * HEAD_DIM
ATTN_BLOCK = 128
N_BUCKETS = 32
MAX_DISTANCE = 2048
N_EXPERTS = 32
TOP_K = 4
D_EXPERT = D_MODEL
SWIGLU_ALPHA = 1.702
SWIGLU_LIMIT = 7.0
NORM_EPS = 1e-6
NEG_INF = -1e30

RWKV_COLS = 3 * RWKV_WIDTH + 2 * LORA_WIDTH
ATTN_COLS = 3 * ATTN_WIDTH
GATE_COLS = 2 * D_MODEL

CHUNK = 64
TOKEN_TILE = 512
MOE_ROWS = 512
ROUTE_TILE = 256
SLAB = 64
ROW_ALIGN = 8
EXTRA_PER = (ROW_ALIGN - 1 + ROUTE_TILE + SLAB - 1) // SLAB - 1
EXTRA_MAX = (N_EXPERTS * (ROW_ALIGN - 1) + TOP_K * ROUTE_TILE) // SLAB + 1
EXPERT_GROUP = 8
MXU_TILE = 256
LANES = 128
VMEM_LIMIT = 56 * 1024 * 1024


def _cparams(sem, vmem=VMEM_LIMIT):
    return pltpu.CompilerParams(dimension_semantics=sem, vmem_limit_bytes=vmem)


def _split2(a):
    hi = a.astype(BF16)
    lo = (a - hi.astype(F32)).astype(BF16)
    return hi, lo


def _split3(a):
    hi = a.astype(BF16)
    r = a - hi.astype(F32)
    mid = r.astype(BF16)
    lo = (r - mid.astype(F32)).astype(BF16)
    return hi, mid, lo


def _dot(a, b):
    return jnp.dot(a, b, preferred_element_type=F32)


def _dot_nt(a, b):
    return lax.dot_general(a, b, (((1,), (1,)), ((), ())), preferred_element_type=F32)


def _dot_tn(a, b):
    return lax.dot_general(a, b, (((0,), (0,)), ((), ())), preferred_element_type=F32)


def _sigmoid(x):
    return 1.0 / (1.0 + jnp.exp(-x))


def _mod_kernel(c_ref, w_ref, b_ref, o_ref):
    c = c_ref[...]
    s = c * _sigmoid(c)
    s_hi, s_lo = _split2(s)
    w_hi, w_lo = _split2(w_ref[...])
    o_ref[...] = _dot(s_hi, w_hi) + _dot(s_hi, w_lo) + _dot(s_lo, w_hi) + b_ref[...]


def _modulation(c, ada_w, ada_b):
    bsz, d = c.shape
    cols = ada_w.shape[1]
    tn = 1024
    return pl.pallas_call(
        _mod_kernel,
        grid=(cols // tn,),
        in_specs=[pl.BlockSpec((bsz, d), lambda j: (0, 0)),
                  pl.BlockSpec((d, tn), lambda j: (0, j)),
                  pl.BlockSpec((1, tn), lambda j: (0, j))],
        out_specs=pl.BlockSpec((bsz, tn), lambda j: (0, j)),
        out_shape=jax.ShapeDtypeStruct((bsz, cols), F32),
        compiler_params=_cparams(("arbitrary",)),
        name="mod",
    )(c, ada_w, ada_b.reshape(1, cols))


def _inproj_kernel(x_ref, g_ref, scale_ref, shift_ref, wr_ref, wa_ref, wg_ref,
                   pr_ref, pg_ref, a0_ref, a1_ref, a2_ref, h_ref):
    x = x_ref[...]
    tm = x.shape[0]
    ms = jnp.mean(x * x, axis=-1, keepdims=True)
    h = x * lax.rsqrt(ms + NORM_EPS) * (g_ref[...] * (1.0 + scale_ref[0])) + shift_ref[0]
    ntile = h_ref.shape[0]
    for c in range(ntile):
        h_ref[c] = h[:, c * LANES:(c + 1) * LANES]
    hb = h.astype(BF16)
    step = 512
    gcols = 3 * GROUP_WIDTH
    for w_ref, o_ref, cols in ((wr_ref, pr_ref, RWKV_COLS), (wg_ref, pg_ref, GATE_COLS),
                               (wa_ref, a0_ref, gcols)):
        for j in range(0, cols, step):
            width = min(step, cols - j)
            o_ref[:, j:j + width] = _dot(hb, w_ref[:, j:j + width]).astype(BF16)
    for gi, a_ref in ((1, a1_ref), (2, a2_ref)):
        dil = ATTN_GROUPS[gi][1]
        run = tm // dil
        hp = jnp.concatenate(
            [jnp.concatenate([h_ref[c, pl.ds(z, run, stride=dil), :] for z in range(dil)], axis=0)
             for c in range(ntile)], axis=1).astype(BF16)
        res = _dot(hp, wa_ref[:, gi * gcols:(gi + 1) * gcols]).astype(BF16)
        for z in range(dil):
            a_ref[z] = res[z * run:(z + 1) * run]


def _inproj(x2, norm_g, scale, shift, w_r, w_a, w_g, bsz, seq):
    n, d = x2.shape
    tm = TOKEN_TILE
    per = seq // tm
    gcols = 3 * GROUP_WIDTH
    const = lambda shape: pl.BlockSpec(shape, lambda i: (0, 0), pipeline_mode=pl.Buffered(1))
    row = lambda width: pl.BlockSpec((tm, width), lambda i: (i, 0))
    dil1, dil2 = ATTN_GROUPS[1][1], ATTN_GROUPS[2][1]
    res_spec = lambda dil: pl.BlockSpec((None, dil, None, tm // dil, gcols),
                                        lambda i: (i // per, 0, i % per, 0, 0))
    res_shape = lambda dil: jax.ShapeDtypeStruct((bsz, dil, per, tm // dil, gcols), BF16)
    pr, pg, a0, a1, a2 = pl.pallas_call(
        _inproj_kernel,
        grid=(n // tm,),
        in_specs=[row(d),
                  pl.BlockSpec((1, d), lambda i: (0, 0)),
                  pl.BlockSpec((1, 1, d), lambda i: (i // per, 0, 0)),
                  pl.BlockSpec((1, 1, d), lambda i: (i // per, 0, 0)),
                  const(w_r.shape), const(w_a.shape), const(w_g.shape)],
        out_specs=[row(RWKV_COLS), row(GATE_COLS), row(gcols), res_spec(dil1), res_spec(dil2)],
        out_shape=[jax.ShapeDtypeStruct((n, RWKV_COLS), BF16),
                   jax.ShapeDtypeStruct((n, GATE_COLS), BF16),
                   jax.ShapeDtypeStruct((n, gcols), BF16),
                   res_shape(dil1), res_shape(dil2)],
        scratch_shapes=[pltpu.VMEM((d // LANES, tm, LANES), F32)],
        compiler_params=_cparams(("parallel",)),
        name="inproj",
    )(x2, norm_g, scale, shift, w_r, w_a, w_g)
    qkv = [a.reshape(bsz, seq, gcols) for a in (a0, a1, a2)]
    return pr, pg, qkv


_P_W0, _P_A0, _P_KK, _P_KA, _P_RK, _P_LNW, _P_LNB, _P_MUR, _P_MUK, _P_MUV = range(10)


def _rwkv_kernel(pr_ref, par_ref, wl_ref, ones_ref, o_ref, state_ref, prev_ref, y_ref):
    c_idx = pl.program_id(1)

    @pl.when(c_idx == 0)
    def _():
        state_ref[...] = jnp.zeros_like(state_ref)
        prev_ref[...] = jnp.zeros_like(prev_ref)

    rw = RWKV_WIDTH
    par = lambda i: par_ref[i:i + 1, :]
    ones_bd = ones_ref[...]

    def headsum(a):
        hi, lo = _split2(a)
        return _dot(hi, ones_bd) + _dot(lo, ones_bd)

    pr = pr_ref[...].astype(F32)
    row = lax.broadcasted_iota(jnp.int32, (CHUNK, 1), 0)
    prev_row = prev_ref[0:1, :]

    def shifted(lo, hi):
        return jnp.where(row == 0, prev_row[:, lo:hi], pltpu.roll(pr[:, lo:hi], 1, 0))

    sh_rkv = shifted(0, 3 * rw)
    sh_lora = shifted(3 * rw + LORA_WIDTH, 3 * rw + 2 * LORA_WIDTH)
    prev_ref[0:1, :] = pr[CHUNK - 1:CHUNK, :]

    lora = pr[:, 3 * rw:3 * rw + LORA_WIDTH] + sh_lora
    lane = lax.broadcasted_iota(jnp.int32, (1, LORA_WIDTH), 1)
    act = jnp.where(lane < DECAY_LORA, jnp.tanh(lora),
                    jnp.where(lane < DECAY_LORA + ICLR_LORA, lora, _sigmoid(lora)))
    lo_out = _dot(act.astype(BF16), wl_ref[...])
    w_pre = par(_P_W0) + lo_out[:, :rw]
    z = -w_pre
    softplus = jnp.maximum(z, 0.0) + jnp.log(1.0 + jnp.exp(-jnp.abs(z)))
    logd = -jnp.exp(-softplus - 0.5)
    a = _sigmoid(par(_P_A0) + lo_out[:, rw:2 * rw])
    g = lo_out[:, 2 * rw:3 * rw]

    r0, k0, v0 = pr[:, :rw], pr[:, rw:2 * rw], pr[:, 2 * rw:3 * rw]
    r = r0 + (sh_rkv[:, :rw] - r0) * par(_P_MUR)
    k = k0 + (sh_rkv[:, rw:2 * rw] - k0) * par(_P_MUK)
    v = v0 + (sh_rkv[:, 2 * rw:] - v0) * par(_P_MUV)

    kk = k * par(_P_KK)
    kk = kk / jnp.maximum(jnp.sqrt(headsum(kk * kk)), 1e-12)
    k2 = k * (1.0 + (a - 1.0) * par(_P_KA))

    ti = lax.broadcasted_iota(jnp.int32, (CHUNK, CHUNK), 0)
    tj = lax.broadcasted_iota(jnp.int32, (CHUNK, CHUNK), 1)
    low_incl = ti >= tj
    low_strict = ti > tj
    tril = jnp.where(low_incl, 1.0, 0.0).astype(BF16)
    d_hi, d_mid, d_lo = _split3(logd)
    cum = _dot(tril, d_hi) + _dot(tril, d_mid) + _dot(tril, d_lo)
    p_incl = jnp.exp(cum)
    p_excl = jnp.exp(cum - logd)
    p_inv = jnp.exp(-cum)

    a_t = (-kk * p_excl).astype(BF16)
    b_t = (kk * a * p_inv).astype(BF16)
    k_t = (k2 * p_inv).astype(BF16)
    r_t = (r * p_incl).astype(BF16)
    v_b = v.astype(BF16)
    eye = jnp.where(ti == tj, 1.0, 0.0)

    heads = range(RWKV_HEADS)
    sls = [slice(h * HEAD_DIM, (h + 1) * HEAD_DIM) for h in heads]
    ar = [jnp.concatenate([a_t[:, sl], r_t[:, sl]], axis=0) for sl in sls]
    bk = [jnp.concatenate([b_t[:, sl], k_t[:, sl]], axis=0) for sl in sls]
    vh = [v_b[:, sl] for sl in sls]
    s0 = [state_ref[h] for h in heads]
    m = [_dot_nt(ar[h], bk[h]) for h in heads]
    ars = [_dot_nt(ar[h], s0[h].astype(BF16)) for h in heads]
    a_ab = [jnp.where(low_strict, m[h][:CHUNK, :CHUNK], 0.0) for h in heads]
    a_ak = [jnp.where(low_strict, m[h][:CHUNK, CHUNK:], 0.0).astype(BF16) for h in heads]
    a_rb = [jnp.where(low_incl, m[h][CHUNK:, :CHUNK], 0.0).astype(BF16) for h in heads]
    a_rk = [jnp.where(low_incl, m[h][CHUNK:, CHUNK:], 0.0).astype(BF16) for h in heads]
    t_inv = [eye + a_ab[h] for h in heads]
    pw = [a_ab[h].astype(BF16) for h in heads]
    for _ in range(int(math.log2(CHUNK)) - 1):
        pw = [_dot(pw[h], pw[h]).astype(BF16) for h in heads]
        t_inv = [t_inv[h] + _dot(pw[h], t_inv[h].astype(BF16)) for h in heads]
    rhs = [ars[h][:CHUNK] + _dot(a_ak[h], vh[h]) for h in heads]
    ub = [_dot(t_inv[h].astype(BF16), rhs[h].astype(BF16)).astype(BF16) for h in heads]
    for h in heads:
        y_ref[:, sls[h]] = ars[h][CHUNK:] + _dot(a_rb[h], ub[h]) + _dot(a_rk[h], vh[h])
    for h in heads:
        uv = jnp.concatenate([ub[h], vh[h]], axis=0)
        state_ref[h] = (s0[h] + _dot_tn(uv, bk[h])) * p_incl[CHUNK - 1:CHUNK, sls[h]]

    y = y_ref[...]
    inv_n = 1.0 / HEAD_DIM
    mu = headsum(y) * inv_n
    dlt = y - mu
    var = headsum(dlt * dlt) * inv_n
    yn = dlt * lax.rsqrt(var + GROUPNORM_EPS) * par(_P_LNW) + par(_P_LNB)
    bonus = headsum(r * k2 * par(_P_RK)) * v
    o_ref[...] = ((yn + bonus) * g).astype(BF16)


def _rwkv(pr, par, w_lora, ones_bd, bsz, seq):
    n = pr.shape[0]
    nchunk = seq // CHUNK
    const = lambda shape: pl.BlockSpec(shape, lambda b, c: (0, 0))
    return pl.pallas_call(
        _rwkv_kernel,
        grid=(bsz, nchunk),
        in_specs=[pl.BlockSpec((CHUNK, RWKV_COLS), lambda b, c: (b * nchunk + c, 0)),
                  const(par.shape), const(w_lora.shape), const(ones_bd.shape)],
        out_specs=pl.BlockSpec((CHUNK, RWKV_WIDTH), lambda b, c: (b * nchunk + c, 0)),
        out_shape=jax.ShapeDtypeStruct((n, RWKV_WIDTH), BF16),
        scratch_shapes=[pltpu.VMEM((RWKV_HEADS, HEAD_DIM, HEAD_DIM), F32),
                        pltpu.VMEM((8, RWKV_COLS), F32),
                        pltpu.VMEM((CHUNK, RWKV_WIDTH), F32)],
        compiler_params=_cparams(("parallel", "arbitrary")),
        name="rwkv",
    )(pr, par, w_lora, ones_bd)


def _attn_kernel(a0_ref, a1_ref, a2_ref, bias_ref, gain_ref, bd_ref, out_ref,
                 qkv_ref, m_ref, s_ref, o_ref):
    seq = a0_ref.shape[0]
    ntile = m_ref.shape[0]
    gw = GROUP_WIDTH
    blk = ATTN_BLOCK
    bd = bd_ref[...]
    gq = gain_ref[0:1, :] * (HEAD_DIM ** -0.5)
    gk = gain_ref[1:2, :]
    lane_head = lax.broadcasted_iota(jnp.int32, (1, gw), 1) // HEAD_DIM
    qkv_ref[0:blk, :] = jnp.zeros((blk, 3 * gw), BF16)

    def headwise(parts):
        acc = parts[0]
        for h in range(1, HEADS_PER_GROUP):
            acc = jnp.where(lane_head == h, parts[h], acc)
        return acc

    for gi, a_ref in enumerate((a0_ref, a1_ref, a2_ref)):
        dil = ATTN_GROUPS[gi][1]
        nb = seq // dil // blk

        def normalise(c, carry, a_ref=a_ref):
            r0 = pl.multiple_of(c * 256, 256)
            for part, gain in ((0, gq), (1, gk)):
                t = a_ref[pl.ds(r0, 256), part * gw:(part + 1) * gw].astype(F32)
                hi, lo = _split2(t * t)
                ms = _dot(hi, bd) + _dot(lo, bd)
                qkv_ref[pl.ds(blk + r0, 256), part * gw:(part + 1) * gw] = (
                    t * lax.rsqrt(ms + NORM_EPS) * gain).astype(BF16)
            qkv_ref[pl.ds(blk + r0, 256), 2 * gw:] = a_ref[pl.ds(r0, 256), 2 * gw:]
            return carry

        lax.fori_loop(0, seq // 256, normalise, 0)

        def block(t, carry, gi=gi, dil=dil, nb=nb):
            z = t // nb
            n = t % nb
            r0 = pl.multiple_of(t * blk, blk)
            first = jnp.where(n == 0, 1, 0)
            q = qkv_ref[pl.ds(blk + r0, blk), 0:gw]
            kwin = qkv_ref[pl.ds(r0, 2 * blk), gw:2 * gw]
            vwin = qkv_ref[pl.ds(r0, 2 * blk), 2 * gw:]
            heads = range(HEADS_PER_GROUP)
            qm = [jnp.where(lane_head == h, q, jnp.zeros_like(q)) for h in heads]
            s = [_dot_nt(qm[h], kwin) + bias_ref[gi, first, h] for h in heads]
            m = [jnp.max(s[h], axis=-1, keepdims=True) for h in heads]
            p = [jnp.exp(s[h] - m[h]) for h in heads]
            den = [jnp.sum(p[h], axis=-1, keepdims=True) for h in heads]
            pv = [_dot(p[h].astype(BF16), vwin) for h in heads]
            o_new = headwise(pv) / headwise([jnp.broadcast_to(den[h], (blk, gw)) for h in heads])
            lse = headwise([jnp.broadcast_to(m[h] + jnp.log(den[h]), (blk, gw)) for h in heads])
            rows = pl.ds(z + n * (blk * dil), blk, stride=dil) if dil > 1 else pl.ds(r0, blk)
            ld = lambda ref: jnp.concatenate([ref[c, rows, :] for c in range(ntile)], axis=1)

            def st(ref, val):
                for c in range(ntile):
                    ref[c, rows, :] = val[:, c * LANES:(c + 1) * LANES]

            if gi == 0:
                st(m_ref, lse)
                st(o_ref, o_new)
            else:
                m_old = ld(m_ref)
                m_new = jnp.maximum(m_old, lse)
                wa = jnp.exp(m_old - m_new)
                wb = jnp.exp(lse - m_new)
                mix = wa * ld(o_ref) + wb * o_new
                if gi == 1:
                    st(m_ref, m_new)
                    st(s_ref, wa + wb)
                    st(o_ref, mix)
                else:
                    st(o_ref, mix / (wa * ld(s_ref) + wb))
            return carry

        lax.fori_loop(0, seq // blk, block, 0)

    out_ref[...] = jnp.concatenate([o_ref[c] for c in range(ntile)], axis=1).astype(out_ref.dtype)


def _attention(qkv, bias, gains, bd, bsz, seq):
    gcols = 3 * GROUP_WIDTH
    const = lambda a: pl.BlockSpec(a.shape, lambda b: (0,) * a.ndim)
    seq_spec = pl.BlockSpec((None, seq, gcols), lambda b: (b, 0, 0))
    out = pl.pallas_call(
        _attn_kernel,
        grid=(bsz,),
        in_specs=[seq_spec, seq_spec, seq_spec, const(bias), const(gains), const(bd)],
        out_specs=pl.BlockSpec((None, seq, GROUP_WIDTH), lambda b: (b, 0, 0)),
        out_shape=jax.ShapeDtypeStruct((bsz, seq, GROUP_WIDTH), BF16),
        scratch_shapes=[pltpu.VMEM((ATTN_BLOCK + seq, gcols), BF16)]
        + [pltpu.VMEM((GROUP_WIDTH // LANES, seq, LANES), F32)] * 3,
        compiler_params=_cparams(("parallel",)),
        name="attn",
    )(*qkv, bias, gains, bd)
    return out.reshape(bsz * seq, GROUP_WIDTH)


def _t5_bucket(dist):
    max_exact = N_BUCKETS // 2
    large = max_exact + (jnp.log(jnp.maximum(dist, max_exact).astype(F32) / max_exact)
                         / math.log(MAX_DISTANCE / max_exact) * (N_BUCKETS - max_exact)).astype(jnp.int32)
    return jnp.where(dist < max_exact, dist, jnp.minimum(large, N_BUCKETS - 1))


def _band_bias(rel_bias, gi):
    window, dil = ATTN_GROUPS[gi]
    qi = jnp.arange(ATTN_BLOCK)[:, None]
    kj = jnp.arange(2 * ATTN_BLOCK)[None, :]
    steps = qi - kj + ATTN_BLOCK
    heads = slice(gi * HEADS_PER_GROUP, (gi + 1) * HEADS_PER_GROUP)
    bucket = _t5_bucket(jnp.maximum(steps, 0) * dil)
    onehot = (bucket[None] == jnp.arange(N_BUCKETS)[:, None, None]).astype(F32)
    bias = jnp.sum(onehot[:, None] * rel_bias[:, heads][:, :, None, None], axis=0)
    band = (steps >= 0) & (steps <= window // dil)
    later = jnp.where(band[None], bias.astype(F32), NEG_INF)
    first = jnp.where((kj >= ATTN_BLOCK)[None], later, NEG_INF)
    return jnp.stack([later, first])


def _post_kernel(x_ref, yr_ref, att_ref, pg_ref,
                 gate1_ref, scale2_ref, shift2_ref, g2_ref, wbr_ref, wba_ref, wout_ref,
                 rwh_ref, rwl_ref, rb_ref,
                 x1_ref, h2_ref, gates_ref, eidx_ref, rank_ref, counts_ref, base_ref, run_ref):
    i = pl.program_id(0)
    tm = x_ref.shape[0]

    @pl.when(i == 0)
    def _():
        run_ref[...] = jnp.zeros_like(run_ref)

    y_r = _dot(yr_ref[...], wbr_ref[...])
    y_a = _dot(att_ref[...], wba_ref[...])
    d = D_MODEL
    mixed = (_sigmoid(pg_ref[:, :d].astype(F32)) * y_r
             + _sigmoid(pg_ref[:, d:].astype(F32)) * y_a)
    x1 = x_ref[...] + gate1_ref[0] * _dot(mixed.astype(BF16), wout_ref[...])
    x1_ref[...] = x1

    ms = jnp.mean(x1 * x1, axis=-1, keepdims=True)
    h2 = x1 * lax.rsqrt(ms + NORM_EPS) * (g2_ref[...] * (1.0 + scale2_ref[0])) + shift2_ref[0]
    h2_ref[...] = h2.astype(h2_ref.dtype)

    h_hi, h_lo = _split2(h2)
    rw_hi, rw_lo = rwh_ref[...], rwl_ref[...]
    logits = (_dot_nt(rw_hi, h_hi) + _dot_nt(rw_hi, h_lo) + _dot_nt(rw_lo, h_hi)
              + rb_ref[:, 0:1])
    eid = lax.broadcasted_iota(jnp.int32, (N_EXPERTS, tm), 0)
    vals, hots = [], []
    lg = logits
    for k in range(TOP_K):
        m = jnp.max(lg, axis=0, keepdims=True)
        idx = jnp.min(jnp.where(lg == m, eid, N_EXPERTS), axis=0, keepdims=True)
        hot = eid == idx
        vals.append(m)
        hots.append(hot)
        eidx_ref[k:k + 1, :] = idx
        lg = jnp.where(hot, -jnp.inf, lg)
    exps = [jnp.exp(v - vals[0]) for v in vals]
    tot = exps[0] + exps[1] + exps[2] + exps[3]
    for k in range(TOP_K):
        gates_ref[k:k + 1, :] = exps[k] / tot

    chosen = jnp.zeros((N_EXPERTS, tm), F32)
    for hot in hots:
        chosen = chosen + jnp.where(hot, 1.0, 0.0)
    ti = lax.broadcasted_iota(jnp.int32, (tm, tm), 0)
    tj = lax.broadcasted_iota(jnp.int32, (tm, tm), 1)
    before = jnp.where(ti < tj, 1.0, 0.0).astype(BF16)
    run = run_ref[:, 0:1]
    base = _dot(chosen.astype(BF16), before) + run
    for k in range(TOP_K):
        rank_ref[k:k + 1, :] = jnp.sum(jnp.where(hots[k], base, 0.0), axis=0,
                                       keepdims=True).astype(jnp.int32)
    run_new = run + jnp.sum(chosen, axis=1, keepdims=True)
    run_ref[...] = jnp.broadcast_to(run_new, run_ref.shape)
    counts_ref[...] = jnp.broadcast_to(run_new, counts_ref.shape).astype(jnp.int32)
    lane = lax.broadcasted_iota(jnp.int32, base_ref.shape, 1)
    starts = jnp.zeros(base_ref.shape, F32)
    for j in range(tm // ROUTE_TILE):
        starts = jnp.where(lane == j, base[:, j * ROUTE_TILE:j * ROUTE_TILE + 1], starts)
    base_ref[...] = starts.astype(jnp.int32)


def _post(x2, y_rwkv, att, pg, gate1, scale2, shift2, norm2_g, w_br_r, w_br_a, w_out,
          rw_hi, rw_lo, rb, seq):
    n, d = x2.shape
    tm = TOKEN_TILE
    per = seq // tm
    row = lambda width: pl.BlockSpec((tm, width), lambda i: (i, 0))
    const = lambda shape: pl.BlockSpec(shape, lambda i: (0,) * len(shape))
    bvec = pl.BlockSpec((1, 1, d), lambda i: (i // per, 0, 0))
    tcol = pl.BlockSpec((TOP_K, tm), lambda i: (0, i))
    return pl.pallas_call(
        _post_kernel,
        grid=(n // tm,),
        in_specs=[row(d), row(RWKV_WIDTH), row(GROUP_WIDTH), row(GATE_COLS),
                  bvec, bvec, bvec, const((1, d)), const(w_br_r.shape), const(w_br_a.shape),
                  const(w_out.shape), const(rw_hi.shape), const(rw_lo.shape), const(rb.shape)],
        out_specs=[row(d), row(d), tcol, tcol, tcol, const((N_EXPERTS, LANES)),
                   pl.BlockSpec((None, N_EXPERTS, LANES), lambda i: (i, 0, 0))],
        out_shape=[jax.ShapeDtypeStruct((n, d), F32),
                   jax.ShapeDtypeStruct((n, d), BF16),
                   jax.ShapeDtypeStruct((TOP_K, n), F32),
                   jax.ShapeDtypeStruct((TOP_K, n), jnp.int32),
                   jax.ShapeDtypeStruct((TOP_K, n), jnp.int32),
                   jax.ShapeDtypeStruct((N_EXPERTS, LANES), jnp.int32),
                   jax.ShapeDtypeStruct((n // tm, N_EXPERTS, LANES), jnp.int32)],
        scratch_shapes=[pltpu.VMEM((N_EXPERTS, 128), F32)],
        compiler_params=_cparams(("arbitrary",)),
        name="post",
    )(x2, y_rwkv, att, pg, gate1, scale2, shift2, norm2_g, w_br_r, w_br_a, w_out,
      rw_hi, rw_lo, rb)


def _pack_halves(x):
    w = x.shape[1] // 2
    lo = lax.bitcast_convert_type(x[:, :w], jnp.uint32)
    hi = lax.bitcast_convert_type(x[:, w:], jnp.uint32)
    return (hi & jnp.uint32(0xFFFF0000)) | (lo >> 16)


def _unpack_halves(u):
    lo = lax.bitcast_convert_type(u << 16, F32).astype(BF16)
    hi = lax.bitcast_convert_type(u & jnp.uint32(0xFFFF0000), F32).astype(BF16)
    return lo, hi


def _slab_rows(eidx_ref, rank_ref, base_ref, sub):
    tt = eidx_ref.shape[1]
    eid = lax.broadcasted_iota(jnp.int32, (N_EXPERTS, tt), 0)
    which = sub % (TOKEN_TILE // ROUTE_TILE)
    basevec = base_ref[:, 0:1]
    for j in range(1, TOKEN_TILE // ROUTE_TILE):
        basevec = jnp.where(which == j, base_ref[:, j:j + 1], basevec)
    lead = basevec & (ROW_ALIGN - 1)
    rows = jnp.full((N_EXPERTS, tt), -1, jnp.int32)
    for k in range(TOP_K):
        rows = jnp.where(eidx_ref[k:k + 1, :] == eid, rank_ref[k:k + 1, :] - (basevec - lead), rows)
    return rows, lead


def _dispatch_kernel(sbase_ref, scnt_ref, pstart_ref, nx_ref, xlist_ref, fill_ref,
                     eidx_ref, rank_ref, base_ref, h2_ref, xs_ref,
                     stage_ref, extra_ref, zero_ref, carry_ref, rows_ref, sem, xsem):
    s = pl.program_id(0)
    nblk = xs_ref.shape[0] // MOE_ROWS

    @pl.when(s == 0)
    def _():
        zero_ref[...] = jnp.zeros_like(zero_ref)

        def fill(j, carry):
            @pl.when(fill_ref[j] != 0)
            def _():
                row0 = pl.multiple_of(j * MOE_ROWS, MOE_ROWS)
                cp = pltpu.make_async_copy(zero_ref, xs_ref.at[pl.ds(row0, MOE_ROWS)], xsem)
                cp.start()
                cp.wait()
            return carry

        lax.fori_loop(0, nblk, fill, 0)

    @pl.when(s == 0)
    def _():
        carry_ref[...] = jnp.zeros_like(carry_ref)

    rows, leadvec = _slab_rows(eidx_ref, rank_ref, base_ref, s)
    h2 = h2_ref[...]
    rid = lax.broadcasted_iota(jnp.int32, (SLAB, 1), 0)
    align = ROW_ALIGN

    def onehot(row, chunk):
        return jnp.where(row - chunk * SLAB == rid, 1.0, 0.0).astype(BF16)

    def lead(e):
        return sbase_ref[s * N_EXPERTS + e] & (align - 1)

    def used(e):
        return lead(e) + scnt_ref[s * N_EXPERTS + e]

    def window_start(e):
        return pl.multiple_of(pstart_ref[e] + sbase_ref[s * N_EXPERTS + e] - lead(e), align)

    total = leadvec + jnp.sum(jnp.where(rows >= 0, 1, 0), axis=1, keepdims=True)
    group0 = total - (total & (align - 1))
    gid = lax.broadcasted_iota(jnp.int32, (align, 1), 0)
    sel = jnp.concatenate(
        [onehot(rows[e:e + 1, :], 0) for e in range(N_EXPERTS)]
        + [jnp.where(rows[e:e + 1, :] - group0[e:e + 1, :] == gid, 1.0, 0.0).astype(BF16)
           for e in range(N_EXPERTS)], axis=0)
    packed = _pack_halves(_dot(sel, h2))
    slot = s % 2
    for e in range(N_EXPERTS):
        stage_ref[slot, e] = packed[e * SLAB:(e + 1) * SLAB]
        stage_ref[slot, e, 0:align, :] = stage_ref[slot, e, 0:align, :] | carry_ref[e]
    for e in range(N_EXPERTS):
        keep = used(e) < align
        new = packed[N_EXPERTS * SLAB + e * align:N_EXPERTS * SLAB + (e + 1) * align]
        carry_ref[e] = jnp.where(keep, new | carry_ref[e], new)

    @pl.when(s > 0)
    def _():
        for e in range(N_EXPERTS):
            pltpu.make_async_copy(stage_ref.at[1 - slot, e], xs_ref.at[pl.ds(0, SLAB)],
                                  sem.at[1 - slot]).wait()

    slab_copy = lambda e: pltpu.make_async_copy(
        stage_ref.at[slot, e], xs_ref.at[pl.ds(window_start(e), SLAB)], sem.at[slot])
    for e in range(N_EXPERTS):
        slab_copy(e).start()

    rows_ref[...] = rows

    def more(i, carry):
        cand = xlist_ref[s * EXTRA_MAX + i]
        e = cand // EXTRA_PER
        c = cand % EXTRA_PER + 1
        extra_ref[...] = _pack_halves(_dot(onehot(rows_ref[pl.ds(e, 1), :], c), h2))
        first = pstart_ref[e] + sbase_ref[s * N_EXPERTS + e]
        first = pl.multiple_of(first - (first & (align - 1)) + c * SLAB, align)
        cp = pltpu.make_async_copy(extra_ref, xs_ref.at[pl.ds(first, SLAB)], xsem)
        cp.start()
        cp.wait()
        return carry

    lax.fori_loop(0, nx_ref[s], more, 0)

    @pl.when(s == pl.num_programs(0) - 1)
    def _():
        for e in range(N_EXPERTS):
            slab_copy(e).wait()


def _dispatch(route, fill, eidx, rank, base, h2, nblk):
    n, d = h2.shape
    tt = ROUTE_TILE
    subs = TOKEN_TILE // tt
    tcol = pl.BlockSpec((TOP_K, tt), lambda s, *_: (0, s))
    return pl.pallas_call(
        _dispatch_kernel,
        grid_spec=pltpu.PrefetchScalarGridSpec(
            num_scalar_prefetch=6,
            grid=(n // tt,),
            in_specs=[tcol, tcol,
                      pl.BlockSpec((None, N_EXPERTS, LANES), lambda s, *_: (s // subs, 0, 0)),
                      pl.BlockSpec((tt, d), lambda s, *_: (s, 0))],
            out_specs=pl.BlockSpec(memory_space=pl.ANY),
            scratch_shapes=[pltpu.VMEM((2, N_EXPERTS, SLAB, d // 2), jnp.uint32),
                            pltpu.VMEM((SLAB, d // 2), jnp.uint32),
                            pltpu.VMEM((MOE_ROWS, d // 2), jnp.uint32),
                            pltpu.VMEM((N_EXPERTS, ROW_ALIGN, d // 2), jnp.uint32),
                            pltpu.VMEM((N_EXPERTS, tt), jnp.int32),
                            pltpu.SemaphoreType.DMA((2,)), pltpu.SemaphoreType.DMA(())]),
        out_shape=jax.ShapeDtypeStruct((nblk * MOE_ROWS, d // 2), jnp.uint32),
        compiler_params=_cparams(("arbitrary",)),
        name="dispatch",
    )(*route, fill, eidx, rank, base, h2)


def _expert_kernel(blk_ref, src_ref, valid_ref, xs_ref, w1g_ref, w1l_ref, b1g_ref, b1l_ref,
                   w2_ref, b2_ref, o_ref):
    j = pl.program_id(0)

    @pl.when(valid_ref[j] != 0)
    def _():
        lo, hi = _unpack_halves(xs_ref[...])
        half = lo.shape[1]

        def up(w_ref, b_ref):
            return _dot(lo, w_ref[:half, :]) + _dot(hi, w_ref[half:, :]) + b_ref[...]

        glu = jnp.minimum(up(w1g_ref, b1g_ref), SWIGLU_LIMIT)
        lin = jnp.clip(up(w1l_ref, b1l_ref), -SWIGLU_LIMIT, SWIGLU_LIMIT)
        act = glu * _sigmoid(SWIGLU_ALPHA * glu) * (lin + 1.0)
        o = _dot(act.astype(BF16), w2_ref[...]) + b2_ref[...]
        o_ref[...] = _pack_halves(o.astype(BF16).astype(F32))

    @pl.when(valid_ref[j] == 0)
    def _():
        o_ref[...] = jnp.zeros_like(o_ref)


def _experts(blk_e, src, valid, xs, w1g, w1l, b1g, b1l, w2, b2):
    rows, half = xs.shape
    nblk = rows // MOE_ROWS
    d, de = w1g.shape[1], w1g.shape[2]
    emap = lambda j, be, src, valid: (be[j], 0, 0)
    return pl.pallas_call(
        _expert_kernel,
        grid_spec=pltpu.PrefetchScalarGridSpec(
            num_scalar_prefetch=3,
            grid=(nblk,),
            in_specs=[pl.BlockSpec((MOE_ROWS, half), lambda j, be, src, valid: (src[j], 0)),
                      pl.BlockSpec((None, d, de), emap), pl.BlockSpec((None, d, de), emap),
                      pl.BlockSpec((None, 1, de), emap), pl.BlockSpec((None, 1, de), emap),
                      pl.BlockSpec((None, de, d), emap), pl.BlockSpec((None, 1, d), emap)],
            out_specs=pl.BlockSpec((MOE_ROWS, half), lambda j, be, src, valid: (j, 0))),
        out_shape=jax.ShapeDtypeStruct((rows, half), jnp.uint32),
        compiler_params=_cparams(("arbitrary",)),
        name="experts",
    )(blk_e, src, valid, xs, w1g, w1l, b1g, b1l, w2, b2)


def _split_w1_kernel(w_ref, perm_ref, g_ref, l_ref):
    t = MXU_TILE
    for cb in range(w_ref.shape[1] // t):
        res = _dot(w_ref[:, cb * t:(cb + 1) * t].astype(BF16), perm_ref[...])
        g_ref[:, cb * (t // 2):(cb + 1) * (t // 2)] = res[:, :t // 2].astype(BF16)
        l_ref[:, cb * (t // 2):(cb + 1) * (t // 2)] = res[:, t // 2:].astype(BF16)


def _split_w1(w1):
    ne, d, two_de = w1.shape
    t = MXU_TILE
    src = jnp.arange(t)[:, None]
    dst = jnp.arange(t)[None, :]
    perm = (src == jnp.where(dst < t // 2, 2 * dst, 2 * (dst - t // 2) + 1)).astype(BF16)
    return pl.pallas_call(
        _split_w1_kernel,
        grid=(ne,),
        in_specs=[pl.BlockSpec((None, d, two_de), lambda e: (e, 0, 0)),
                  pl.BlockSpec((t, t), lambda e: (0, 0))],
        out_specs=[pl.BlockSpec((None, d, two_de // 2), lambda e: (e, 0, 0)),
                   pl.BlockSpec((None, d, two_de // 2), lambda e: (e, 0, 0))],
        out_shape=[jax.ShapeDtypeStruct((ne, d, two_de // 2), BF16),
                   jax.ShapeDtypeStruct((ne, d, two_de // 2), BF16)],
        compiler_params=_cparams(("parallel",)),
        name="split_w1",
    )(w1, perm)


def _combine_kernel(sbase_ref, scnt_ref, pstart_ref, nx_ref, xlist_ref,
                    eidx_ref, rank_ref, gates_ref, base_ref, x1_ref, gate2_ref, o_hbm, out_ref,
                    slab_ref, extra_ref, acc_ref, rows_ref, ge_ref, sem, xsem):
    s = pl.program_id(0)
    last = pl.num_programs(0) - 1
    half = acc_ref.shape[1] // 2
    slot = s % 2

    def window_start(sub, e, chunk=0):
        first = pstart_ref[e] + sbase_ref[sub * N_EXPERTS + e]
        return pl.multiple_of(first - (first & (ROW_ALIGN - 1)) + chunk * SLAB, ROW_ALIGN)

    def slab_copy(sub, e, to):
        return pltpu.make_async_copy(o_hbm.at[pl.ds(window_start(sub, e), SLAB)],
                                     slab_ref.at[to, e], sem.at[to])

    @pl.when(s == 0)
    def _():
        for e in range(N_EXPERTS):
            slab_copy(0, e, 0).start()

    @pl.when(s < last)
    def _():
        for e in range(N_EXPERTS):
            slab_copy(s + 1, e, 1 - slot).start()

    rows, _ = _slab_rows(eidx_ref, rank_ref, base_ref, s)
    eid = lax.broadcasted_iota(jnp.int32, rows.shape, 0)
    ge = jnp.zeros(rows.shape, F32)
    for k in range(TOP_K):
        ge = jnp.where(eidx_ref[k:k + 1, :] == eid, gates_ref[k:k + 1, :], ge)
    rid = lax.broadcasted_iota(jnp.int32, (SLAB, 1), 0)

    def weights(row, gate, chunk):
        return jnp.where(row - chunk * SLAB == rid, gate, 0.0).astype(BF16)

    for e in range(N_EXPERTS):
        slab_copy(s, e, slot).wait()
    acc_lo = acc_hi = None
    for g0 in range(0, N_EXPERTS, EXPERT_GROUP):
        wt = jnp.concatenate([weights(rows[e:e + 1, :], ge[e:e + 1, :], 0)
                              for e in range(g0, g0 + EXPERT_GROUP)], axis=0)
        lo, hi = _unpack_halves(
            slab_ref[slot, g0:g0 + EXPERT_GROUP].reshape(EXPERT_GROUP * SLAB, half))
        p_lo, p_hi = _dot_tn(wt, lo), _dot_tn(wt, hi)
        acc_lo = p_lo if acc_lo is None else acc_lo + p_lo
        acc_hi = p_hi if acc_hi is None else acc_hi + p_hi
    acc_ref[:, :half] = acc_lo
    acc_ref[:, half:] = acc_hi

    rows_ref[...] = rows
    ge_ref[...] = ge

    def more(i, carry):
        cand = xlist_ref[s * EXTRA_MAX + i]
        e = cand // EXTRA_PER
        c = cand % EXTRA_PER + 1
        cp = pltpu.make_async_copy(o_hbm.at[pl.ds(window_start(s, e, c), SLAB)], extra_ref, xsem)
        cp.start()
        cp.wait()
        lo, hi = _unpack_halves(extra_ref[...])
        wt = weights(rows_ref[pl.ds(e, 1), :], ge_ref[pl.ds(e, 1), :], c)
        acc_ref[:, :half] += _dot_tn(wt, lo)
        acc_ref[:, half:] += _dot_tn(wt, hi)
        return carry

    lax.fori_loop(0, nx_ref[s], more, 0)
    out_ref[...] = x1_ref[...] + gate2_ref[0] * acc_ref[...]


def _combine(route, eidx, rank, gates, base, x1, gate2, o_rows, seq):
    n, d = x1.shape
    tt = ROUTE_TILE
    subs = TOKEN_TILE // tt
    per = seq // tt
    tcol = pl.BlockSpec((TOP_K, tt), lambda s, *_: (0, s))
    return pl.pallas_call(
        _combine_kernel,
        grid_spec=pltpu.PrefetchScalarGridSpec(
            num_scalar_prefetch=5,
            grid=(n // tt,),
            in_specs=[tcol, tcol, tcol,
                      pl.BlockSpec((None, N_EXPERTS, LANES), lambda s, *_: (s // subs, 0, 0)),
                      pl.BlockSpec((tt, d), lambda s, *_: (s, 0)),
                      pl.BlockSpec((1, 1, d), lambda s, *_: (s // per, 0, 0)),
                      pl.BlockSpec(memory_space=pl.ANY)],
            out_specs=pl.BlockSpec((tt, d), lambda s, *_: (s, 0)),
            scratch_shapes=[pltpu.VMEM((2, N_EXPERTS, SLAB, d // 2), jnp.uint32),
                            pltpu.VMEM((SLAB, d // 2), jnp.uint32),
                            pltpu.VMEM((tt, d), F32),
                            pltpu.VMEM((N_EXPERTS, tt), jnp.int32),
                            pltpu.VMEM((N_EXPERTS, tt), F32),
                            pltpu.SemaphoreType.DMA((2,)), pltpu.SemaphoreType.DMA(())]),
        out_shape=jax.ShapeDtypeStruct((n, d), F32),
        compiler_params=_cparams(("arbitrary",)),
        name="combine",
    )(*route, eidx, rank, gates, base, x1, gate2, o_rows)


def _layer(x, c, ada_w, ada_b, norm1_g, norm2_g, w_in, mu_rkv, mu_wag, w0, w1, w2, a0, a1, a2,
           g1, g2, k_k, k_a, r_k, ln_w, ln_b, qn_g, kn_g, rel_bias, w_br_rwkv, w_br_attn, w_out,
           router_w, router_b, exp_w1, exp_b1, exp_w2, exp_b2):
    bsz, seq, d = x.shape
    n = bsz * seq
    rw = RWKV_WIDTH
    assert d == D_MODEL and seq % (ATTN_BLOCK * ATTN_GROUPS[-1][1]) == 0 and seq % TOKEN_TILE == 0

    mod = _modulation(c, ada_w, ada_b)
    shift1, scale1, gate1, shift2, scale2, gate2 = [
        m.reshape(bsz, 1, d) for m in jnp.split(mod, 6, axis=-1)]

    lora_w = jnp.concatenate([w1, a1, g1], axis=1)
    lora_mu = jnp.concatenate([jnp.broadcast_to(mu_wag[0][:, None], w1.shape),
                               jnp.broadcast_to(mu_wag[1][:, None], a1.shape),
                               jnp.broadcast_to(mu_wag[2][:, None], g1.shape)], axis=1)
    w_r = jnp.concatenate([w_in[:, :3 * rw], lora_w * (1.0 - lora_mu), lora_w * lora_mu],
                          axis=1).astype(BF16)
    qa0 = 3 * rw
    acols = []
    for gi in range(len(ATTN_GROUPS)):
        for part in range(3):
            lo = qa0 + part * ATTN_WIDTH + gi * GROUP_WIDTH
            acols.append(w_in[:, lo:lo + GROUP_WIDTH])
    w_a = jnp.concatenate(acols, axis=1).astype(BF16)
    w_g = w_in[:, qa0 + 3 * ATTN_WIDTH:].astype(BF16)

    x2 = x.reshape(n, d)
    pr, pg, qkv = _inproj(x2, norm1_g.reshape(1, d), scale1, shift1, w_r, w_a, w_g, bsz, seq)

    par = jnp.stack([w0, a0, k_k, k_a, r_k.reshape(rw), ln_w, ln_b,
                     mu_rkv[0], mu_rkv[1], mu_rkv[2]]
                    + [jnp.zeros((rw,), F32)] * 6)
    w_lora = jnp.concatenate([jnp.pad(w2, ((0, 0), (0, 2 * rw))),
                              jnp.pad(a2, ((0, 0), (rw, rw))),
                              jnp.pad(g2, ((0, 0), (2 * rw, 0)))], axis=0).astype(BF16)
    hid = jnp.arange(rw) // HEAD_DIM
    ones_bd = (hid[:, None] == hid[None, :]).astype(BF16)
    y_rwkv = _rwkv(pr, par, w_lora, ones_bd, bsz, seq)

    gains = jnp.pad(jnp.stack([jnp.tile(qn_g, HEADS_PER_GROUP), jnp.tile(kn_g, HEADS_PER_GROUP)]),
                    ((0, 6), (0, 0)))
    bias = jnp.stack([_band_bias(rel_bias, gi) for gi in range(len(ATTN_GROUPS))])
    ghead = jnp.arange(GROUP_WIDTH) // HEAD_DIM
    mean_bd = ((ghead[:, None] == ghead[None, :]).astype(F32) / HEAD_DIM).astype(BF16)
    att = _attention(qkv, bias, gains, mean_bd, bsz, seq)

    rw_t = router_w.T
    rw_hi = rw_t.astype(BF16)
    rw_lo = (rw_t - rw_hi.astype(F32)).astype(BF16)
    rb = jnp.broadcast_to(router_b[:, None], (N_EXPERTS, 128))
    x1, h2, gates, eidx, rank, counts, base = _post(
        x2, y_rwkv, att, pg, gate1, scale2, shift2, norm2_g.reshape(1, d),
        w_br_rwkv.astype(BF16), w_br_attn.astype(BF16), w_out.astype(BF16), rw_hi, rw_lo, rb, seq)

    i32 = jnp.int32
    cnt = counts[:, 0]
    real = (cnt + MOE_ROWS - 1) // MOE_ROWS
    reserved = (cnt + SLAB + ROW_ALIGN + MOE_ROWS - 1) // MOE_ROWS
    bend = jnp.cumsum(reserved)
    bstart = bend - reserved
    nblk = (n * TOP_K) // MOE_ROWS + 2 * N_EXPERTS
    j = jnp.arange(nblk, dtype=i32)
    owner = jnp.clip(jnp.sum((bend[None, :] <= j[:, None]).astype(i32), axis=1), 0, N_EXPERTS - 1)
    local = j - jnp.take(bstart, owner)
    valid = local < jnp.take(real, owner)
    src = lax.cummax(jnp.where(valid, j, 0))
    blk_e = jnp.take(owner, src)
    fill = jnp.logical_or(~valid, local == jnp.take(real, owner) - 1)
    pstart = bstart * MOE_ROWS
    subs = TOKEN_TILE // ROUTE_TILE
    sbase = jnp.transpose(base[:, :, :subs], (0, 2, 1)).reshape(n // ROUTE_TILE, N_EXPERTS)
    scnt = jnp.concatenate([sbase[1:], cnt[None, :]], axis=0) - sbase
    chunks = ((sbase & (ROW_ALIGN - 1)) + scnt + SLAB - 1) // SLAB
    wanted = (jnp.arange(1, EXTRA_PER + 1, dtype=i32)[None, None, :] < chunks[:, :, None])
    wanted = wanted.reshape(n // ROUTE_TILE, N_EXPERTS * EXTRA_PER)
    xlist = jnp.argsort(jnp.logical_not(wanted), axis=1, stable=True)[:, :EXTRA_MAX]
    nx = jnp.sum(wanted.astype(i32), axis=1)
    route = (sbase.reshape(-1).astype(i32), scnt.reshape(-1).astype(i32), pstart.astype(i32),
             nx.astype(i32), xlist.reshape(-1).astype(i32))

    xs = _dispatch(route, fill.astype(i32), eidx, rank, base, h2, nblk)
    w1g, w1l = _split_w1(exp_w1)
    o_rows = _experts(blk_e.astype(i32), src.astype(i32), valid.astype(i32), xs, w1g, w1l,
                      exp_b1[:, None, 0::2], exp_b1[:, None, 1::2],
                      exp_w2.astype(BF16), exp_b2[:, None, :])
    out = _combine(route, eidx, rank, gates, base, x1, gate2, o_rows, seq)
    return out.reshape(bsz, seq, d)


def kernel(x, c, ada_w, ada_b, norm1_g, norm2_g, w_in, rwkv_mu_rkv, rwkv_mu_wag, rwkv_w0, rwkv_w1, rwkv_w2, rwkv_a0, rwkv_a1, rwkv_a2, rwkv_g1, rwkv_g2, rwkv_k_k, rwkv_k_a, rwkv_r_k, rwkv_ln_w, rwkv_ln_b, attn_qn_g, attn_kn_g, rel_bias, w_br_rwkv, w_br_attn, w_out, router_w, router_b, exp_w1, exp_b1, exp_w2, exp_b2):
    per_layer = (ada_w, ada_b, norm1_g, norm2_g, w_in, rwkv_mu_rkv, rwkv_mu_wag, rwkv_w0, rwkv_w1,
                 rwkv_w2, rwkv_a0, rwkv_a1, rwkv_a2, rwkv_g1, rwkv_g2, rwkv_k_k, rwkv_k_a,
                 rwkv_r_k, rwkv_ln_w, rwkv_ln_b, attn_qn_g, attn_kn_g)
    tail = (w_br_rwkv, w_br_attn, w_out, router_w, router_b, exp_w1, exp_b1, exp_w2, exp_b2)
    for layer in range(ada_w.shape[0]):
        head = [p[layer] for p in per_layer]
        rest = [p[layer] for p in tail]
        x = _layer(x, c, *head, rel_bias, *rest)
    return x
```

```python
import functools
import math

import jax
import jax.numpy as jnp
from jax import lax
from jax.experimental import pallas as pl
from jax.experimental.pallas import tpu as pltpu

F32 = jnp.float32
BF16 = jnp.bfloat16

D_MODEL = 1024
HEAD_DIM = 64
RWKV_HEADS = 8
RWKV_WIDTH = RWKV_HEADS * HEAD_DIM
DECAY_LORA = 64
ICLR_LORA = 64
GATE_LORA = 128
LORA_WIDTH = DECAY_LORA + ICLR_LORA + GATE_LORA
GROUPNORM_EPS = 64e-5
ATTN_GROUPS = ((128, 1), (512, 4), (2048, 16))
HEADS_PER_GROUP = 4
ATTN_HEADS = HEADS_PER_GROUP * len(ATTN_GROUPS)
ATTN_WIDTH = ATTN_HEADS * HEAD_DIM
GROUP_WIDTH = HEADS_PER_GROUP * HEAD_DIM
ATTN_BLOCK = 128
N_BUCKETS = 32
MAX_DISTANCE = 2048
N_EXPERTS = 32
TOP_K = 4
D_EXPERT = D_MODEL
SWIGLU_ALPHA = 1.702
SWIGLU_LIMIT = 7.0
NORM_EPS = 1e-6
NEG_INF = -1e30

RWKV_COLS = 3 * RWKV_WIDTH + 2 * LORA_WIDTH
ATTN_COLS = 3 * ATTN_WIDTH
GATE_COLS = 2 * D_MODEL

CHUNK = 64
TOKEN_TILE = 512
MOE_ROWS = 512
ROUTE_TILE = 256
SLAB = 128
ROW_ALIGN = 8
EXTRA_PER = (ROW_ALIGN - 1 + ROUTE_TILE + SLAB - 1) // SLAB - 1
EXTRA_MAX = (N_EXPERTS * (ROW_ALIGN - 1) + TOP_K * ROUTE_TILE) // SLAB + 1
EXPERT_GROUP = 8
MXU_TILE = 256
LANES = 128
VMEM_LIMIT = 56 * 1024 * 1024


def _cparams(sem, vmem=VMEM_LIMIT):
    return pltpu.CompilerParams(dimension_semantics=sem, vmem_limit_bytes=vmem)


def _split2(a):
    hi = a.astype(BF16)
    lo = (a - hi.astype(F32)).astype(BF16)
    return hi, lo


def _split3(a):
    hi = a.astype(BF16)
    r = a - hi.astype(F32)
    mid = r.astype(BF16)
    lo = (r - mid.astype(F32)).astype(BF16)
    return hi, mid, lo


def _dot(a, b):
    return jnp.dot(a, b, preferred_element_type=F32)


def _dot_nt(a, b):
    return lax.dot_general(a, b, (((1,), (1,)), ((), ())), preferred_element_type=F32)


def _dot_tn(a, b):
    return lax.dot_general(a, b, (((0,), (0,)), ((), ())), preferred_element_type=F32)


def _sigmoid(x):
    return 1.0 / (1.0 + jnp.exp(-x))


def _mod_kernel(c_ref, w_ref, b_ref, o_ref):
    c = c_ref[...]
    s = c * _sigmoid(c)
    s_hi, s_lo = _split2(s)
    w_hi, w_lo = _split2(w_ref[...])
    o_ref[...] = _dot(s_hi, w_hi) + _dot(s_hi, w_lo) + _dot(s_lo, w_hi) + b_ref[...]


def _modulation(c, ada_w, ada_b):
    bsz, d = c.shape
    cols = ada_w.shape[1]
    tn = 1024
    return pl.pallas_call(
        _mod_kernel,
        grid=(cols // tn,),
        in_specs=[pl.BlockSpec((bsz, d), lambda j: (0, 0)),
                  pl.BlockSpec((d, tn), lambda j: (0, j)),
                  pl.BlockSpec((1, tn), lambda j: (0, j))],
        out_specs=pl.BlockSpec((bsz, tn), lambda j: (0, j)),
        out_shape=jax.ShapeDtypeStruct((bsz, cols), F32),
        compiler_params=_cparams(("arbitrary",)),
        name="mod",
    )(c, ada_w, ada_b.reshape(1, cols))


def _inproj_kernel(x_ref, g_ref, scale_ref, shift_ref, wr_ref, wa_ref, wg_ref,
                   pr_ref, pg_ref, a0_ref, a1_ref, a2_ref, h_ref):
    x = x_ref[...]
    tm = x.shape[0]
    ms = jnp.mean(x * x, axis=-1, keepdims=True)
    h = x * lax.rsqrt(ms + NORM_EPS) * (g_ref[...] * (1.0 + scale_ref[0])) + shift_ref[0]
    ntile = h_ref.shape[0]
    for c in range(ntile):
        h_ref[c] = h[:, c * LANES:(c + 1) * LANES]
    hb = h.astype(BF16)
    step = 512
    gcols = 3 * GROUP_WIDTH
    for w_ref, o_ref, cols in ((wr_ref, pr_ref, RWKV_COLS), (wg_ref, pg_ref, GATE_COLS),
                               (wa_ref, a0_ref, gcols)):
        for j in range(0, cols, step):
            width = min(step, cols - j)
            o_ref[:, j:j + width] = _dot(hb, w_ref[:, j:j + width]).astype(BF16)
    for gi, a_ref in ((1, a1_ref), (2, a2_ref)):
        dil = ATTN_GROUPS[gi][1]
        run = tm // dil
        hp = jnp.concatenate(
            [jnp.concatenate([h_ref[c, pl.ds(z, run, stride=dil), :] for z in range(dil)], axis=0)
             for c in range(ntile)], axis=1).astype(BF16)
        res = _dot(hp, wa_ref[:, gi * gcols:(gi + 1) * gcols]).astype(BF16)
        for z in range(dil):
            a_ref[z] = res[z * run:(z + 1) * run]


def _inproj(x2, norm_g, scale, shift, w_r, w_a, w_g, bsz, seq):
    n, d = x2.shape
    tm = TOKEN_TILE
    per = seq // tm
    gcols = 3 * GROUP_WIDTH
    const = lambda shape: pl.BlockSpec(shape, lambda i: (0, 0), pipeline_mode=pl.Buffered(1))
    row = lambda width: pl.BlockSpec((tm, width), lambda i: (i, 0))
    dil1, dil2 = ATTN_GROUPS[1][1], ATTN_GROUPS[2][1]
    res_spec = lambda dil: pl.BlockSpec((None, dil, None, tm // dil, gcols),
                                        lambda i: (i // per, 0, i % per, 0, 0))
    res_shape = lambda dil: jax.ShapeDtypeStruct((bsz, dil, per, tm // dil, gcols), BF16)
    pr, pg, a0, a1, a2 = pl.pallas_call(
        _inproj_kernel,
        grid=(n // tm,),
        in_specs=[row(d),
                  pl.BlockSpec((1, d), lambda i: (0, 0)),
                  pl.BlockSpec((1, 1, d), lambda i: (i // per, 0, 0)),
                  pl.BlockSpec((1, 1, d), lambda i: (i // per, 0, 0)),
                  const(w_r.shape), const(w_a.shape), const(w_g.shape)],
        out_specs=[row(RWKV_COLS), row(GATE_COLS), row(gcols), res_spec(dil1), res_spec(dil2)],
        out_shape=[jax.ShapeDtypeStruct((n, RWKV_COLS), BF16),
                   jax.ShapeDtypeStruct((n, GATE_COLS), BF16),
                   jax.ShapeDtypeStruct((n, gcols), BF16),
                   res_shape(dil1), res_shape(dil2)],
        scratch_shapes=[pltpu.VMEM((d // LANES, tm, LANES), F32)],
        compiler_params=_cparams(("parallel",)),
        name="inproj",
    )(x2, norm_g, scale, shift, w_r, w_a, w_g)
    qkv = [a.reshape(bsz, seq, gcols) for a in (a0, a1, a2)]
    return pr, pg, qkv


_P_W0, _P_A0, _P_KK, _P_KA, _P_RK, _P_LNW, _P_LNB, _P_MUR, _P_MUK, _P_MUV = range(10)


def _rwkv_kernel(pr_ref, par_ref, wl_ref, ones_ref, o_ref, state_ref, prev_ref, y_ref):
    c_idx = pl.program_id(1)

    @pl.when(c_idx == 0)
    def _():
        state_ref[...] = jnp.zeros_like(state_ref)
        prev_ref[...] = jnp.zeros_like(prev_ref)

    rw = RWKV_WIDTH
    par = lambda i: par_ref[i:i + 1, :]
    ones_bd = ones_ref[...]

    def headsum(a):
        hi, lo = _split2(a)
        return _dot(hi, ones_bd) + _dot(lo, ones_bd)

    pr = pr_ref[...].astype(F32)
    row = lax.broadcasted_iota(jnp.int32, (CHUNK, 1), 0)
    prev_row = prev_ref[0:1, :]

    def shifted(lo, hi):
        return jnp.where(row == 0, prev_row[:, lo:hi], pltpu.roll(pr[:, lo:hi], 1, 0))

    sh_rkv = shifted(0, 3 * rw)
    sh_lora = shifted(3 * rw + LORA_WIDTH, 3 * rw + 2 * LORA_WIDTH)
    prev_ref[0:1, :] = pr[CHUNK - 1:CHUNK, :]

    lora = pr[:, 3 * rw:3 * rw + LORA_WIDTH] + sh_lora
    lane = lax.broadcasted_iota(jnp.int32, (1, LORA_WIDTH), 1)
    act = jnp.where(lane < DECAY_LORA, jnp.tanh(lora),
                    jnp.where(lane < DECAY_LORA + ICLR_LORA, lora, _sigmoid(lora)))
    lo_out = _dot(act.astype(BF16), wl_ref[...])
    w_pre = par(_P_W0) + lo_out[:, :rw]
    z = -w_pre
    softplus = jnp.maximum(z, 0.0) + jnp.log(1.0 + jnp.exp(-jnp.abs(z)))
    logd = -jnp.exp(-softplus - 0.5)
    a = _sigmoid(par(_P_A0) + lo_out[:, rw:2 * rw])
    g = lo_out[:, 2 * rw:3 * rw]

    r0, k0, v0 = pr[:, :rw], pr[:, rw:2 * rw], pr[:, 2 * rw:3 * rw]
    r = r0 + (sh_rkv[:, :rw] - r0) * par(_P_MUR)
    k = k0 + (sh_rkv[:, rw:2 * rw] - k0) * par(_P_MUK)
    v = v0 + (sh_rkv[:, 2 * rw:] - v0) * par(_P_MUV)

    kk = k * par(_P_KK)
    kk = kk / jnp.maximum(jnp.sqrt(headsum(kk * kk)), 1e-12)
    k2 = k * (1.0 + (a - 1.0) * par(_P_KA))

    ti = lax.broadcasted_iota(jnp.int32, (CHUNK, CHUNK), 0)
    tj = lax.broadcasted_iota(jnp.int32, (CHUNK, CHUNK), 1)
    low_incl = ti >= tj
    low_strict = ti > tj
    tril = jnp.where(low_incl, 1.0, 0.0).astype(BF16)
    d_hi, d_mid, d_lo = _split3(logd)
    cum = _dot(tril, d_hi) + _dot(tril, d_mid) + _dot(tril, d_lo)
    p_incl = jnp.exp(cum)
    p_excl = jnp.exp(cum - logd)
    p_inv = jnp.exp(-cum)

    a_t = (-kk * p_excl).astype(BF16)
    b_t = (kk * a * p_inv).astype(BF16)
    k_t = (k2 * p_inv).astype(BF16)
    r_t = (r * p_incl).astype(BF16)
    v_b = v.astype(BF16)
    eye = jnp.where(ti == tj, 1.0, 0.0)

    heads = range(RWKV_HEADS)
    sls = [slice(h * HEAD_DIM, (h + 1) * HEAD_DIM) for h in heads]
    ar = [jnp.concatenate([a_t[:, sl], r_t[:, sl]], axis=0) for sl in sls]
    bk = [jnp.concatenate([b_t[:, sl], k_t[:, sl]], axis=0) for sl in sls]
    vh = [v_b[:, sl] for sl in sls]
    s0 = [state_ref[h] for h in heads]
    m = [_dot_nt(ar[h], bk[h]) for h in heads]
    ars = [_dot_nt(ar[h], s0[h].astype(BF16)) for h in heads]
    a_ab = [jnp.where(low_strict, m[h][:CHUNK, :CHUNK], 0.0) for h in heads]
    a_ak = [jnp.where(low_strict, m[h][:CHUNK, CHUNK:], 0.0).astype(BF16) for h in heads]
    a_rb = [jnp.where(low_incl, m[h][CHUNK:, :CHUNK], 0.0).astype(BF16) for h in heads]
    a_rk = [jnp.where(low_incl, m[h][CHUNK:, CHUNK:], 0.0).astype(BF16) for h in heads]
    t_inv = [eye + a_ab[h] for h in heads]
    pw = [a_ab[h].astype(BF16) for h in heads]
    for _ in range(int(math.log2(CHUNK)) - 1):
        pw = [_dot(pw[h], pw[h]).astype(BF16) for h in heads]
        t_inv = [t_inv[h] + _dot(pw[h], t_inv[h].astype(BF16)) for h in heads]
    rhs = [ars[h][:CHUNK] + _dot(a_ak[h], vh[h]) for h in heads]
    ub = [_dot(t_inv[h].astype(BF16), rhs[h].astype(BF16)).astype(BF16) for h in heads]
    for h in heads:
        y_ref[:, sls[h]] = ars[h][CHUNK:] + _dot(a_rb[h], ub[h]) + _dot(a_rk[h], vh[h])
    for h in heads:
        uv = jnp.concatenate([ub[h], vh[h]], axis=0)
        state_ref[h] = (s0[h] + _dot_tn(uv, bk[h])) * p_incl[CHUNK - 1:CHUNK, sls[h]]

    y = y_ref[...]
    inv_n = 1.0 / HEAD_DIM
    mu = headsum(y) * inv_n
    dlt = y - mu
    var = headsum(dlt * dlt) * inv_n
    yn = dlt * lax.rsqrt(var + GROUPNORM_EPS) * par(_P_LNW) + par(_P_LNB)
    bonus = headsum(r * k2 * par(_P_RK)) * v
    o_ref[...] = ((yn + bonus) * g).astype(BF16)


def _rwkv(pr, par, w_lora, ones_bd, bsz, seq):
    n = pr.shape[0]
    nchunk = seq // CHUNK
    const = lambda shape: pl.BlockSpec(shape, lambda b, c: (0, 0))
    return pl.pallas_call(
        _rwkv_kernel,
        grid=(bsz, nchunk),
        in_specs=[pl.BlockSpec((CHUNK, RWKV_COLS), lambda b, c: (b * nchunk + c, 0)),
                  const(par.shape), const(w_lora.shape), const(ones_bd.shape)],
        out_specs=pl.BlockSpec((CHUNK, RWKV_WIDTH), lambda b, c: (b * nchunk + c, 0)),
        out_shape=jax.ShapeDtypeStruct((n, RWKV_WIDTH), BF16),
        scratch_shapes=[pltpu.VMEM((RWKV_HEADS, HEAD_DIM, HEAD_DIM), F32),
                        pltpu.VMEM((8, RWKV_COLS), F32),
                        pltpu.VMEM((CHUNK, RWKV_WIDTH), F32)],
        compiler_params=_cparams(("parallel", "arbitrary")),
        name="rwkv",
    )(pr, par, w_lora, ones_bd)


def _attn_kernel(a0_ref, a1_ref, a2_ref, bias_ref, gain_ref, bd_ref, out_ref,
                 qkv_ref, m_ref, s_ref, o_ref):
    seq = a0_ref.shape[0]
    ntile = m_ref.shape[0]
    gw = GROUP_WIDTH
    blk = ATTN_BLOCK
    bd = bd_ref[...]
    gq = gain_ref[0:1, :] * (HEAD_DIM ** -0.5)
    gk = gain_ref[1:2, :]
    lane_head = lax.broadcasted_iota(jnp.int32, (1, gw), 1) // HEAD_DIM
    qkv_ref[0:blk, :] = jnp.zeros((blk, 3 * gw), BF16)

    def headwise(parts):
        acc = parts[0]
        for h in range(1, HEADS_PER_GROUP):
            acc = jnp.where(lane_head == h, parts[h], acc)
        return acc

    for gi, a_ref in enumerate((a0_ref, a1_ref, a2_ref)):
        dil = ATTN_GROUPS[gi][1]
        nb = seq // dil // blk

        def normalise(c, carry, a_ref=a_ref):
            r0 = pl.multiple_of(c * 256, 256)
            for part, gain in ((0, gq), (1, gk)):
                t = a_ref[pl.ds(r0, 256), part * gw:(part + 1) * gw].astype(F32)
                hi, lo = _split2(t * t)
                ms = _dot(hi, bd) + _dot(lo, bd)
                qkv_ref[pl.ds(blk + r0, 256), part * gw:(part + 1) * gw] = (
                    t * lax.rsqrt(ms + NORM_EPS) * gain).astype(BF16)
            qkv_ref[pl.ds(blk + r0, 256), 2 * gw:] = a_ref[pl.ds(r0, 256), 2 * gw:]
            return carry

        lax.fori_loop(0, seq // 256, normalise, 0)

        def block(t, carry, gi=gi, dil=dil, nb=nb):
            z = t // nb
            n = t % nb
            r0 = pl.multiple_of(t * blk, blk)
            first = jnp.where(n == 0, 1, 0)
            q = qkv_ref[pl.ds(blk + r0, blk), 0:gw]
            kwin = qkv_ref[pl.ds(r0, 2 * blk), gw:2 * gw]
            vwin = qkv_ref[pl.ds(r0, 2 * blk), 2 * gw:]
            heads = range(HEADS_PER_GROUP)
            qm = [jnp.where(lane_head == h, q, jnp.zeros_like(q)) for h in heads]
            s = [_dot_nt(qm[h], kwin) + bias_ref[gi, first, h] for h in heads]
            m = [jnp.max(s[h], axis=-1, keepdims=True) for h in heads]
            p = [jnp.exp(s[h] - m[h]) for h in heads]
            den = [jnp.sum(p[h], axis=-1, keepdims=True) for h in heads]
            pv = [_dot(p[h].astype(BF16), vwin) for h in heads]
            o_new = headwise(pv) / headwise([jnp.broadcast_to(den[h], (blk, gw)) for h in heads])
            lse = headwise([jnp.broadcast_to(m[h] + jnp.log(den[h]), (blk, gw)) for h in heads])
            rows = pl.ds(z + n * (blk * dil), blk, stride=dil) if dil > 1 else pl.ds(r0, blk)
            ld = lambda ref: jnp.concatenate([ref[c, rows, :] for c in range(ntile)], axis=1)

            def st(ref, val):
                for c in range(ntile):
                    ref[c, rows, :] = val[:, c * LANES:(c + 1) * LANES]

            if gi == 0:
                st(m_ref, lse)
                st(o_ref, o_new)
            else:
                m_old = ld(m_ref)
                m_new = jnp.maximum(m_old, lse)
                wa = jnp.exp(m_old - m_new)
                wb = jnp.exp(lse - m_new)
                mix = wa * ld(o_ref) + wb * o_new
                if gi == 1:
                    st(m_ref, m_new)
                    st(s_ref, wa + wb)
                    st(o_ref, mix)
                else:
                    st(o_ref, mix / (wa * ld(s_ref) + wb))
            return carry

        lax.fori_loop(0, seq // blk, block, 0)

    out_ref[...] = jnp.concatenate([o_ref[c] for c in range(ntile)], axis=1).astype(out_ref.dtype)


def _attention(qkv, bias, gains, bd, bsz, seq):
    gcols = 3 * GROUP_WIDTH
    const = lambda a: pl.BlockSpec(a.shape, lambda b: (0,) * a.ndim)
    seq_spec = pl.BlockSpec((None, seq, gcols), lambda b: (b, 0, 0))
    out = pl.pallas_call(
        _attn_kernel,
        grid=(bsz,),
        in_specs=[seq_spec, seq_spec, seq_spec, const(bias), const(gains), const(bd)],
        out_specs=pl.BlockSpec((None, seq, GROUP_WIDTH), lambda b: (b, 0, 0)),
        out_shape=jax.ShapeDtypeStruct((bsz, seq, GROUP_WIDTH), BF16),
        scratch_shapes=[pltpu.VMEM((ATTN_BLOCK + seq, gcols), BF16)]
        + [pltpu.VMEM((GROUP_WIDTH // LANES, seq, LANES), F32)] * 3,
        compiler_params=_cparams(("parallel",)),
        name="attn",
    )(*qkv, bias, gains, bd)
    return out.reshape(bsz * seq, GROUP_WIDTH)


def _t5_bucket(dist):
    max_exact = N_BUCKETS // 2
    large = max_exact + (jnp.log(jnp.maximum(dist, max_exact).astype(F32) / max_exact)
                         / math.log(MAX_DISTANCE / max_exact) * (N_BUCKETS - max_exact)).astype(jnp.int32)
    return jnp.where(dist < max_exact, dist, jnp.minimum(large, N_BUCKETS - 1))


def _band_bias(rel_bias, gi):
    window, dil = ATTN_GROUPS[gi]
    qi = jnp.arange(ATTN_BLOCK)[:, None]
    kj = jnp.arange(2 * ATTN_BLOCK)[None, :]
    steps = qi - kj + ATTN_BLOCK
    heads = slice(gi * HEADS_PER_GROUP, (gi + 1) * HEADS_PER_GROUP)
    bucket = _t5_bucket(jnp.maximum(steps, 0) * dil)
    onehot = (bucket[None] == jnp.arange(N_BUCKETS)[:, None, None]).astype(F32)
    bias = jnp.sum(onehot[:, None] * rel_bias[:, heads][:, :, None, None], axis=0)
    band = (steps >= 0) & (steps <= window // dil)
    later = jnp.where(band[None], bias.astype(F32), NEG_INF)
    first = jnp.where((kj >= ATTN_BLOCK)[None], later, NEG_INF)
    return jnp.stack([later, first])


def _post_kernel(x_ref, yr_ref, att_ref, pg_ref,
                 gate1_ref, scale2_ref, shift2_ref, g2_ref, wbr_ref, wba_ref, wout_ref,
                 rwh_ref, rwl_ref, rb_ref,
                 x1_ref, h2_ref, gates_ref, eidx_ref, rank_ref, counts_ref, base_ref, run_ref):
    i = pl.program_id(0)
    tm = x_ref.shape[0]

    @pl.when(i == 0)
    def _():
        run_ref[...] = jnp.zeros_like(run_ref)

    y_r = _dot(yr_ref[...], wbr_ref[...])
    y_a = _dot(att_ref[...], wba_ref[...])
    d = D_MODEL
    mixed = (_sigmoid(pg_ref[:, :d].astype(F32)) * y_r
             + _sigmoid(pg_ref[:, d:].astype(F32)) * y_a)
    x1 = x_ref[...] + gate1_ref[0] * _dot(mixed.astype(BF16), wout_ref[...])
    x1_ref[...] = x1

    ms = jnp.mean(x1 * x1, axis=-1, keepdims=True)
    h2 = x1 * lax.rsqrt(ms + NORM_EPS) * (g2_ref[...] * (1.0 + scale2_ref[0])) + shift2_ref[0]
    h2_ref[...] = h2.astype(h2_ref.dtype)

    h_hi, h_lo = _split2(h2)
    rw_hi, rw_lo = rwh_ref[...], rwl_ref[...]
    logits = (_dot_nt(rw_hi, h_hi) + _dot_nt(rw_hi, h_lo) + _dot_nt(rw_lo, h_hi)
              + rb_ref[:, 0:1])
    eid = lax.broadcasted_iota(jnp.int32, (N_EXPERTS, tm), 0)
    vals, hots = [], []
    lg = logits
    for k in range(TOP_K):
        m = jnp.max(lg, axis=0, keepdims=True)
        idx = jnp.min(jnp.where(lg == m, eid, N_EXPERTS), axis=0, keepdims=True)
        hot = eid == idx
        vals.append(m)
        hots.append(hot)
        eidx_ref[k:k + 1, :] = idx
        lg = jnp.where(hot, -jnp.inf, lg)
    exps = [jnp.exp(v - vals[0]) for v in vals]
    tot = exps[0] + exps[1] + exps[2] + exps[3]
    for k in range(TOP_K):
        gates_ref[k:k + 1, :] = exps[k] / tot

    chosen = jnp.zeros((N_EXPERTS, tm), F32)
    for hot in hots:
        chosen = chosen + jnp.where(hot, 1.0, 0.0)
    ti = lax.broadcasted_iota(jnp.int32, (tm, tm), 0)
    tj = lax.broadcasted_iota(jnp.int32, (tm, tm), 1)
    before = jnp.where(ti < tj, 1.0, 0.0).astype(BF16)
    run = run_ref[:, 0:1]
    base = _dot(chosen.astype(BF16), before) + run
    for k in range(TOP_K):
        rank_ref[k:k + 1, :] = jnp.sum(jnp.where(hots[k], base, 0.0), axis=0,
                                       keepdims=True).astype(jnp.int32)
    run_new = run + jnp.sum(chosen, axis=1, keepdims=True)
    run_ref[...] = jnp.broadcast_to(run_new, run_ref.shape)
    counts_ref[...] = jnp.broadcast_to(run_new, counts_ref.shape).astype(jnp.int32)
    lane = lax.broadcasted_iota(jnp.int32, base_ref.shape, 1)
    starts = jnp.zeros(base_ref.shape, F32)
    for j in range(tm // ROUTE_TILE):
        starts = jnp.where(lane == j, base[:, j * ROUTE_TILE:j * ROUTE_TILE + 1], starts)
    base_ref[...] = starts.astype(jnp.int32)


def _post(x2, y_rwkv, att, pg, gate1, scale2, shift2, norm2_g, w_br_r, w_br_a, w_out,
          rw_hi, rw_lo, rb, seq):
    n, d = x2.shape
    tm = TOKEN_TILE
    per = seq // tm
    row = lambda width: pl.BlockSpec((tm, width), lambda i: (i, 0))
    const = lambda shape: pl.BlockSpec(shape, lambda i: (0,) * len(shape))
    bvec = pl.BlockSpec((1, 1, d), lambda i: (i // per, 0, 0))
    tcol = pl.BlockSpec((TOP_K, tm), lambda i: (0, i))
    return pl.pallas_call(
        _post_kernel,
        grid=(n // tm,),
        in_specs=[row(d), row(RWKV_WIDTH), row(GROUP_WIDTH), row(GATE_COLS),
                  bvec, bvec, bvec, const((1, d)), const(w_br_r.shape), const(w_br_a.shape),
                  const(w_out.shape), const(rw_hi.shape), const(rw_lo.shape), const(rb.shape)],
        out_specs=[row(d), row(d), tcol, tcol, tcol, const((N_EXPERTS, LANES)),
                   pl.BlockSpec((None, N_EXPERTS, LANES), lambda i: (i, 0, 0))],
        out_shape=[jax.ShapeDtypeStruct((n, d), F32),
                   jax.ShapeDtypeStruct((n, d), BF16),
                   jax.ShapeDtypeStruct((TOP_K, n), F32),
                   jax.ShapeDtypeStruct((TOP_K, n), jnp.int32),
                   jax.ShapeDtypeStruct((TOP_K, n), jnp.int32),
                   jax.ShapeDtypeStruct((N_EXPERTS, LANES), jnp.int32),
                   jax.ShapeDtypeStruct((n // tm, N_EXPERTS, LANES), jnp.int32)],
        scratch_shapes=[pltpu.VMEM((N_EXPERTS, 128), F32)],
        compiler_params=_cparams(("arbitrary",)),
        name="post",
    )(x2, y_rwkv, att, pg, gate1, scale2, shift2, norm2_g, w_br_r, w_br_a, w_out,
      rw_hi, rw_lo, rb)


def _pack_halves(x):
    w = x.shape[1] // 2
    lo = lax.bitcast_convert_type(x[:, :w], jnp.uint32)
    hi = lax.bitcast_convert_type(x[:, w:], jnp.uint32)
    return (hi & jnp.uint32(0xFFFF0000)) | (lo >> 16)


def _unpack_halves(u):
    lo = lax.bitcast_convert_type(u << 16, F32).astype(BF16)
    hi = lax.bitcast_convert_type(u & jnp.uint32(0xFFFF0000), F32).astype(BF16)
    return lo, hi


def _slab_rows(eidx_ref, rank_ref, base_ref, sub):
    tt = eidx_ref.shape[1]
    eid = lax.broadcasted_iota(jnp.int32, (N_EXPERTS, tt), 0)
    which = sub % (TOKEN_TILE // ROUTE_TILE)
    basevec = base_ref[:, 0:1]
    for j in range(1, TOKEN_TILE // ROUTE_TILE):
        basevec = jnp.where(which == j, base_ref[:, j:j + 1], basevec)
    lead = basevec & (ROW_ALIGN - 1)
    rows = jnp.full((N_EXPERTS, tt), -1, jnp.int32)
    for k in range(TOP_K):
        rows = jnp.where(eidx_ref[k:k + 1, :] == eid, rank_ref[k:k + 1, :] - (basevec - lead), rows)
    return rows, lead


def _dispatch_kernel(sbase_ref, scnt_ref, pstart_ref, nx_ref, xlist_ref, fill_ref,
                     eidx_ref, rank_ref, base_ref, h2_ref, xs_ref,
                     stage_ref, extra_ref, zero_ref, carry_ref, rows_ref, sem, xsem):
    s = pl.program_id(0)
    nblk = xs_ref.shape[0] // MOE_ROWS

    @pl.when(s == 0)
    def _():
        zero_ref[...] = jnp.zeros_like(zero_ref)

        def fill(j, carry):
            @pl.when(fill_ref[j] != 0)
            def _():
                row0 = pl.multiple_of(j * MOE_ROWS, MOE_ROWS)
                cp = pltpu.make_async_copy(zero_ref, xs_ref.at[pl.ds(row0, MOE_ROWS)], xsem)
                cp.start()
                cp.wait()
            return carry

        lax.fori_loop(0, nblk, fill, 0)

    @pl.when(s == 0)
    def _():
        carry_ref[...] = jnp.zeros_like(carry_ref)

    rows, leadvec = _slab_rows(eidx_ref, rank_ref, base_ref, s)
    h2 = h2_ref[...]
    rid = lax.broadcasted_iota(jnp.int32, (SLAB, 1), 0)
    align = ROW_ALIGN

    def onehot(row, chunk):
        return jnp.where(row - chunk * SLAB == rid, 1.0, 0.0).astype(BF16)

    def lead(e):
        return sbase_ref[s * N_EXPERTS + e] & (align - 1)

    def used(e):
        return lead(e) + scnt_ref[s * N_EXPERTS + e]

    def window_start(e):
        return pl.multiple_of(pstart_ref[e] + sbase_ref[s * N_EXPERTS + e] - lead(e), align)

    total = leadvec + jnp.sum(jnp.where(rows >= 0, 1, 0), axis=1, keepdims=True)
    group0 = total - (total & (align - 1))
    gid = lax.broadcasted_iota(jnp.int32, (align, 1), 0)
    slot = s % 2
    for g0 in range(0, N_EXPERTS, EXPERT_GROUP):
        sel = jnp.concatenate([onehot(rows[e:e + 1, :], 0)
                               for e in range(g0, g0 + EXPERT_GROUP)], axis=0)
        packed = _pack_halves(_dot(sel, h2))
        for e in range(g0, g0 + EXPERT_GROUP):
            stage_ref[slot, e] = packed[(e - g0) * SLAB:(e - g0 + 1) * SLAB]
            stage_ref[slot, e, 0:align, :] = stage_ref[slot, e, 0:align, :] | carry_ref[e]
    tails = jnp.concatenate(
        [jnp.where(rows[e:e + 1, :] - group0[e:e + 1, :] == gid, 1.0, 0.0).astype(BF16)
         for e in range(N_EXPERTS)], axis=0)
    tail_rows = _pack_halves(_dot(tails, h2))
    for e in range(N_EXPERTS):
        keep = used(e) < align
        new = tail_rows[e * align:(e + 1) * align]
        carry_ref[e] = jnp.where(keep, new | carry_ref[e], new)

    @pl.when(s > 0)
    def _():
        for e in range(N_EXPERTS):
            pltpu.make_async_copy(stage_ref.at[1 - slot, e], xs_ref.at[pl.ds(0, SLAB)],
                                  sem.at[1 - slot]).wait()

    slab_copy = lambda e: pltpu.make_async_copy(
        stage_ref.at[slot, e], xs_ref.at[pl.ds(window_start(e), SLAB)], sem.at[slot])
    for e in range(N_EXPERTS):
        slab_copy(e).start()

    rows_ref[...] = rows

    def more(i, carry):
        cand = xlist_ref[s * EXTRA_MAX + i]
        e = cand // EXTRA_PER
        c = cand % EXTRA_PER + 1
        extra_ref[...] = _pack_halves(_dot(onehot(rows_ref[pl.ds(e, 1), :], c), h2))
        first = pstart_ref[e] + sbase_ref[s * N_EXPERTS + e]
        first = pl.multiple_of(first - (first & (align - 1)) + c * SLAB, align)
        cp = pltpu.make_async_copy(extra_ref, xs_ref.at[pl.ds(first, SLAB)], xsem)
        cp.start()
        cp.wait()
        return carry

    lax.fori_loop(0, nx_ref[s], more, 0)

    @pl.when(s == pl.num_programs(0) - 1)
    def _():
        for e in range(N_EXPERTS):
            slab_copy(e).wait()


def _dispatch(route, fill, eidx, rank, base, h2, nblk):
    n, d = h2.shape
    tt = ROUTE_TILE
    subs = TOKEN_TILE // tt
    tcol = pl.BlockSpec((TOP_K, tt), lambda s, *_: (0, s))
    return pl.pallas_call(
        _dispatch_kernel,
        grid_spec=pltpu.PrefetchScalarGridSpec(
            num_scalar_prefetch=6,
            grid=(n // tt,),
            in_specs=[tcol, tcol,
                      pl.BlockSpec((None, N_EXPERTS, LANES), lambda s, *_: (s // subs, 0, 0)),
                      pl.BlockSpec((tt, d), lambda s, *_: (s, 0))],
            out_specs=pl.BlockSpec(memory_space=pl.ANY),
            scratch_shapes=[pltpu.VMEM((2, N_EXPERTS, SLAB, d // 2), jnp.uint32),
                            pltpu.VMEM((SLAB, d // 2), jnp.uint32),
                            pltpu.VMEM((MOE_ROWS, d // 2), jnp.uint32),
                            pltpu.VMEM((N_EXPERTS, ROW_ALIGN, d // 2), jnp.uint32),
                            pltpu.VMEM((N_EXPERTS, tt), jnp.int32),
                            pltpu.SemaphoreType.DMA((2,)), pltpu.SemaphoreType.DMA(())]),
        out_shape=jax.ShapeDtypeStruct((nblk * MOE_ROWS, d // 2), jnp.uint32),
        compiler_params=_cparams(("arbitrary",)),
        name="dispatch",
    )(*route, fill, eidx, rank, base, h2)


def _expert_kernel(blk_ref, src_ref, valid_ref, xs_ref, w1g_ref, w1l_ref, b1g_ref, b1l_ref,
                   w2_ref, b2_ref, o_ref):
    j = pl.program_id(0)

    @pl.when(valid_ref[j] != 0)
    def _():
        lo, hi = _unpack_halves(xs_ref[...])
        half = lo.shape[1]

        def up(w_ref, b_ref):
            return _dot(lo, w_ref[:half, :]) + _dot(hi, w_ref[half:, :]) + b_ref[...]

        glu = jnp.minimum(up(w1g_ref, b1g_ref), SWIGLU_LIMIT)
        lin = jnp.clip(up(w1l_ref, b1l_ref), -SWIGLU_LIMIT, SWIGLU_LIMIT)
        act = glu * _sigmoid(SWIGLU_ALPHA * glu) * (lin + 1.0)
        o = _dot(act.astype(BF16), w2_ref[...]) + b2_ref[...]
        o_ref[...] = _pack_halves(o.astype(BF16).astype(F32))

    @pl.when(valid_ref[j] == 0)
    def _():
        o_ref[...] = jnp.zeros_like(o_ref)


def _experts(blk_e, src, valid, xs, w1g, w1l, b1g, b1l, w2, b2):
    rows, half = xs.shape
    nblk = rows // MOE_ROWS
    d, de = w1g.shape[1], w1g.shape[2]
    emap = lambda j, be, src, valid: (be[j], 0, 0)
    return pl.pallas_call(
        _expert_kernel,
        grid_spec=pltpu.PrefetchScalarGridSpec(
            num_scalar_prefetch=3,
            grid=(nblk,),
            in_specs=[pl.BlockSpec((MOE_ROWS, half), lambda j, be, src, valid: (src[j], 0)),
                      pl.BlockSpec((None, d, de), emap), pl.BlockSpec((None, d, de), emap),
                      pl.BlockSpec((None, 1, de), emap), pl.BlockSpec((None, 1, de), emap),
                      pl.BlockSpec((None, de, d), emap), pl.BlockSpec((None, 1, d), emap)],
            out_specs=pl.BlockSpec((MOE_ROWS, half), lambda j, be, src, valid: (j, 0))),
        out_shape=jax.ShapeDtypeStruct((rows, half), jnp.uint32),
        compiler_params=_cparams(("arbitrary",)),
        name="experts",
    )(blk_e, src, valid, xs, w1g, w1l, b1g, b1l, w2, b2)


def _split_w1_kernel(w_ref, perm_ref, g_ref, l_ref):
    t = MXU_TILE
    for cb in range(w_ref.shape[1] // t):
        res = _dot(w_ref[:, cb * t:(cb + 1) * t].astype(BF16), perm_ref[...])
        g_ref[:, cb * (t // 2):(cb + 1) * (t // 2)] = res[:, :t // 2].astype(BF16)
        l_ref[:, cb * (t // 2):(cb + 1) * (t // 2)] = res[:, t // 2:].astype(BF16)


def _split_w1(w1):
    ne, d, two_de = w1.shape
    t = MXU_TILE
    src = jnp.arange(t)[:, None]
    dst = jnp.arange(t)[None, :]
    perm = (src == jnp.where(dst < t // 2, 2 * dst, 2 * (dst - t // 2) + 1)).astype(BF16)
    return pl.pallas_call(
        _split_w1_kernel,
        grid=(ne,),
        in_specs=[pl.BlockSpec((None, d, two_de), lambda e: (e, 0, 0)),
                  pl.BlockSpec((t, t), lambda e: (0, 0))],
        out_specs=[pl.BlockSpec((None, d, two_de // 2), lambda e: (e, 0, 0)),
                   pl.BlockSpec((None, d, two_de // 2), lambda e: (e, 0, 0))],
        out_shape=[jax.ShapeDtypeStruct((ne, d, two_de // 2), BF16),
                   jax.ShapeDtypeStruct((ne, d, two_de // 2), BF16)],
        compiler_params=_cparams(("parallel",)),
        name="split_w1",
    )(w1, perm)


def _combine_kernel(sbase_ref, scnt_ref, pstart_ref, nx_ref, xlist_ref,
                    eidx_ref, rank_ref, gates_ref, base_ref, x1_ref, gate2_ref, o_hbm, out_ref,
                    slab_ref, extra_ref, acc_ref, rows_ref, ge_ref, sem, xsem):
    s = pl.program_id(0)
    last = pl.num_programs(0) - 1
    half = acc_ref.shape[1] // 2
    slot = s % 2

    def window_start(sub, e, chunk=0):
        first = pstart_ref[e] + sbase_ref[sub * N_EXPERTS + e]
        return pl.multiple_of(first - (first & (ROW_ALIGN - 1)) + chunk * SLAB, ROW_ALIGN)

    def slab_copy(sub, e, to):
        return pltpu.make_async_copy(o_hbm.at[pl.ds(window_start(sub, e), SLAB)],
                                     slab_ref.at[to, e], sem.at[to])

    @pl.when(s == 0)
    def _():
        for e in range(N_EXPERTS):
            slab_copy(0, e, 0).start()

    @pl.when(s < last)
    def _():
        for e in range(N_EXPERTS):
            slab_copy(s + 1, e, 1 - slot).start()

    rows, _ = _slab_rows(eidx_ref, rank_ref, base_ref, s)
    eid = lax.broadcasted_iota(jnp.int32, rows.shape, 0)
    ge = jnp.zeros(rows.shape, F32)
    for k in range(TOP_K):
        ge = jnp.where(eidx_ref[k:k + 1, :] == eid, gates_ref[k:k + 1, :], ge)
    rid = lax.broadcasted_iota(jnp.int32, (SLAB, 1), 0)

    def weights(row, gate, chunk):
        return jnp.where(row - chunk * SLAB == rid, gate, 0.0).astype(BF16)

    for e in range(N_EXPERTS):
        slab_copy(s, e, slot).wait()
    acc_lo = acc_hi = None
    for g0 in range(0, N_EXPERTS, EXPERT_GROUP):
        wt = jnp.concatenate([weights(rows[e:e + 1, :], ge[e:e + 1, :], 0)
                              for e in range(g0, g0 + EXPERT_GROUP)], axis=0)
        lo, hi = _unpack_halves(
            slab_ref[slot, g0:g0 + EXPERT_GROUP].reshape(EXPERT_GROUP * SLAB, half))
        p_lo, p_hi = _dot_tn(wt, lo), _dot_tn(wt, hi)
        acc_lo = p_lo if acc_lo is None else acc_lo + p_lo
        acc_hi = p_hi if acc_hi is None else acc_hi + p_hi
    acc_ref[:, :half] = acc_lo
    acc_ref[:, half:] = acc_hi

    rows_ref[...] = rows
    ge_ref[...] = ge

    def more(i, carry):
        cand = xlist_ref[s * EXTRA_MAX + i]
        e = cand // EXTRA_PER
        c = cand % EXTRA_PER + 1
        cp = pltpu.make_async_copy(o_hbm.at[pl.ds(window_start(s, e, c), SLAB)], extra_ref, xsem)
        cp.start()
        cp.wait()
        lo, hi = _unpack_halves(extra_ref[...])
        wt = weights(rows_ref[pl.ds(e, 1), :], ge_ref[pl.ds(e, 1), :], c)
        acc_ref[:, :half] += _dot_tn(wt, lo)
        acc_ref[:, half:] += _dot_tn(wt, hi)
        return carry

    lax.fori_loop(0, nx_ref[s], more, 0)
    out_ref[...] = x1_ref[...] + gate2_ref[0] * acc_ref[...]


def _combine(route, eidx, rank, gates, base, x1, gate2, o_rows, seq):
    n, d = x1.shape
    tt = ROUTE_TILE
    subs = TOKEN_TILE // tt
    per = seq // tt
    tcol = pl.BlockSpec((TOP_K, tt), lambda s, *_: (0, s))
    return pl.pallas_call(
        _combine_kernel,
        grid_spec=pltpu.PrefetchScalarGridSpec(
            num_scalar_prefetch=5,
            grid=(n // tt,),
            in_specs=[tcol, tcol, tcol,
                      pl.BlockSpec((None, N_EXPERTS, LANES), lambda s, *_: (s // subs, 0, 0)),
                      pl.BlockSpec((tt, d), lambda s, *_: (s, 0)),
                      pl.BlockSpec((1, 1, d), lambda s, *_: (s // per, 0, 0)),
                      pl.BlockSpec(memory_space=pl.ANY)],
            out_specs=pl.BlockSpec((tt, d), lambda s, *_: (s, 0)),
            scratch_shapes=[pltpu.VMEM((2, N_EXPERTS, SLAB, d // 2), jnp.uint32),
                            pltpu.VMEM((SLAB, d // 2), jnp.uint32),
                            pltpu.VMEM((tt, d), F32),
                            pltpu.VMEM((N_EXPERTS, tt), jnp.int32),
                            pltpu.VMEM((N_EXPERTS, tt), F32),
                            pltpu.SemaphoreType.DMA((2,)), pltpu.SemaphoreType.DMA(())]),
        out_shape=jax.ShapeDtypeStruct((n, d), F32),
        compiler_params=_cparams(("arbitrary",)),
        name="combine",
    )(*route, eidx, rank, gates, base, x1, gate2, o_rows)


def _layer(x, c, ada_w, ada_b, norm1_g, norm2_g, w_in, mu_rkv, mu_wag, w0, w1, w2, a0, a1, a2,
           g1, g2, k_k, k_a, r_k, ln_w, ln_b, qn_g, kn_g, rel_bias, w_br_rwkv, w_br_attn, w_out,
           router_w, router_b, exp_w1, exp_b1, exp_w2, exp_b2):
    bsz, seq, d = x.shape
    n = bsz * seq
    rw = RWKV_WIDTH
    assert d == D_MODEL and seq % (ATTN_BLOCK * ATTN_GROUPS[-1][1]) == 0 and seq % TOKEN_TILE == 0

    mod = _modulation(c, ada_w, ada_b)
    shift1, scale1, gate1, shift2, scale2, gate2 = [
        m.reshape(bsz, 1, d) for m in jnp.split(mod, 6, axis=-1)]

    lora_w = jnp.concatenate([w1, a1, g1], axis=1)
    lora_mu = jnp.concatenate([jnp.broadcast_to(mu_wag[0][:, None], w1.shape),
                               jnp.broadcast_to(mu_wag[1][:, None], a1.shape),
                               jnp.broadcast_to(mu_wag[2][:, None], g1.shape)], axis=1)
    w_r = jnp.concatenate([w_in[:, :3 * rw], lora_w * (1.0 - lora_mu), lora_w * lora_mu],
                          axis=1).astype(BF16)
    qa0 = 3 * rw
    acols = []
    for gi in range(len(ATTN_GROUPS)):
        for part in range(3):
            lo = qa0 + part * ATTN_WIDTH + gi * GROUP_WIDTH
            acols.append(w_in[:, lo:lo + GROUP_WIDTH])
    w_a = jnp.concatenate(acols, axis=1).astype(BF16)
    w_g = w_in[:, qa0 + 3 * ATTN_WIDTH:].astype(BF16)

    x2 = x.reshape(n, d)
    pr, pg, qkv = _inproj(x2, norm1_g.reshape(1, d), scale1, shift1, w_r, w_a, w_g, bsz, seq)

    par = jnp.stack([w0, a0, k_k, k_a, r_k.reshape(rw), ln_w, ln_b,
                     mu_rkv[0], mu_rkv[1], mu_rkv[2]]
                    + [jnp.zeros((rw,), F32)] * 6)
    w_lora = jnp.concatenate([jnp.pad(w2, ((0, 0), (0, 2 * rw))),
                              jnp.pad(a2, ((0, 0), (rw, rw))),
                              jnp.pad(g2, ((0, 0), (2 * rw, 0)))], axis=0).astype(BF16)
    hid = jnp.arange(rw) // HEAD_DIM
    ones_bd = (hid[:, None] == hid[None, :]).astype(BF16)
    y_rwkv = _rwkv(pr, par, w_lora, ones_bd, bsz, seq)

    gains = jnp.pad(jnp.stack([jnp.tile(qn_g, HEADS_PER_GROUP), jnp.tile(kn_g, HEADS_PER_GROUP)]),
                    ((0, 6), (0, 0)))
    bias = jnp.stack([_band_bias(rel_bias, gi) for gi in range(len(ATTN_GROUPS))])
    ghead = jnp.arange(GROUP_WIDTH) // HEAD_DIM
    mean_bd = ((ghead[:, None] == ghead[None, :]).astype(F32) / HEAD_DIM).astype(BF16)
    att = _attention(qkv, bias, gains, mean_bd, bsz, seq)

    rw_t = router_w.T
    rw_hi = rw_t.astype(BF16)
    rw_lo = (rw_t - rw_hi.astype(F32)).astype(BF16)
    rb = jnp.broadcast_to(router_b[:, None], (N_EXPERTS, 128))
    x1, h2, gates, eidx, rank, counts, base = _post(
        x2, y_rwkv, att, pg, gate1, scale2, shift2, norm2_g.reshape(1, d),
        w_br_rwkv.astype(BF16), w_br_attn.astype(BF16), w_out.astype(BF16), rw_hi, rw_lo, rb, seq)

    i32 = jnp.int32
    cnt = counts[:, 0]
    real = (cnt + MOE_ROWS - 1) // MOE_ROWS
    reserved = (cnt + SLAB + ROW_ALIGN + MOE_ROWS - 1) // MOE_ROWS
    bend = jnp.cumsum(reserved)
    bstart = bend - reserved
    nblk = (n * TOP_K) // MOE_ROWS + 2 * N_EXPERTS
    j = jnp.arange(nblk, dtype=i32)
    owner = jnp.clip(jnp.sum((bend[None, :] <= j[:, None]).astype(i32), axis=1), 0, N_EXPERTS - 1)
    local = j - jnp.take(bstart, owner)
    valid = local < jnp.take(real, owner)
    src = lax.cummax(jnp.where(valid, j, 0))
    blk_e = jnp.take(owner, src)
    fill = jnp.logical_or(~valid, local == jnp.take(real, owner) - 1)
    pstart = bstart * MOE_ROWS
    subs = TOKEN_TILE // ROUTE_TILE
    sbase = jnp.transpose(base[:, :, :subs], (0, 2, 1)).reshape(n // ROUTE_TILE, N_EXPERTS)
    scnt = jnp.concatenate([sbase[1:], cnt[None, :]], axis=0) - sbase
    chunks = ((sbase & (ROW_ALIGN - 1)) + scnt + SLAB - 1) // SLAB
    wanted = (jnp.arange(1, EXTRA_PER + 1, dtype=i32)[None, None, :] < chunks[:, :, None])
    wanted = wanted.reshape(n // ROUTE_TILE, N_EXPERTS * EXTRA_PER)
    xlist = jnp.argsort(jnp.logical_not(wanted), axis=1, stable=True)[:, :EXTRA_MAX]
    nx = jnp.sum(wanted.astype(i32), axis=1)
    route = (sbase.reshape(-1).astype(i32), scnt.reshape(-1).astype(i32), pstart.astype(i32),
             nx.astype(i32), xlist.reshape(-1).astype(i32))

    xs = _dispatch(route, fill.astype(i32), eidx, rank, base, h2, nblk)
    w1g, w1l = _split_w1(exp_w1)
    o_rows = _experts(blk_e.astype(i32), src.astype(i32), valid.astype(i32), xs, w1g, w1l,
                      exp_b1[:, None, 0::2], exp_b1[:, None, 1::2],
                      exp_w2.astype(BF16), exp_b2[:, None, :])
    out = _combine(route, eidx, rank, gates, base, x1, gate2, o_rows, seq)
    return out.reshape(bsz, seq, d)


def kernel(x, c, ada_w, ada_b, norm1_g, norm2_g, w_in, rwkv_mu_rkv, rwkv_mu_wag, rwkv_w0, rwkv_w1, rwkv_w2, rwkv_a0, rwkv_a1, rwkv_a2, rwkv_g1, rwkv_g2, rwkv_k_k, rwkv_k_a, rwkv_r_k, rwkv_ln_w, rwkv_ln_b, attn_qn_g, attn_kn_g, rel_bias, w_br_rwkv, w_br_attn, w_out, router_w, router_b, exp_w1, exp_b1, exp_w2, exp_b2):
    per_layer = (ada_w, ada_b, norm1_g, norm2_g, w_in, rwkv_mu_rkv, rwkv_mu_wag, rwkv_w0, rwkv_w1,
                 rwkv_w2, rwkv_a0, rwkv_a1, rwkv_a2, rwkv_g1, rwkv_g2, rwkv_k_k, rwkv_k_a,
                 rwkv_r_k, rwkv_ln_w, rwkv_ln_b, attn_qn_g, attn_kn_g)
    tail = (w_br_rwkv, w_br_attn, w_out, router_w, router_b, exp_w1, exp_b1, exp_w2, exp_b2)
    for layer in range(ada_w.shape[0]):
        head = [p[layer] for p in per_layer]
        rest = [p[layer] for p in tail]
        x = _layer(x, c, *head, rel_bias, *rest)
    return x
```

```python
import functools
import math

import jax
import jax.numpy as jnp
from jax import lax
from jax.experimental import pallas as pl
from jax.experimental.pallas import tpu as pltpu

F32 = jnp.float32
BF16 = jnp.bfloat16

D_MODEL = 1024
HEAD_DIM = 64
RWKV_HEADS = 8
RWKV_WIDTH = RWKV_HEADS * HEAD_DIM
DECAY_LORA = 64
ICLR_LORA = 64
GATE_LORA = 128
LORA_WIDTH = DECAY_LORA + ICLR_LORA + GATE_LORA
GROUPNORM_EPS = 64e-5
ATTN_GROUPS = ((128, 1), (512, 4), (2048, 16))
HEADS_PER_GROUP = 4
ATTN_HEADS = HEADS_PER_GROUP * len(ATTN_GROUPS)
ATTN_WIDTH = ATTN_HEADS * HEAD_DIM
GROUP_WIDTH = HEADS_PER_GROUP * HEAD_DIM
ATTN_BLOCK = 128
N_BUCKETS = 32
MAX_DISTANCE = 2048
N_EXPERTS = 32
TOP_K = 4
D_EXPERT = D_MODEL
SWIGLU_ALPHA = 1.702
SWIGLU_LIMIT = 7.0
NORM_EPS = 1e-6
NEG_INF = -1e30

RWKV_COLS = 3 * RWKV_WIDTH + 2 * LORA_WIDTH
ATTN_COLS = 3 * ATTN_WIDTH
GATE_COLS = 2 * D_MODEL

CHUNK = 64
RWKV_BATCH = 4
ATTN_UNROLL = 4
TOKEN_TILE = 512
MOE_ROWS = 512
ROUTE_TILE = 256
SLAB = 128
ROW_ALIGN = 8
EXTRA_PER = (ROW_ALIGN - 1 + ROUTE_TILE + SLAB - 1) // SLAB - 1
EXTRA_MAX = (N_EXPERTS * (ROW_ALIGN - 1) + TOP_K * ROUTE_TILE) // SLAB + 1
EXPERT_GROUP = 8
MXU_TILE = 256
LANES = 128
VMEM_LIMIT = 56 * 1024 * 1024


def _cparams(sem, vmem=VMEM_LIMIT):
    return pltpu.CompilerParams(dimension_semantics=sem, vmem_limit_bytes=vmem)


def _split2(a):
    hi = a.astype(BF16)
    lo = (a - hi.astype(F32)).astype(BF16)
    return hi, lo


def _split3(a):
    hi = a.astype(BF16)
    r = a - hi.astype(F32)
    mid = r.astype(BF16)
    lo = (r - mid.astype(F32)).astype(BF16)
    return hi, mid, lo


def _dot(a, b):
    return jnp.dot(a, b, preferred_element_type=F32)


def _dot_nt(a, b):
    return lax.dot_general(a, b, (((1,), (1,)), ((), ())), preferred_element_type=F32)


def _dot_tn(a, b):
    return lax.dot_general(a, b, (((0,), (0,)), ((), ())), preferred_element_type=F32)


def _sigmoid(x):
    return 1.0 / (1.0 + jnp.exp(-x))


def _mod_kernel(c_ref, w_ref, b_ref, o_ref):
    c = c_ref[...]
    s = c * _sigmoid(c)
    s_hi, s_lo = _split2(s)
    w_hi, w_lo = _split2(w_ref[...])
    o_ref[...] = _dot(s_hi, w_hi) + _dot(s_hi, w_lo) + _dot(s_lo, w_hi) + b_ref[...]


def _modulation(c, ada_w, ada_b):
    bsz, d = c.shape
    cols = ada_w.shape[1]
    tn = 1024
    return pl.pallas_call(
        _mod_kernel,
        grid=(cols // tn,),
        in_specs=[pl.BlockSpec((bsz, d), lambda j: (0, 0)),
                  pl.BlockSpec((d, tn), lambda j: (0, j)),
                  pl.BlockSpec((1, tn), lambda j: (0, j))],
        out_specs=pl.BlockSpec((bsz, tn), lambda j: (0, j)),
        out_shape=jax.ShapeDtypeStruct((bsz, cols), F32),
        compiler_params=_cparams(("arbitrary",)),
        name="mod",
    )(c, ada_w, ada_b.reshape(1, cols))


def _inproj_kernel(x_ref, g_ref, scale_ref, shift_ref, wr_ref, wa_ref, wg_ref,
                   pr_ref, pg_ref, a0_ref, a1_ref, a2_ref, h_ref):
    x = x_ref[...]
    tm = x.shape[0]
    ms = jnp.mean(x * x, axis=-1, keepdims=True)
    h = x * lax.rsqrt(ms + NORM_EPS) * (g_ref[...] * (1.0 + scale_ref[0])) + shift_ref[0]
    ntile = h_ref.shape[0]
    for c in range(ntile):
        h_ref[c] = h[:, c * LANES:(c + 1) * LANES]
    hb = h.astype(BF16)
    step = 512
    gcols = 3 * GROUP_WIDTH
    for w_ref, o_ref, cols in ((wr_ref, pr_ref, RWKV_COLS), (wg_ref, pg_ref, GATE_COLS),
                               (wa_ref, a0_ref, gcols)):
        for j in range(0, cols, step):
            width = min(step, cols - j)
            o_ref[:, j:j + width] = _dot(hb, w_ref[:, j:j + width]).astype(BF16)
    for gi, a_ref in ((1, a1_ref), (2, a2_ref)):
        dil = ATTN_GROUPS[gi][1]
        run = tm // dil
        hp = jnp.concatenate(
            [jnp.concatenate([h_ref[c, pl.ds(z, run, stride=dil), :] for z in range(dil)], axis=0)
             for c in range(ntile)], axis=1).astype(BF16)
        res = _dot(hp, wa_ref[:, gi * gcols:(gi + 1) * gcols]).astype(BF16)
        for z in range(dil):
            a_ref[z] = res[z * run:(z + 1) * run]


def _inproj(x2, norm_g, scale, shift, w_r, w_a, w_g, bsz, seq):
    n, d = x2.shape
    tm = TOKEN_TILE
    per = seq // tm
    gcols = 3 * GROUP_WIDTH
    const = lambda shape: pl.BlockSpec(shape, lambda i: (0, 0), pipeline_mode=pl.Buffered(1))
    row = lambda width: pl.BlockSpec((tm, width), lambda i: (i, 0))
    dil1, dil2 = ATTN_GROUPS[1][1], ATTN_GROUPS[2][1]
    res_spec = lambda dil: pl.BlockSpec((None, dil, None, tm // dil, gcols),
                                        lambda i: (i // per, 0, i % per, 0, 0))
    res_shape = lambda dil: jax.ShapeDtypeStruct((bsz, dil, per, tm // dil, gcols), BF16)
    pr, pg, a0, a1, a2 = pl.pallas_call(
        _inproj_kernel,
        grid=(n // tm,),
        in_specs=[row(d),
                  pl.BlockSpec((1, d), lambda i: (0, 0)),
                  pl.BlockSpec((1, 1, d), lambda i: (i // per, 0, 0)),
                  pl.BlockSpec((1, 1, d), lambda i: (i // per, 0, 0)),
                  const(w_r.shape), const(w_a.shape), const(w_g.shape)],
        out_specs=[row(RWKV_COLS), row(GATE_COLS), row(gcols), res_spec(dil1), res_spec(dil2)],
        out_shape=[jax.ShapeDtypeStruct((n, RWKV_COLS), BF16),
                   jax.ShapeDtypeStruct((n, GATE_COLS), BF16),
                   jax.ShapeDtypeStruct((n, gcols), BF16),
                   res_shape(dil1), res_shape(dil2)],
        scratch_shapes=[pltpu.VMEM((d // LANES, tm, LANES), F32)],
        compiler_params=_cparams(("parallel",)),
        name="inproj",
    )(x2, norm_g, scale, shift, w_r, w_a, w_g)
    qkv = [a.reshape(bsz, seq, gcols) for a in (a0, a1, a2)]
    return pr, pg, qkv


_P_W0, _P_A0, _P_KK, _P_KA, _P_RK, _P_LNW, _P_LNB, _P_MUR, _P_MUK, _P_MUV = range(10)


def _rwkv_kernel(pr_ref, par_ref, wl_ref, ones_ref, o_ref, state_ref, prev_ref, y_ref):
    c_idx = pl.program_id(1)
    nb = pr_ref.shape[0]
    rows_all = nb * CHUNK

    @pl.when(c_idx == 0)
    def _():
        state_ref[...] = jnp.zeros_like(state_ref)
        prev_ref[...] = jnp.zeros_like(prev_ref)

    rw = RWKV_WIDTH
    par = lambda i: par_ref[i:i + 1, :]
    ones_bd = ones_ref[...]

    def headsum(a):
        hi, lo = _split2(a)
        return _dot(hi, ones_bd) + _dot(lo, ones_bd)

    pr = jnp.concatenate([pr_ref[b].astype(F32) for b in range(nb)], axis=0)
    row = lax.broadcasted_iota(jnp.int32, (rows_all, 1), 0)
    first = row % CHUNK == 0

    def shifted(lo, hi):
        prev_row = prev_ref[0:1, lo:hi]
        for b in range(1, nb):
            prev_row = jnp.where(row >= b * CHUNK, prev_ref[b:b + 1, lo:hi], prev_row)
        return jnp.where(first, prev_row, pltpu.roll(pr[:, lo:hi], 1, 0))

    sh_rkv = shifted(0, 3 * rw)
    sh_lora = shifted(3 * rw + LORA_WIDTH, 3 * rw + 2 * LORA_WIDTH)
    for b in range(nb):
        prev_ref[b:b + 1, :] = pr[(b + 1) * CHUNK - 1:(b + 1) * CHUNK, :]

    lora = pr[:, 3 * rw:3 * rw + LORA_WIDTH] + sh_lora
    lane = lax.broadcasted_iota(jnp.int32, (1, LORA_WIDTH), 1)
    act = jnp.where(lane < DECAY_LORA, jnp.tanh(lora),
                    jnp.where(lane < DECAY_LORA + ICLR_LORA, lora, _sigmoid(lora)))
    lo_out = _dot(act.astype(BF16), wl_ref[...])
    w_pre = par(_P_W0) + lo_out[:, :rw]
    z = -w_pre
    softplus = jnp.maximum(z, 0.0) + jnp.log(1.0 + jnp.exp(-jnp.abs(z)))
    logd = -jnp.exp(-softplus - 0.5)
    a = _sigmoid(par(_P_A0) + lo_out[:, rw:2 * rw])
    g = lo_out[:, 2 * rw:3 * rw]

    r0, k0, v0 = pr[:, :rw], pr[:, rw:2 * rw], pr[:, 2 * rw:3 * rw]
    r = r0 + (sh_rkv[:, :rw] - r0) * par(_P_MUR)
    k = k0 + (sh_rkv[:, rw:2 * rw] - k0) * par(_P_MUK)
    v = v0 + (sh_rkv[:, 2 * rw:] - v0) * par(_P_MUV)

    kk = k * par(_P_KK)
    kk = kk / jnp.maximum(jnp.sqrt(headsum(kk * kk)), 1e-12)
    k2 = k * (1.0 + (a - 1.0) * par(_P_KA))

    ri = lax.broadcasted_iota(jnp.int32, (rows_all, rows_all), 0)
    rj = lax.broadcasted_iota(jnp.int32, (rows_all, rows_all), 1)
    tril = jnp.where(jnp.logical_and(ri >= rj, ri // CHUNK == rj // CHUNK), 1.0, 0.0).astype(BF16)
    ti = lax.broadcasted_iota(jnp.int32, (CHUNK, CHUNK), 0)
    tj = lax.broadcasted_iota(jnp.int32, (CHUNK, CHUNK), 1)
    low_incl = ti >= tj
    low_strict = ti > tj
    d_hi, d_mid, d_lo = _split3(logd)
    cum = _dot(tril, d_hi) + _dot(tril, d_mid) + _dot(tril, d_lo)
    p_incl = jnp.exp(cum)
    p_excl = jnp.exp(cum - logd)
    p_inv = jnp.exp(-cum)

    a_t = (-kk * p_excl).astype(BF16)
    b_t = (kk * a * p_inv).astype(BF16)
    k_t = (k2 * p_inv).astype(BF16)
    r_t = (r * p_incl).astype(BF16)
    v_b = v.astype(BF16)
    eye = jnp.where(ti == tj, 1.0, 0.0)

    chains = [(b, h) for b in range(nb) for h in range(RWKV_HEADS)]
    ids = range(len(chains))
    rs = [slice(b * CHUNK, (b + 1) * CHUNK) for b, _ in chains]
    ls = [slice(h * HEAD_DIM, (h + 1) * HEAD_DIM) for _, h in chains]
    ar = [jnp.concatenate([a_t[rs[i], ls[i]], r_t[rs[i], ls[i]]], axis=0) for i in ids]
    bk = [jnp.concatenate([b_t[rs[i], ls[i]], k_t[rs[i], ls[i]]], axis=0) for i in ids]
    vh = [v_b[rs[i], ls[i]] for i in ids]
    s0 = [state_ref[b, h] for b, h in chains]
    m = [_dot_nt(ar[i], bk[i]) for i in ids]
    ars = [_dot_nt(ar[i], s0[i].astype(BF16)) for i in ids]
    a_ab = [jnp.where(low_strict, m[i][:CHUNK, :CHUNK], 0.0) for i in ids]
    a_ak = [jnp.where(low_strict, m[i][:CHUNK, CHUNK:], 0.0).astype(BF16) for i in ids]
    low2 = (lax.broadcasted_iota(jnp.int32, (CHUNK, 2 * CHUNK), 0)
            >= lax.broadcasted_iota(jnp.int32, (CHUNK, 2 * CHUNK), 1) % CHUNK)
    a_r = [jnp.where(low2, m[i][CHUNK:, :], 0.0).astype(BF16) for i in ids]
    t_inv = [eye + a_ab[i] for i in ids]
    pw = [a_ab[i].astype(BF16) for i in ids]
    for _ in range(int(math.log2(CHUNK)) - 1):
        pw = [_dot(pw[i], pw[i]).astype(BF16) for i in ids]
        t_inv = [t_inv[i] + _dot(pw[i], t_inv[i].astype(BF16)) for i in ids]
    rhs = [ars[i][:CHUNK] + _dot(a_ak[i], vh[i]) for i in ids]
    ub = [_dot(t_inv[i].astype(BF16), rhs[i].astype(BF16)).astype(BF16) for i in ids]
    uv = [jnp.concatenate([ub[i], vh[i]], axis=0) for i in ids]
    for i in ids:
        y_ref[rs[i], ls[i]] = ars[i][CHUNK:] + _dot(a_r[i], uv[i])
    for i, (b, h) in enumerate(chains):
        last = (b + 1) * CHUNK - 1
        state_ref[b, h] = (s0[i] + _dot_tn(uv[i], bk[i])) * p_incl[last:last + 1, ls[i]]

    y = y_ref[...]
    inv_n = 1.0 / HEAD_DIM
    mu = headsum(y) * inv_n
    dlt = y - mu
    var = headsum(dlt * dlt) * inv_n
    yn = dlt * lax.rsqrt(var + GROUPNORM_EPS) * par(_P_LNW) + par(_P_LNB)
    bonus = headsum(r * k2 * par(_P_RK)) * v
    out = ((yn + bonus) * g).astype(BF16)
    for b in range(nb):
        o_ref[b] = out[b * CHUNK:(b + 1) * CHUNK]


def _rwkv(pr, par, w_lora, ones_bd, bsz, seq):
    n = pr.shape[0]
    nchunk = seq // CHUNK
    nb = RWKV_BATCH if bsz % RWKV_BATCH == 0 else 1
    const = lambda shape: pl.BlockSpec(shape, lambda b, c: (0, 0))
    out = pl.pallas_call(
        _rwkv_kernel,
        grid=(bsz // nb, nchunk),
        in_specs=[pl.BlockSpec((nb, CHUNK, RWKV_COLS), lambda b, c: (b, c, 0)),
                  const(par.shape), const(w_lora.shape), const(ones_bd.shape)],
        out_specs=pl.BlockSpec((nb, CHUNK, RWKV_WIDTH), lambda b, c: (b, c, 0)),
        out_shape=jax.ShapeDtypeStruct((bsz, seq, RWKV_WIDTH), BF16),
        scratch_shapes=[pltpu.VMEM((nb, RWKV_HEADS, HEAD_DIM, HEAD_DIM), F32),
                        pltpu.VMEM((8, RWKV_COLS), F32),
                        pltpu.VMEM((nb * CHUNK, RWKV_WIDTH), F32)],
        compiler_params=_cparams(("parallel", "arbitrary")),
        name="rwkv",
    )(pr.reshape(bsz, seq, RWKV_COLS), par, w_lora, ones_bd)
    return out.reshape(n, RWKV_WIDTH)


def _attn_kernel(a0_ref, a1_ref, a2_ref, bias_ref, gain_ref, bd_ref, out_ref,
                 qkv_ref, m_ref, s_ref, o_ref):
    seq = a0_ref.shape[0]
    ntile = m_ref.shape[0]
    gw = GROUP_WIDTH
    blk = ATTN_BLOCK
    bd = bd_ref[...]
    gq = gain_ref[0:1, :] * (HEAD_DIM ** -0.5)
    gk = gain_ref[1:2, :]
    lane_head = lax.broadcasted_iota(jnp.int32, (1, gw), 1) // HEAD_DIM
    qkv_ref[0:blk, :] = jnp.zeros((blk, 3 * gw), BF16)

    def headwise(parts):
        acc = parts[0]
        for h in range(1, HEADS_PER_GROUP):
            acc = jnp.where(lane_head == h, parts[h], acc)
        return acc

    for gi, a_ref in enumerate((a0_ref, a1_ref, a2_ref)):
        dil = ATTN_GROUPS[gi][1]
        nb = seq // dil // blk

        def normalise(c, carry, a_ref=a_ref):
            r0 = pl.multiple_of(c * 256, 256)
            for part, gain in ((0, gq), (1, gk)):
                t = a_ref[pl.ds(r0, 256), part * gw:(part + 1) * gw].astype(F32)
                hi, lo = _split2(t * t)
                ms = _dot(hi, bd) + _dot(lo, bd)
                qkv_ref[pl.ds(blk + r0, 256), part * gw:(part + 1) * gw] = (
                    t * lax.rsqrt(ms + NORM_EPS) * gain).astype(BF16)
            qkv_ref[pl.ds(blk + r0, 256), 2 * gw:] = a_ref[pl.ds(r0, 256), 2 * gw:]
            return carry

        lax.fori_loop(0, seq // 256, normalise, 0)

        def blocks(it, carry, gi=gi, dil=dil, nb=nb):
            ts = [it * ATTN_UNROLL + u for u in range(ATTN_UNROLL)]
            r0 = [pl.multiple_of(t * blk, blk) for t in ts]
            first = [jnp.where(t % nb == 0, 1, 0) for t in ts]
            q = [qkv_ref[pl.ds(blk + r, blk), 0:gw] for r in r0]
            kwin = [qkv_ref[pl.ds(r, 2 * blk), gw:2 * gw] for r in r0]
            vwin = [qkv_ref[pl.ds(r, 2 * blk), 2 * gw:] for r in r0]
            pairs = [(u, h) for u in range(ATTN_UNROLL) for h in range(HEADS_PER_GROUP)]
            qm = [jnp.where(lane_head == h, q[u], jnp.zeros_like(q[u])) for u, h in pairs]
            s = [_dot_nt(qm[i], kwin[u]) + bias_ref[gi, first[u], h]
                 for i, (u, h) in enumerate(pairs)]
            m = [jnp.max(si, axis=-1, keepdims=True) for si in s]
            p = [jnp.exp(si - mi) for si, mi in zip(s, m)]
            den = [jnp.sum(pi, axis=-1, keepdims=True) for pi in p]
            pv = [_dot(p[i].astype(BF16), vwin[u]) for i, (u, h) in enumerate(pairs)]
            for u, t in enumerate(ts):
                sel = slice(u * HEADS_PER_GROUP, (u + 1) * HEADS_PER_GROUP)
                o_new = headwise(pv[sel]) / headwise(
                    [jnp.broadcast_to(d, (blk, gw)) for d in den[sel]])
                lse = headwise([jnp.broadcast_to(mi + jnp.log(d), (blk, gw))
                                for mi, d in zip(m[sel], den[sel])])
                z = t // nb
                n = t % nb
                rows = pl.ds(z + n * (blk * dil), blk, stride=dil) if dil > 1 else pl.ds(r0[u], blk)
                ld = lambda ref: jnp.concatenate([ref[c, rows, :] for c in range(ntile)], axis=1)

                def st(ref, val):
                    for c in range(ntile):
                        ref[c, rows, :] = val[:, c * LANES:(c + 1) * LANES]

                if gi == 0:
                    st(m_ref, lse)
                    st(o_ref, o_new)
                else:
                    m_old = ld(m_ref)
                    m_new = jnp.maximum(m_old, lse)
                    wa = jnp.exp(m_old - m_new)
                    wb = jnp.exp(lse - m_new)
                    mix = wa * ld(o_ref) + wb * o_new
                    if gi == 1:
                        st(m_ref, m_new)
                        st(s_ref, wa + wb)
                        st(o_ref, mix)
                    else:
                        st(o_ref, mix / (wa * ld(s_ref) + wb))
            return carry

        lax.fori_loop(0, seq // blk // ATTN_UNROLL, blocks, 0)

    out_ref[...] = jnp.concatenate([o_ref[c] for c in range(ntile)], axis=1).astype(out_ref.dtype)


def _attention(qkv, bias, gains, bd, bsz, seq):
    gcols = 3 * GROUP_WIDTH
    const = lambda a: pl.BlockSpec(a.shape, lambda b: (0,) * a.ndim)
    seq_spec = pl.BlockSpec((None, seq, gcols), lambda b: (b, 0, 0))
    out = pl.pallas_call(
        _attn_kernel,
        grid=(bsz,),
        in_specs=[seq_spec, seq_spec, seq_spec, const(bias), const(gains), const(bd)],
        out_specs=pl.BlockSpec((None, seq, GROUP_WIDTH), lambda b: (b, 0, 0)),
        out_shape=jax.ShapeDtypeStruct((bsz, seq, GROUP_WIDTH), BF16),
        scratch_shapes=[pltpu.VMEM((ATTN_BLOCK + seq, gcols), BF16)]
        + [pltpu.VMEM((GROUP_WIDTH // LANES, seq, LANES), F32)] * 3,
        compiler_params=_cparams(("parallel",)),
        name="attn",
    )(*qkv, bias, gains, bd)
    return out.reshape(bsz * seq, GROUP_WIDTH)


def _t5_bucket(dist):
    max_exact = N_BUCKETS // 2
    large = max_exact + (jnp.log(jnp.maximum(dist, max_exact).astype(F32) / max_exact)
                         / math.log(MAX_DISTANCE / max_exact) * (N_BUCKETS - max_exact)).astype(jnp.int32)
    return jnp.where(dist < max_exact, dist, jnp.minimum(large, N_BUCKETS - 1))


def _band_bias(rel_bias, gi):
    window, dil = ATTN_GROUPS[gi]
    qi = jnp.arange(ATTN_BLOCK)[:, None]
    kj = jnp.arange(2 * ATTN_BLOCK)[None, :]
    steps = qi - kj + ATTN_BLOCK
    heads = slice(gi * HEADS_PER_GROUP, (gi + 1) * HEADS_PER_GROUP)
    bucket = _t5_bucket(jnp.maximum(steps, 0) * dil)
    onehot = (bucket[None] == jnp.arange(N_BUCKETS)[:, None, None]).astype(F32)
    bias = jnp.sum(onehot[:, None] * rel_bias[:, heads][:, :, None, None], axis=0)
    band = (steps >= 0) & (steps <= window // dil)
    later = jnp.where(band[None], bias.astype(F32), NEG_INF)
    first = jnp.where((kj >= ATTN_BLOCK)[None], later, NEG_INF)
    return jnp.stack([later, first])


def _post_kernel(x_ref, yr_ref, att_ref, pg_ref,
                 gate1_ref, scale2_ref, shift2_ref, g2_ref, wbr_ref, wba_ref, wout_ref,
                 rwh_ref, rwl_ref, rb_ref,
                 x1_ref, h2_ref, gates_ref, eidx_ref, rank_ref, counts_ref, base_ref, run_ref):
    i = pl.program_id(0)
    tm = x_ref.shape[0]

    @pl.when(i == 0)
    def _():
        run_ref[...] = jnp.zeros_like(run_ref)

    y_r = _dot(yr_ref[...], wbr_ref[...])
    y_a = _dot(att_ref[...], wba_ref[...])
    d = D_MODEL
    mixed = (_sigmoid(pg_ref[:, :d].astype(F32)) * y_r
             + _sigmoid(pg_ref[:, d:].astype(F32)) * y_a)
    x1 = x_ref[...] + gate1_ref[0] * _dot(mixed.astype(BF16), wout_ref[...])
    x1_ref[...] = x1

    ms = jnp.mean(x1 * x1, axis=-1, keepdims=True)
    h2 = x1 * lax.rsqrt(ms + NORM_EPS) * (g2_ref[...] * (1.0 + scale2_ref[0])) + shift2_ref[0]
    h2_ref[...] = h2.astype(h2_ref.dtype)

    h_hi, h_lo = _split2(h2)
    rw_hi, rw_lo = rwh_ref[...], rwl_ref[...]
    logits = (_dot_nt(rw_hi, h_hi) + _dot_nt(rw_hi, h_lo) + _dot_nt(rw_lo, h_hi)
              + rb_ref[:, 0:1])
    eid = lax.broadcasted_iota(jnp.int32, (N_EXPERTS, tm), 0)
    vals, hots = [], []
    lg = logits
    for k in range(TOP_K):
        m = jnp.max(lg, axis=0, keepdims=True)
        idx = jnp.min(jnp.where(lg == m, eid, N_EXPERTS), axis=0, keepdims=True)
        hot = eid == idx
        vals.append(m)
        hots.append(hot)
        eidx_ref[k:k + 1, :] = idx
        lg = jnp.where(hot, -jnp.inf, lg)
    exps = [jnp.exp(v - vals[0]) for v in vals]
    tot = exps[0] + exps[1] + exps[2] + exps[3]
    for k in range(TOP_K):
        gates_ref[k:k + 1, :] = exps[k] / tot

    chosen = jnp.zeros((N_EXPERTS, tm), F32)
    for hot in hots:
        chosen = chosen + jnp.where(hot, 1.0, 0.0)
    ti = lax.broadcasted_iota(jnp.int32, (tm, tm), 0)
    tj = lax.broadcasted_iota(jnp.int32, (tm, tm), 1)
    before = jnp.where(ti < tj, 1.0, 0.0).astype(BF16)
    run = run_ref[:, 0:1]
    base = _dot(chosen.astype(BF16), before) + run
    for k in range(TOP_K):
        rank_ref[k:k + 1, :] = jnp.sum(jnp.where(hots[k], base, 0.0), axis=0,
                                       keepdims=True).astype(jnp.int32)
    run_new = run + jnp.sum(chosen, axis=1, keepdims=True)
    run_ref[...] = jnp.broadcast_to(run_new, run_ref.shape)
    counts_ref[...] = jnp.broadcast_to(run_new, counts_ref.shape).astype(jnp.int32)
    lane = lax.broadcasted_iota(jnp.int32, base_ref.shape, 1)
    starts = jnp.zeros(base_ref.shape, F32)
    for j in range(tm // ROUTE_TILE):
        starts = jnp.where(lane == j, base[:, j * ROUTE_TILE:j * ROUTE_TILE + 1], starts)
    base_ref[...] = starts.astype(jnp.int32)


def _post(x2, y_rwkv, att, pg, gate1, scale2, shift2, norm2_g, w_br_r, w_br_a, w_out,
          rw_hi, rw_lo, rb, seq):
    n, d = x2.shape
    tm = TOKEN_TILE
    per = seq // tm
    row = lambda width: pl.BlockSpec((tm, width), lambda i: (i, 0))
    const = lambda shape: pl.BlockSpec(shape, lambda i: (0,) * len(shape))
    bvec = pl.BlockSpec((1, 1, d), lambda i: (i // per, 0, 0))
    tcol = pl.BlockSpec((TOP_K, tm), lambda i: (0, i))
    return pl.pallas_call(
        _post_kernel,
        grid=(n // tm,),
        in_specs=[row(d), row(RWKV_WIDTH), row(GROUP_WIDTH), row(GATE_COLS),
                  bvec, bvec, bvec, const((1, d)), const(w_br_r.shape), const(w_br_a.shape),
                  const(w_out.shape), const(rw_hi.shape), const(rw_lo.shape), const(rb.shape)],
        out_specs=[row(d), row(d), tcol, tcol, tcol, const((N_EXPERTS, LANES)),
                   pl.BlockSpec((None, N_EXPERTS, LANES), lambda i: (i, 0, 0))],
        out_shape=[jax.ShapeDtypeStruct((n, d), F32),
                   jax.ShapeDtypeStruct((n, d), BF16),
                   jax.ShapeDtypeStruct((TOP_K, n), F32),
                   jax.ShapeDtypeStruct((TOP_K, n), jnp.int32),
                   jax.ShapeDtypeStruct((TOP_K, n), jnp.int32),
                   jax.ShapeDtypeStruct((N_EXPERTS, LANES), jnp.int32),
                   jax.ShapeDtypeStruct((n // tm, N_EXPERTS, LANES), jnp.int32)],
        scratch_shapes=[pltpu.VMEM((N_EXPERTS, 128), F32)],
        compiler_params=_cparams(("arbitrary",)),
        name="post",
    )(x2, y_rwkv, att, pg, gate1, scale2, shift2, norm2_g, w_br_r, w_br_a, w_out,
      rw_hi, rw_lo, rb)


def _pack_halves(x):
    w = x.shape[1] // 2
    lo = lax.bitcast_convert_type(x[:, :w], jnp.uint32)
    hi = lax.bitcast_convert_type(x[:, w:], jnp.uint32)
    return (hi & jnp.uint32(0xFFFF0000)) | (lo >> 16)


def _unpack_halves(u):
    lo = lax.bitcast_convert_type(u << 16, F32).astype(BF16)
    hi = lax.bitcast_convert_type(u & jnp.uint32(0xFFFF0000), F32).astype(BF16)
    return lo, hi


def _slab_rows(eidx_ref, rank_ref, base_ref, sub):
    tt = eidx_ref.shape[1]
    eid = lax.broadcasted_iota(jnp.int32, (N_EXPERTS, tt), 0)
    which = sub % (TOKEN_TILE // ROUTE_TILE)
    basevec = base_ref[:, 0:1]
    for j in range(1, TOKEN_TILE // ROUTE_TILE):
        basevec = jnp.where(which == j, base_ref[:, j:j + 1], basevec)
    lead = basevec & (ROW_ALIGN - 1)
    rows = jnp.full((N_EXPERTS, tt), -1, jnp.int32)
    for k in range(TOP_K):
        rows = jnp.where(eidx_ref[k:k + 1, :] == eid, rank_ref[k:k + 1, :] - (basevec - lead), rows)
    return rows, lead


def _dispatch_kernel(sbase_ref, scnt_ref, pstart_ref, nx_ref, xlist_ref, fill_ref,
                     eidx_ref, rank_ref, base_ref, h2_ref, xs_ref,
                     stage_ref, extra_ref, zero_ref, carry_ref, rows_ref, sem, xsem):
    s = pl.program_id(0)
    nblk = xs_ref.shape[0] // MOE_ROWS

    @pl.when(s == 0)
    def _():
        zero_ref[...] = jnp.zeros_like(zero_ref)

        def fill(j, carry):
            @pl.when(fill_ref[j] != 0)
            def _():
                row0 = pl.multiple_of(j * MOE_ROWS, MOE_ROWS)
                cp = pltpu.make_async_copy(zero_ref, xs_ref.at[pl.ds(row0, MOE_ROWS)], xsem)
                cp.start()
                cp.wait()
            return carry

        lax.fori_loop(0, nblk, fill, 0)

    @pl.when(s == 0)
    def _():
        carry_ref[...] = jnp.zeros_like(carry_ref)

    rows, leadvec = _slab_rows(eidx_ref, rank_ref, base_ref, s)
    h2 = h2_ref[...]
    rid = lax.broadcasted_iota(jnp.int32, (SLAB, 1), 0)
    align = ROW_ALIGN

    def onehot(row, chunk):
        return jnp.where(row - chunk * SLAB == rid, 1.0, 0.0).astype(BF16)

    def lead(e):
        return sbase_ref[s * N_EXPERTS + e] & (align - 1)

    def used(e):
        return lead(e) + scnt_ref[s * N_EXPERTS + e]

    def window_start(e):
        return pl.multiple_of(pstart_ref[e] + sbase_ref[s * N_EXPERTS + e] - lead(e), align)

    total = leadvec + jnp.sum(jnp.where(rows >= 0, 1, 0), axis=1, keepdims=True)
    group0 = total - (total & (align - 1))
    gid = lax.broadcasted_iota(jnp.int32, (align, 1), 0)
    slot = s % 2
    for g0 in range(0, N_EXPERTS, EXPERT_GROUP):
        sel = jnp.concatenate([onehot(rows[e:e + 1, :], 0)
                               for e in range(g0, g0 + EXPERT_GROUP)], axis=0)
        packed = _pack_halves(_dot(sel, h2))
        for e in range(g0, g0 + EXPERT_GROUP):
            stage_ref[slot, e] = packed[(e - g0) * SLAB:(e - g0 + 1) * SLAB]
            stage_ref[slot, e, 0:align, :] = stage_ref[slot, e, 0:align, :] | carry_ref[e]
    tails = jnp.concatenate(
        [jnp.where(rows[e:e + 1, :] - group0[e:e + 1, :] == gid, 1.0, 0.0).astype(BF16)
         for e in range(N_EXPERTS)], axis=0)
    tail_rows = _pack_halves(_dot(tails, h2))
    for e in range(N_EXPERTS):
        keep = used(e) < align
        new = tail_rows[e * align:(e + 1) * align]
        carry_ref[e] = jnp.where(keep, new | carry_ref[e], new)

    @pl.when(s > 0)
    def _():
        for e in range(N_EXPERTS):
            pltpu.make_async_copy(stage_ref.at[1 - slot, e], xs_ref.at[pl.ds(0, SLAB)],
                                  sem.at[1 - slot]).wait()

    slab_copy = lambda e: pltpu.make_async_copy(
        stage_ref.at[slot, e], xs_ref.at[pl.ds(window_start(e), SLAB)], sem.at[slot])
    for e in range(N_EXPERTS):
        slab_copy(e).start()

    rows_ref[...] = rows

    def more(i, carry):
        cand = xlist_ref[s * EXTRA_MAX + i]
        e = cand // EXTRA_PER
        c = cand % EXTRA_PER + 1
        extra_ref[...] = _pack_halves(_dot(onehot(rows_ref[pl.ds(e, 1), :], c), h2))
        first = pstart_ref[e] + sbase_ref[s * N_EXPERTS + e]
        first = pl.multiple_of(first - (first & (align - 1)) + c * SLAB, align)
        cp = pltpu.make_async_copy(extra_ref, xs_ref.at[pl.ds(first, SLAB)], xsem)
        cp.start()
        cp.wait()
        return carry

    lax.fori_loop(0, nx_ref[s], more, 0)

    @pl.when(s == pl.num_programs(0) - 1)
    def _():
        for e in range(N_EXPERTS):
            slab_copy(e).wait()


def _dispatch(route, fill, eidx, rank, base, h2, nblk):
    n, d = h2.shape
    tt = ROUTE_TILE
    subs = TOKEN_TILE // tt
    tcol = pl.BlockSpec((TOP_K, tt), lambda s, *_: (0, s))
    return pl.pallas_call(
        _dispatch_kernel,
        grid_spec=pltpu.PrefetchScalarGridSpec(
            num_scalar_prefetch=6,
            grid=(n // tt,),
            in_specs=[tcol, tcol,
                      pl.BlockSpec((None, N_EXPERTS, LANES), lambda s, *_: (s // subs, 0, 0)),
                      pl.BlockSpec((tt, d), lambda s, *_: (s, 0))],
            out_specs=pl.BlockSpec(memory_space=pl.ANY),
            scratch_shapes=[pltpu.VMEM((2, N_EXPERTS, SLAB, d // 2), jnp.uint32),
                            pltpu.VMEM((SLAB, d // 2), jnp.uint32),
                            pltpu.VMEM((MOE_ROWS, d // 2), jnp.uint32),
                            pltpu.VMEM((N_EXPERTS, ROW_ALIGN, d // 2), jnp.uint32),
                            pltpu.VMEM((N_EXPERTS, tt), jnp.int32),
                            pltpu.SemaphoreType.DMA((2,)), pltpu.SemaphoreType.DMA(())]),
        out_shape=jax.ShapeDtypeStruct((nblk * MOE_ROWS, d // 2), jnp.uint32),
        compiler_params=_cparams(("arbitrary",)),
        name="dispatch",
    )(*route, fill, eidx, rank, base, h2)


def _expert_kernel(blk_ref, src_ref, valid_ref, xs_ref, w1g_ref, w1l_ref, b1g_ref, b1l_ref,
                   w2_ref, b2_ref, o_ref):
    j = pl.program_id(0)

    @pl.when(valid_ref[j] != 0)
    def _():
        lo, hi = _unpack_halves(xs_ref[...])
        half = lo.shape[1]

        def up(w_ref, b_ref):
            return _dot(lo, w_ref[:half, :]) + _dot(hi, w_ref[half:, :]) + b_ref[...]

        glu = jnp.minimum(up(w1g_ref, b1g_ref), SWIGLU_LIMIT)
        lin = jnp.clip(up(w1l_ref, b1l_ref), -SWIGLU_LIMIT, SWIGLU_LIMIT)
        act = glu * _sigmoid(SWIGLU_ALPHA * glu) * (lin + 1.0)
        o = _dot(act.astype(BF16), w2_ref[...]) + b2_ref[...]
        o_ref[...] = _pack_halves(o.astype(BF16).astype(F32))

    @pl.when(valid_ref[j] == 0)
    def _():
        o_ref[...] = jnp.zeros_like(o_ref)


def _experts(blk_e, src, valid, xs, w1g, w1l, b1g, b1l, w2, b2):
    rows, half = xs.shape
    nblk = rows // MOE_ROWS
    d, de = w1g.shape[1], w1g.shape[2]
    emap = lambda j, be, src, valid: (be[j], 0, 0)
    return pl.pallas_call(
        _expert_kernel,
        grid_spec=pltpu.PrefetchScalarGridSpec(
            num_scalar_prefetch=3,
            grid=(nblk,),
            in_specs=[pl.BlockSpec((MOE_ROWS, half), lambda j, be, src, valid: (src[j], 0)),
                      pl.BlockSpec((None, d, de), emap), pl.BlockSpec((None, d, de), emap),
                      pl.BlockSpec((None, 1, de), emap), pl.BlockSpec((None, 1, de), emap),
                      pl.BlockSpec((None, de, d), emap), pl.BlockSpec((None, 1, d), emap)],
            out_specs=pl.BlockSpec((MOE_ROWS, half), lambda j, be, src, valid: (j, 0))),
        out_shape=jax.ShapeDtypeStruct((rows, half), jnp.uint32),
        compiler_params=_cparams(("arbitrary",)),
        name="experts",
    )(blk_e, src, valid, xs, w1g, w1l, b1g, b1l, w2, b2)


def _split_w1_kernel(w_ref, perm_ref, g_ref, l_ref):
    t = MXU_TILE
    for cb in range(w_ref.shape[1] // t):
        res = _dot(w_ref[:, cb * t:(cb + 1) * t].astype(BF16), perm_ref[...])
        g_ref[:, cb * (t // 2):(cb + 1) * (t // 2)] = res[:, :t // 2].astype(BF16)
        l_ref[:, cb * (t // 2):(cb + 1) * (t // 2)] = res[:, t // 2:].astype(BF16)


def _split_w1(w1):
    ne, d, two_de = w1.shape
    t = MXU_TILE
    src = jnp.arange(t)[:, None]
    dst = jnp.arange(t)[None, :]
    perm = (src == jnp.where(dst < t // 2, 2 * dst, 2 * (dst - t // 2) + 1)).astype(BF16)
    return pl.pallas_call(
        _split_w1_kernel,
        grid=(ne,),
        in_specs=[pl.BlockSpec((None, d, two_de), lambda e: (e, 0, 0)),
                  pl.BlockSpec((t, t), lambda e: (0, 0))],
        out_specs=[pl.BlockSpec((None, d, two_de // 2), lambda e: (e, 0, 0)),
                   pl.BlockSpec((None, d, two_de // 2), lambda e: (e, 0, 0))],
        out_shape=[jax.ShapeDtypeStruct((ne, d, two_de // 2), BF16),
                   jax.ShapeDtypeStruct((ne, d, two_de // 2), BF16)],
        compiler_params=_cparams(("parallel",)),
        name="split_w1",
    )(w1, perm)


def _combine_kernel(sbase_ref, scnt_ref, pstart_ref, nx_ref, xlist_ref,
                    eidx_ref, rank_ref, gates_ref, base_ref, x1_ref, gate2_ref, o_hbm, out_ref,
                    slab_ref, extra_ref, acc_ref, rows_ref, ge_ref, sem, xsem):
    s = pl.program_id(0)
    last = pl.num_programs(0) - 1
    half = acc_ref.shape[1] // 2
    slot = s % 2

    def window_start(sub, e, chunk=0):
        first = pstart_ref[e] + sbase_ref[sub * N_EXPERTS + e]
        return pl.multiple_of(first - (first & (ROW_ALIGN - 1)) + chunk * SLAB, ROW_ALIGN)

    def slab_copy(sub, e, to):
        return pltpu.make_async_copy(o_hbm.at[pl.ds(window_start(sub, e), SLAB)],
                                     slab_ref.at[to, e], sem.at[to])

    @pl.when(s == 0)
    def _():
        for e in range(N_EXPERTS):
            slab_copy(0, e, 0).start()

    @pl.when(s < last)
    def _():
        for e in range(N_EXPERTS):
            slab_copy(s + 1, e, 1 - slot).start()

    rows, _ = _slab_rows(eidx_ref, rank_ref, base_ref, s)
    eid = lax.broadcasted_iota(jnp.int32, rows.shape, 0)
    ge = jnp.zeros(rows.shape, F32)
    for k in range(TOP_K):
        ge = jnp.where(eidx_ref[k:k + 1, :] == eid, gates_ref[k:k + 1, :], ge)
    rid = lax.broadcasted_iota(jnp.int32, (SLAB, 1), 0)

    def weights(row, gate, chunk):
        return jnp.where(row - chunk * SLAB == rid, gate, 0.0).astype(BF16)

    for e in range(N_EXPERTS):
        slab_copy(s, e, slot).wait()
    acc_lo = acc_hi = None
    for g0 in range(0, N_EXPERTS, EXPERT_GROUP):
        wt = jnp.concatenate([weights(rows[e:e + 1, :], ge[e:e + 1, :], 0)
                              for e in range(g0, g0 + EXPERT_GROUP)], axis=0)
        lo, hi = _unpack_halves(
            slab_ref[slot, g0:g0 + EXPERT_GROUP].reshape(EXPERT_GROUP * SLAB, half))
        p_lo, p_hi = _dot_tn(wt, lo), _dot_tn(wt, hi)
        acc_lo = p_lo if acc_lo is None else acc_lo + p_lo
        acc_hi = p_hi if acc_hi is None else acc_hi + p_hi
    acc_ref[:, :half] = acc_lo
    acc_ref[:, half:] = acc_hi

    rows_ref[...] = rows
    ge_ref[...] = ge

    def more(i, carry):
        cand = xlist_ref[s * EXTRA_MAX + i]
        e = cand // EXTRA_PER
        c = cand % EXTRA_PER + 1
        cp = pltpu.make_async_copy(o_hbm.at[pl.ds(window_start(s, e, c), SLAB)], extra_ref, xsem)
        cp.start()
        cp.wait()
        lo, hi = _unpack_halves(extra_ref[...])
        wt = weights(rows_ref[pl.ds(e, 1), :], ge_ref[pl.ds(e, 1), :], c)
        acc_ref[:, :half] += _dot_tn(wt, lo)
        acc_ref[:, half:] += _dot_tn(wt, hi)
        return carry

    lax.fori_loop(0, nx_ref[s], more, 0)
    out_ref[...] = x1_ref[...] + gate2_ref[0] * acc_ref[...]


def _combine(route, eidx, rank, gates, base, x1, gate2, o_rows, seq):
    n, d = x1.shape
    tt = ROUTE_TILE
    subs = TOKEN_TILE // tt
    per = seq // tt
    tcol = pl.BlockSpec((TOP_K, tt), lambda s, *_: (0, s))
    return pl.pallas_call(
        _combine_kernel,
        grid_spec=pltpu.PrefetchScalarGridSpec(
            num_scalar_prefetch=5,
            grid=(n // tt,),
            in_specs=[tcol, tcol, tcol,
                      pl.BlockSpec((None, N_EXPERTS, LANES), lambda s, *_: (s // subs, 0, 0)),
                      pl.BlockSpec((tt, d), lambda s, *_: (s, 0)),
                      pl.BlockSpec((1, 1, d), lambda s, *_: (s // per, 0, 0)),
                      pl.BlockSpec(memory_space=pl.ANY)],
            out_specs=pl.BlockSpec((tt, d), lambda s, *_: (s, 0)),
            scratch_shapes=[pltpu.VMEM((2, N_EXPERTS, SLAB, d // 2), jnp.uint32),
                            pltpu.VMEM((SLAB, d // 2), jnp.uint32),
                            pltpu.VMEM((tt, d), F32),
                            pltpu.VMEM((N_EXPERTS, tt), jnp.int32),
                            pltpu.VMEM((N_EXPERTS, tt), F32),
                            pltpu.SemaphoreType.DMA((2,)), pltpu.SemaphoreType.DMA(())]),
        out_shape=jax.ShapeDtypeStruct((n, d), F32),
        compiler_params=_cparams(("arbitrary",)),
        name="combine",
    )(*route, eidx, rank, gates, base, x1, gate2, o_rows)


def _layer(x, c, ada_w, ada_b, norm1_g, norm2_g, w_in, mu_rkv, mu_wag, w0, w1, w2, a0, a1, a2,
           g1, g2, k_k, k_a, r_k, ln_w, ln_b, qn_g, kn_g, rel_bias, w_br_rwkv, w_br_attn, w_out,
           router_w, router_b, exp_w1, exp_b1, exp_w2, exp_b2):
    bsz, seq, d = x.shape
    n = bsz * seq
    rw = RWKV_WIDTH
    assert d == D_MODEL and seq % (ATTN_BLOCK * ATTN_GROUPS[-1][1]) == 0 and seq % TOKEN_TILE == 0

    mod = _modulation(c, ada_w, ada_b)
    shift1, scale1, gate1, shift2, scale2, gate2 = [
        m.reshape(bsz, 1, d) for m in jnp.split(mod, 6, axis=-1)]

    lora_w = jnp.concatenate([w1, a1, g1], axis=1)
    lora_mu = jnp.concatenate([jnp.broadcast_to(mu_wag[0][:, None], w1.shape),
                               jnp.broadcast_to(mu_wag[1][:, None], a1.shape),
                               jnp.broadcast_to(mu_wag[2][:, None], g1.shape)], axis=1)
    w_r = jnp.concatenate([w_in[:, :3 * rw], lora_w * (1.0 - lora_mu), lora_w * lora_mu],
                          axis=1).astype(BF16)
    qa0 = 3 * rw
    acols = []
    for gi in range(len(ATTN_GROUPS)):
        for part in range(3):
            lo = qa0 + part * ATTN_WIDTH + gi * GROUP_WIDTH
            acols.append(w_in[:, lo:lo + GROUP_WIDTH])
    w_a = jnp.concatenate(acols, axis=1).astype(BF16)
    w_g = w_in[:, qa0 + 3 * ATTN_WIDTH:].astype(BF16)

    x2 = x.reshape(n, d)
    pr, pg, qkv = _inproj(x2, norm1_g.reshape(1, d), scale1, shift1, w_r, w_a, w_g, bsz, seq)

    par = jnp.stack([w0, a0, k_k, k_a, r_k.reshape(rw), ln_w, ln_b,
                     mu_rkv[0], mu_rkv[1], mu_rkv[2]]
                    + [jnp.zeros((rw,), F32)] * 6)
    w_lora = jnp.concatenate([jnp.pad(w2, ((0, 0), (0, 2 * rw))),
                              jnp.pad(a2, ((0, 0), (rw, rw))),
                              jnp.pad(g2, ((0, 0), (2 * rw, 0)))], axis=0).astype(BF16)
    hid = jnp.arange(rw) // HEAD_DIM
    ones_bd = (hid[:, None] == hid[None, :]).astype(BF16)
    y_rwkv = _rwkv(pr, par, w_lora, ones_bd, bsz, seq)

    gains = jnp.pad(jnp.stack([jnp.tile(qn_g, HEADS_PER_GROUP), jnp.tile(kn_g, HEADS_PER_GROUP)]),
                    ((0, 6), (0, 0)))
    bias = jnp.stack([_band_bias(rel_bias, gi) for gi in range(len(ATTN_GROUPS))])
    ghead = jnp.arange(GROUP_WIDTH) // HEAD_DIM
    mean_bd = ((ghead[:, None] == ghead[None, :]).astype(F32) / HEAD_DIM).astype(BF16)
    att = _attention(qkv, bias, gains, mean_bd, bsz, seq)

    rw_t = router_w.T
    rw_hi = rw_t.astype(BF16)
    rw_lo = (rw_t - rw_hi.astype(F32)).astype(BF16)
    rb = jnp.broadcast_to(router_b[:, None], (N_EXPERTS, 128))
    x1, h2, gates, eidx, rank, counts, base = _post(
        x2, y_rwkv, att, pg, gate1, scale2, shift2, norm2_g.reshape(1, d),
        w_br_rwkv.astype(BF16), w_br_attn.astype(BF16), w_out.astype(BF16), rw_hi, rw_lo, rb, seq)

    i32 = jnp.int32
    cnt = counts[:, 0]
    real = (cnt + MOE_ROWS - 1) // MOE_ROWS
    reserved = (cnt + SLAB + ROW_ALIGN + MOE_ROWS - 1) // MOE_ROWS
    bend = jnp.cumsum(reserved)
    bstart = bend - reserved
    nblk = (n * TOP_K) // MOE_ROWS + 2 * N_EXPERTS
    j = jnp.arange(nblk, dtype=i32)
    owner = jnp.clip(jnp.sum((bend[None, :] <= j[:, None]).astype(i32), axis=1), 0, N_EXPERTS - 1)
    local = j - jnp.take(bstart, owner)
    valid = local < jnp.take(real, owner)
    src = lax.cummax(jnp.where(valid, j, 0))
    blk_e = jnp.take(owner, src)
    fill = jnp.logical_or(~valid, local == jnp.take(real, owner) - 1)
    pstart = bstart * MOE_ROWS
    subs = TOKEN_TILE // ROUTE_TILE
    sbase = jnp.transpose(base[:, :, :subs], (0, 2, 1)).reshape(n // ROUTE_TILE, N_EXPERTS)
    scnt = jnp.concatenate([sbase[1:], cnt[None, :]], axis=0) - sbase
    chunks = ((sbase & (ROW_ALIGN - 1)) + scnt + SLAB - 1) // SLAB
    wanted = (jnp.arange(1, EXTRA_PER + 1, dtype=i32)[None, None, :] < chunks[:, :, None])
    wanted = wanted.reshape(n // ROUTE_TILE, N_EXPERTS * EXTRA_PER)
    xlist = jnp.argsort(jnp.logical_not(wanted), axis=1, stable=True)[:, :EXTRA_MAX]
    nx = jnp.sum(wanted.astype(i32), axis=1)
    route = (sbase.reshape(-1).astype(i32), scnt.reshape(-1).astype(i32), pstart.astype(i32),
             nx.astype(i32), xlist.reshape(-1).astype(i32))

    xs = _dispatch(route, fill.astype(i32), eidx, rank, base, h2, nblk)
    w1g, w1l = _split_w1(exp_w1)
    o_rows = _experts(blk_e.astype(i32), src.astype(i32), valid.astype(i32), xs, w1g, w1l,
                      exp_b1[:, None, 0::2], exp_b1[:, None, 1::2],
                      exp_w2.astype(BF16), exp_b2[:, None, :])
    out = _combine(route, eidx, rank, gates, base, x1, gate2, o_rows, seq)
    return out.reshape(bsz, seq, d)


def kernel(x, c, ada_w, ada_b, norm1_g, norm2_g, w_in, rwkv_mu_rkv, rwkv_mu_wag, rwkv_w0, rwkv_w1, rwkv_w2, rwkv_a0, rwkv_a1, rwkv_a2, rwkv_g1, rwkv_g2, rwkv_k_k, rwkv_k_a, rwkv_r_k, rwkv_ln_w, rwkv_ln_b, attn_qn_g, attn_kn_g, rel_bias, w_br_rwkv, w_br_attn, w_out, router_w, router_b, exp_w1, exp_b1, exp_w2, exp_b2):
    per_layer = (ada_w, ada_b, norm1_g, norm2_g, w_in, rwkv_mu_rkv, rwkv_mu_wag, rwkv_w0, rwkv_w1,
                 rwkv_w2, rwkv_a0, rwkv_a1, rwkv_a2, rwkv_g1, rwkv_g2, rwkv_k_k, rwkv_k_a,
                 rwkv_r_k, rwkv_ln_w, rwkv_ln_b, attn_qn_g, attn_kn_g)
    tail = (w_br_rwkv, w_br_attn, w_out, router_w, router_b, exp_w1, exp_b1, exp_w2, exp_b2)
    for layer in range(ada_w.shape[0]):
        head = [p[layer] for p in per_layer]
        rest = [p[layer] for p in tail]
        x = _layer(x, c, *head, rel_bias, *rest)
    return x
```

```python
import functools
import math

import jax
import jax.numpy as jnp
from jax import lax
from jax.experimental import pallas as pl
from jax.experimental.pallas import tpu as pltpu

F32 = jnp.float32
BF16 = jnp.bfloat16

D_MODEL = 1024
HEAD_DIM = 64
RWKV_HEADS = 8
RWKV_WIDTH = RWKV_HEADS * HEAD_DIM
DECAY_LORA = 64
ICLR_LORA = 64
GATE_LORA = 128
LORA_WIDTH = DECAY_LORA + ICLR_LORA + GATE_LORA
GROUPNORM_EPS = 64e-5
ATTN_GROUPS = ((128, 1), (512, 4), (2048, 16))
HEADS_PER_GROUP = 4
ATTN_HEADS = HEADS_PER_GROUP * len(ATTN_GROUPS)
ATTN_WIDTH = ATTN_HEADS * HEAD_DIM
GROUP_WIDTH = HEADS_PER_GROUP * HEAD_DIM
ATTN_BLOCK = 128
N_BUCKETS = 32
MAX_DISTANCE = 2048
N_EXPERTS = 32
TOP_K = 4
D_EXPERT = D_MODEL
SWIGLU_ALPHA = 1.702
SWIGLU_LIMIT = 7.0
NORM_EPS = 1e-6
NEG_INF = -1e30

RWKV_COLS = 3 * RWKV_WIDTH + 2 * LORA_WIDTH
ATTN_COLS = 3 * ATTN_WIDTH
GATE_COLS = 2 * D_MODEL

CHUNK = 64
RWKV_BATCH = 4
ATTN_UNROLL = 4
TOKEN_TILE = 512
MOE_ROWS = 512
ROUTE_TILE = 256
SLAB = 64
EXTRA_BATCH = 8
ROW_ALIGN = 8
EXTRA_PER = (ROW_ALIGN - 1 + ROUTE_TILE + SLAB - 1) // SLAB - 1
EXTRA_MAX = (N_EXPERTS * (ROW_ALIGN - 1) + TOP_K * ROUTE_TILE) // SLAB + 1
EXPERT_GROUP = 8
MXU_TILE = 256
LANES = 128
VMEM_LIMIT = 56 * 1024 * 1024


def _cparams(sem, vmem=VMEM_LIMIT):
    return pltpu.CompilerParams(dimension_semantics=sem, vmem_limit_bytes=vmem)


def _split2(a):
    hi = a.astype(BF16)
    lo = (a - hi.astype(F32)).astype(BF16)
    return hi, lo


def _split3(a):
    hi = a.astype(BF16)
    r = a - hi.astype(F32)
    mid = r.astype(BF16)
    lo = (r - mid.astype(F32)).astype(BF16)
    return hi, mid, lo


def _dot(a, b):
    return jnp.dot(a, b, preferred_element_type=F32)


def _dot_nt(a, b):
    return lax.dot_general(a, b, (((1,), (1,)), ((), ())), preferred_element_type=F32)


def _dot_tn(a, b):
    return lax.dot_general(a, b, (((0,), (0,)), ((), ())), preferred_element_type=F32)


def _sigmoid(x):
    return 0.5 * jnp.tanh(0.5 * x) + 0.5


def _mod_kernel(c_ref, w_ref, b_ref, o_ref):
    c = c_ref[...]
    s = c * _sigmoid(c)
    s_hi, s_lo = _split2(s)
    w_hi, w_lo = _split2(w_ref[...])
    o_ref[...] = _dot(s_hi, w_hi) + _dot(s_hi, w_lo) + _dot(s_lo, w_hi) + b_ref[...]


def _modulation(c, ada_w, ada_b):
    bsz, d = c.shape
    cols = ada_w.shape[1]
    tn = 1024
    return pl.pallas_call(
        _mod_kernel,
        grid=(cols // tn,),
        in_specs=[pl.BlockSpec((bsz, d), lambda j: (0, 0)),
                  pl.BlockSpec((d, tn), lambda j: (0, j)),
                  pl.BlockSpec((1, tn), lambda j: (0, j))],
        out_specs=pl.BlockSpec((bsz, tn), lambda j: (0, j)),
        out_shape=jax.ShapeDtypeStruct((bsz, cols), F32),
        compiler_params=_cparams(("arbitrary",)),
        name="mod",
    )(c, ada_w, ada_b.reshape(1, cols))


def _inproj_kernel(x_ref, g_ref, scale_ref, shift_ref, wr_ref, wa_ref, wg_ref,
                   pr_ref, pg_ref, a0_ref, a1_ref, a2_ref, h_ref):
    x = x_ref[...]
    tm = x.shape[0]
    ms = jnp.mean(x * x, axis=-1, keepdims=True)
    h = x * lax.rsqrt(ms + NORM_EPS) * (g_ref[...] * (1.0 + scale_ref[0])) + shift_ref[0]
    ntile = h_ref.shape[0]
    for c in range(ntile):
        h_ref[c] = h[:, c * LANES:(c + 1) * LANES]
    hb = h.astype(BF16)
    step = 512
    gcols = 3 * GROUP_WIDTH
    for w_ref, o_ref, cols in ((wr_ref, pr_ref, RWKV_COLS), (wg_ref, pg_ref, GATE_COLS),
                               (wa_ref, a0_ref, gcols)):
        for j in range(0, cols, step):
            width = min(step, cols - j)
            o_ref[:, j:j + width] = _dot(hb, w_ref[:, j:j + width]).astype(BF16)
    for gi, a_ref in ((1, a1_ref), (2, a2_ref)):
        dil = ATTN_GROUPS[gi][1]
        run = tm // dil
        hp = jnp.concatenate(
            [jnp.concatenate([h_ref[c, pl.ds(z, run, stride=dil), :] for z in range(dil)], axis=0)
             for c in range(ntile)], axis=1).astype(BF16)
        res = _dot(hp, wa_ref[:, gi * gcols:(gi + 1) * gcols]).astype(BF16)
        for z in range(dil):
            a_ref[z] = res[z * run:(z + 1) * run]


def _inproj(x2, norm_g, scale, shift, w_r, w_a, w_g, bsz, seq):
    n, d = x2.shape
    tm = TOKEN_TILE
    per = seq // tm
    gcols = 3 * GROUP_WIDTH
    const = lambda shape: pl.BlockSpec(shape, lambda i: (0, 0), pipeline_mode=pl.Buffered(1))
    row = lambda width: pl.BlockSpec((tm, width), lambda i: (i, 0))
    dil1, dil2 = ATTN_GROUPS[1][1], ATTN_GROUPS[2][1]
    res_spec = lambda dil: pl.BlockSpec((None, dil, None, tm // dil, gcols),
                                        lambda i: (i // per, 0, i % per, 0, 0))
    res_shape = lambda dil: jax.ShapeDtypeStruct((bsz, dil, per, tm // dil, gcols), BF16)
    pr, pg, a0, a1, a2 = pl.pallas_call(
        _inproj_kernel,
        grid=(n // tm,),
        in_specs=[row(d),
                  pl.BlockSpec((1, d), lambda i: (0, 0)),
                  pl.BlockSpec((1, 1, d), lambda i: (i // per, 0, 0)),
                  pl.BlockSpec((1, 1, d), lambda i: (i // per, 0, 0)),
                  const(w_r.shape), const(w_a.shape), const(w_g.shape)],
        out_specs=[row(RWKV_COLS), row(GATE_COLS), row(gcols), res_spec(dil1), res_spec(dil2)],
        out_shape=[jax.ShapeDtypeStruct((n, RWKV_COLS), BF16),
                   jax.ShapeDtypeStruct((n, GATE_COLS), BF16),
                   jax.ShapeDtypeStruct((n, gcols), BF16),
                   res_shape(dil1), res_shape(dil2)],
        scratch_shapes=[pltpu.VMEM((d // LANES, tm, LANES), F32)],
        compiler_params=_cparams(("parallel",)),
        name="inproj",
    )(x2, norm_g, scale, shift, w_r, w_a, w_g)
    qkv = [a.reshape(bsz, seq, gcols) for a in (a0, a1, a2)]
    return pr, pg, qkv


_P_W0, _P_A0, _P_KK, _P_KA, _P_RK, _P_LNW, _P_LNB, _P_MUR, _P_MUK, _P_MUV = range(10)


def _rwkv_kernel(pr_ref, par_ref, wl_ref, ones_ref, o_ref, state_ref, prev_ref, y_ref):
    c_idx = pl.program_id(1)
    nb = pr_ref.shape[0]
    rows_all = nb * CHUNK

    @pl.when(c_idx == 0)
    def _():
        state_ref[...] = jnp.zeros_like(state_ref)
        prev_ref[...] = jnp.zeros_like(prev_ref)

    rw = RWKV_WIDTH
    par = lambda i: par_ref[i:i + 1, :]
    ones_bd = ones_ref[...]

    def headsum(a):
        hi, lo = _split2(a)
        return _dot(hi, ones_bd) + _dot(lo, ones_bd)

    pr = jnp.concatenate([pr_ref[b].astype(F32) for b in range(nb)], axis=0)
    row = lax.broadcasted_iota(jnp.int32, (rows_all, 1), 0)
    first = row % CHUNK == 0

    def shifted(lo, hi):
        prev_row = prev_ref[0:1, lo:hi]
        for b in range(1, nb):
            prev_row = jnp.where(row >= b * CHUNK, prev_ref[b:b + 1, lo:hi], prev_row)
        return jnp.where(first, prev_row, pltpu.roll(pr[:, lo:hi], 1, 0))

    sh_rkv = shifted(0, 3 * rw)
    sh_lora = shifted(3 * rw + LORA_WIDTH, 3 * rw + 2 * LORA_WIDTH)
    for b in range(nb):
        prev_ref[b:b + 1, :] = pr[(b + 1) * CHUNK - 1:(b + 1) * CHUNK, :]

    lora = pr[:, 3 * rw:3 * rw + LORA_WIDTH] + sh_lora
    lane = lax.broadcasted_iota(jnp.int32, (1, LORA_WIDTH), 1)
    act = jnp.where(lane < DECAY_LORA, jnp.tanh(lora),
                    jnp.where(lane < DECAY_LORA + ICLR_LORA, lora, _sigmoid(lora)))
    lo_out = _dot(act.astype(BF16), wl_ref[...])
    w_pre = par(_P_W0) + lo_out[:, :rw]
    z = -w_pre
    softplus = jnp.maximum(z, 0.0) + jnp.log(1.0 + jnp.exp(-jnp.abs(z)))
    logd = -jnp.exp(-softplus - 0.5)
    a = _sigmoid(par(_P_A0) + lo_out[:, rw:2 * rw])
    g = lo_out[:, 2 * rw:3 * rw]

    r0, k0, v0 = pr[:, :rw], pr[:, rw:2 * rw], pr[:, 2 * rw:3 * rw]
    r = r0 + (sh_rkv[:, :rw] - r0) * par(_P_MUR)
    k = k0 + (sh_rkv[:, rw:2 * rw] - k0) * par(_P_MUK)
    v = v0 + (sh_rkv[:, 2 * rw:] - v0) * par(_P_MUV)

    kk = k * par(_P_KK)
    kk = kk / jnp.maximum(jnp.sqrt(headsum(kk * kk)), 1e-12)
    k2 = k * (1.0 + (a - 1.0) * par(_P_KA))

    ri = lax.broadcasted_iota(jnp.int32, (rows_all, rows_all), 0)
    rj = lax.broadcasted_iota(jnp.int32, (rows_all, rows_all), 1)
    tril = jnp.where(jnp.logical_and(ri >= rj, ri // CHUNK == rj // CHUNK), 1.0, 0.0).astype(BF16)
    ti = lax.broadcasted_iota(jnp.int32, (CHUNK, CHUNK), 0)
    tj = lax.broadcasted_iota(jnp.int32, (CHUNK, CHUNK), 1)
    low_incl = ti >= tj
    low_strict = ti > tj
    d_hi, d_mid, d_lo = _split3(logd)
    cum = _dot(tril, d_hi) + _dot(tril, d_mid) + _dot(tril, d_lo)
    p_incl = jnp.exp(cum)
    p_excl = jnp.exp(cum - logd)
    p_inv = jnp.exp(-cum)

    a_t = (-kk * p_excl).astype(BF16)
    b_t = (kk * a * p_inv).astype(BF16)
    k_t = (k2 * p_inv).astype(BF16)
    r_t = (r * p_incl).astype(BF16)
    v_b = v.astype(BF16)
    eye = jnp.where(ti == tj, 1.0, 0.0)

    chains = [(b, h) for b in range(nb) for h in range(RWKV_HEADS)]
    ids = range(len(chains))
    rs = [slice(b * CHUNK, (b + 1) * CHUNK) for b, _ in chains]
    ls = [slice(h * HEAD_DIM, (h + 1) * HEAD_DIM) for _, h in chains]
    ar = [jnp.concatenate([a_t[rs[i], ls[i]], r_t[rs[i], ls[i]]], axis=0) for i in ids]
    bk = [jnp.concatenate([b_t[rs[i], ls[i]], k_t[rs[i], ls[i]]], axis=0) for i in ids]
    vh = [v_b[rs[i], ls[i]] for i in ids]
    s0 = [state_ref[b, h] for b, h in chains]
    m = [_dot_nt(ar[i], bk[i]) for i in ids]
    ars = [_dot_nt(ar[i], s0[i].astype(BF16)) for i in ids]
    a_ab = [jnp.where(low_strict, m[i][:CHUNK, :CHUNK], 0.0) for i in ids]
    a_ak = [jnp.where(low_strict, m[i][:CHUNK, CHUNK:], 0.0).astype(BF16) for i in ids]
    low2 = (lax.broadcasted_iota(jnp.int32, (CHUNK, 2 * CHUNK), 0)
            >= lax.broadcasted_iota(jnp.int32, (CHUNK, 2 * CHUNK), 1) % CHUNK)
    a_r = [jnp.where(low2, m[i][CHUNK:, :], 0.0).astype(BF16) for i in ids]
    t_inv = [eye + a_ab[i] for i in ids]
    pw = [a_ab[i].astype(BF16) for i in ids]
    for _ in range(int(math.log2(CHUNK)) - 1):
        pw = [_dot(pw[i], pw[i]).astype(BF16) for i in ids]
        t_inv = [t_inv[i] + _dot(pw[i], t_inv[i].astype(BF16)) for i in ids]
    rhs = [ars[i][:CHUNK] + _dot(a_ak[i], vh[i]) for i in ids]
    ub = [_dot(t_inv[i].astype(BF16), rhs[i].astype(BF16)).astype(BF16) for i in ids]
    uv = [jnp.concatenate([ub[i], vh[i]], axis=0) for i in ids]
    for i in ids:
        y_ref[rs[i], ls[i]] = ars[i][CHUNK:] + _dot(a_r[i], uv[i])
    for i, (b, h) in enumerate(chains):
        last = (b + 1) * CHUNK - 1
        state_ref[b, h] = (s0[i] + _dot_tn(uv[i], bk[i])) * p_incl[last:last + 1, ls[i]]

    y = y_ref[...]
    inv_n = 1.0 / HEAD_DIM
    mu = headsum(y) * inv_n
    dlt = y - mu
    var = headsum(dlt * dlt) * inv_n
    yn = dlt * lax.rsqrt(var + GROUPNORM_EPS) * par(_P_LNW) + par(_P_LNB)
    bonus = headsum(r * k2 * par(_P_RK)) * v
    out = ((yn + bonus) * g).astype(BF16)
    for b in range(nb):
        o_ref[b] = out[b * CHUNK:(b + 1) * CHUNK]


def _rwkv(pr, par, w_lora, ones_bd, bsz, seq):
    n = pr.shape[0]
    nchunk = seq // CHUNK
    nb = RWKV_BATCH if bsz % RWKV_BATCH == 0 else 1
    const = lambda shape: pl.BlockSpec(shape, lambda b, c: (0, 0))
    out = pl.pallas_call(
        _rwkv_kernel,
        grid=(bsz // nb, nchunk),
        in_specs=[pl.BlockSpec((nb, CHUNK, RWKV_COLS), lambda b, c: (b, c, 0)),
                  const(par.shape), const(w_lora.shape), const(ones_bd.shape)],
        out_specs=pl.BlockSpec((nb, CHUNK, RWKV_WIDTH), lambda b, c: (b, c, 0)),
        out_shape=jax.ShapeDtypeStruct((bsz, seq, RWKV_WIDTH), BF16),
        scratch_shapes=[pltpu.VMEM((nb, RWKV_HEADS, HEAD_DIM, HEAD_DIM), F32),
                        pltpu.VMEM((8, RWKV_COLS), F32),
                        pltpu.VMEM((nb * CHUNK, RWKV_WIDTH), F32)],
        compiler_params=_cparams(("parallel", "arbitrary")),
        name="rwkv",
    )(pr.reshape(bsz, seq, RWKV_COLS), par, w_lora, ones_bd)
    return out.reshape(n, RWKV_WIDTH)


def _attn_kernel(a0_ref, a1_ref, a2_ref, bias_ref, gain_ref, bd_ref, out_ref,
                 qkv_ref, m_ref, s_ref, o_ref):
    seq = a0_ref.shape[0]
    ntile = m_ref.shape[0]
    gw = GROUP_WIDTH
    blk = ATTN_BLOCK
    bd = bd_ref[...]
    gq = gain_ref[0:1, :] * (HEAD_DIM ** -0.5)
    gk = gain_ref[1:2, :]
    lane_head = lax.broadcasted_iota(jnp.int32, (1, gw), 1) // HEAD_DIM
    qkv_ref[0:blk, :] = jnp.zeros((blk, 3 * gw), BF16)

    def headwise(parts):
        acc = parts[0]
        for h in range(1, HEADS_PER_GROUP):
            acc = jnp.where(lane_head == h, parts[h], acc)
        return acc

    for gi, a_ref in enumerate((a0_ref, a1_ref, a2_ref)):
        dil = ATTN_GROUPS[gi][1]
        nb = seq // dil // blk

        def normalise(c, carry, a_ref=a_ref):
            r0 = pl.multiple_of(c * 256, 256)
            for part, gain in ((0, gq), (1, gk)):
                t = a_ref[pl.ds(r0, 256), part * gw:(part + 1) * gw].astype(F32)
                hi, lo = _split2(t * t)
                ms = _dot(hi, bd) + _dot(lo, bd)
                qkv_ref[pl.ds(blk + r0, 256), part * gw:(part + 1) * gw] = (
                    t * lax.rsqrt(ms + NORM_EPS) * gain).astype(BF16)
            qkv_ref[pl.ds(blk + r0, 256), 2 * gw:] = a_ref[pl.ds(r0, 256), 2 * gw:]
            return carry

        lax.fori_loop(0, seq // 256, normalise, 0)

        def blocks(it, carry, gi=gi, dil=dil, nb=nb):
            ts = [it * ATTN_UNROLL + u for u in range(ATTN_UNROLL)]
            r0 = [pl.multiple_of(t * blk, blk) for t in ts]
            first = [jnp.where(t % nb == 0, 1, 0) for t in ts]
            q = [qkv_ref[pl.ds(blk + r, blk), 0:gw] for r in r0]
            kwin = [qkv_ref[pl.ds(r, 2 * blk), gw:2 * gw] for r in r0]
            vwin = [qkv_ref[pl.ds(r, 2 * blk), 2 * gw:] for r in r0]
            pairs = [(u, h) for u in range(ATTN_UNROLL) for h in range(HEADS_PER_GROUP)]
            qm = [jnp.where(lane_head == h, q[u], jnp.zeros_like(q[u])) for u, h in pairs]
            s = [_dot_nt(qm[i], kwin[u]) + bias_ref[gi, first[u], h]
                 for i, (u, h) in enumerate(pairs)]
            m = [jnp.max(si, axis=-1, keepdims=True) for si in s]
            p = [jnp.exp(si - mi) for si, mi in zip(s, m)]
            den = [jnp.sum(pi, axis=-1, keepdims=True) for pi in p]
            pv = [_dot(p[i].astype(BF16), vwin[u]) for i, (u, h) in enumerate(pairs)]
            for u, t in enumerate(ts):
                sel = slice(u * HEADS_PER_GROUP, (u + 1) * HEADS_PER_GROUP)
                o_new = headwise(pv[sel]) / headwise(
                    [jnp.broadcast_to(d, (blk, gw)) for d in den[sel]])
                lse = headwise([jnp.broadcast_to(mi + jnp.log(d), (blk, gw))
                                for mi, d in zip(m[sel], den[sel])])
                z = t // nb
                n = t % nb
                rows = pl.ds(z + n * (blk * dil), blk, stride=dil) if dil > 1 else pl.ds(r0[u], blk)
                ld = lambda ref: jnp.concatenate([ref[c, rows, :] for c in range(ntile)], axis=1)

                def st(ref, val):
                    for c in range(ntile):
                        ref[c, rows, :] = val[:, c * LANES:(c + 1) * LANES]

                if gi == 0:
                    st(m_ref, lse)
                    st(o_ref, o_new)
                else:
                    m_old = ld(m_ref)
                    m_new = jnp.maximum(m_old, lse)
                    wa = jnp.exp(m_old - m_new)
                    wb = jnp.exp(lse - m_new)
                    mix = wa * ld(o_ref) + wb * o_new
                    if gi == 1:
                        st(m_ref, m_new)
                        st(s_ref, wa + wb)
                        st(o_ref, mix)
                    else:
                        st(o_ref, mix / (wa * ld(s_ref) + wb))
            return carry

        lax.fori_loop(0, seq // blk // ATTN_UNROLL, blocks, 0)

    out_ref[...] = jnp.concatenate([o_ref[c] for c in range(ntile)], axis=1).astype(out_ref.dtype)


def _attention(qkv, bias, gains, bd, bsz, seq):
    gcols = 3 * GROUP_WIDTH
    const = lambda a: pl.BlockSpec(a.shape, lambda b: (0,) * a.ndim)
    seq_spec = pl.BlockSpec((None, seq, gcols), lambda b: (b, 0, 0))
    out = pl.pallas_call(
        _attn_kernel,
        grid=(bsz,),
        in_specs=[seq_spec, seq_spec, seq_spec, const(bias), const(gains), const(bd)],
        out_specs=pl.BlockSpec((None, seq, GROUP_WIDTH), lambda b: (b, 0, 0)),
        out_shape=jax.ShapeDtypeStruct((bsz, seq, GROUP_WIDTH), BF16),
        scratch_shapes=[pltpu.VMEM((ATTN_BLOCK + seq, gcols), BF16)]
        + [pltpu.VMEM((GROUP_WIDTH // LANES, seq, LANES), F32)] * 3,
        compiler_params=_cparams(("parallel",)),
        name="attn",
    )(*qkv, bias, gains, bd)
    return out.reshape(bsz * seq, GROUP_WIDTH)


def _t5_bucket(dist):
    max_exact = N_BUCKETS // 2
    large = max_exact + (jnp.log(jnp.maximum(dist, max_exact).astype(F32) / max_exact)
                         / math.log(MAX_DISTANCE / max_exact) * (N_BUCKETS - max_exact)).astype(jnp.int32)
    return jnp.where(dist < max_exact, dist, jnp.minimum(large, N_BUCKETS - 1))


def _band_bias(rel_bias, gi):
    window, dil = ATTN_GROUPS[gi]
    qi = jnp.arange(ATTN_BLOCK)[:, None]
    kj = jnp.arange(2 * ATTN_BLOCK)[None, :]
    steps = qi - kj + ATTN_BLOCK
    heads = slice(gi * HEADS_PER_GROUP, (gi + 1) * HEADS_PER_GROUP)
    bucket = _t5_bucket(jnp.maximum(steps, 0) * dil)
    onehot = (bucket[None] == jnp.arange(N_BUCKETS)[:, None, None]).astype(F32)
    bias = jnp.sum(onehot[:, None] * rel_bias[:, heads][:, :, None, None], axis=0)
    band = (steps >= 0) & (steps <= window // dil)
    later = jnp.where(band[None], bias.astype(F32), NEG_INF)
    first = jnp.where((kj >= ATTN_BLOCK)[None], later, NEG_INF)
    return jnp.stack([later, first])


def _post_kernel(x_ref, yr_ref, att_ref, pg_ref,
                 gate1_ref, scale2_ref, shift2_ref, g2_ref, wbr_ref, wba_ref, wout_ref,
                 rwh_ref, rwl_ref, rb_ref,
                 x1_ref, h2_ref, gates_ref, eidx_ref, rank_ref, counts_ref, base_ref, run_ref):
    i = pl.program_id(0)
    tm = x_ref.shape[0]

    @pl.when(i == 0)
    def _():
        run_ref[...] = jnp.zeros_like(run_ref)

    y_r = _dot(yr_ref[...], wbr_ref[...])
    y_a = _dot(att_ref[...], wba_ref[...])
    d = D_MODEL
    mixed = (_sigmoid(pg_ref[:, :d].astype(F32)) * y_r
             + _sigmoid(pg_ref[:, d:].astype(F32)) * y_a)
    x1 = x_ref[...] + gate1_ref[0] * _dot(mixed.astype(BF16), wout_ref[...])
    x1_ref[...] = x1

    ms = jnp.mean(x1 * x1, axis=-1, keepdims=True)
    h2 = x1 * lax.rsqrt(ms + NORM_EPS) * (g2_ref[...] * (1.0 + scale2_ref[0])) + shift2_ref[0]
    h2_ref[...] = h2.astype(h2_ref.dtype)

    h_hi, h_lo = _split2(h2)
    rw_hi, rw_lo = rwh_ref[...], rwl_ref[...]
    logits = (_dot_nt(rw_hi, h_hi) + _dot_nt(rw_hi, h_lo) + _dot_nt(rw_lo, h_hi)
              + rb_ref[:, 0:1])
    eid = lax.broadcasted_iota(jnp.int32, (N_EXPERTS, tm), 0)
    vals, hots = [], []
    lg = logits
    for k in range(TOP_K):
        m = jnp.max(lg, axis=0, keepdims=True)
        idx = jnp.min(jnp.where(lg == m, eid, N_EXPERTS), axis=0, keepdims=True)
        hot = eid == idx
        vals.append(m)
        hots.append(hot)
        eidx_ref[k:k + 1, :] = idx
        lg = jnp.where(hot, -jnp.inf, lg)
    exps = [jnp.exp(v - vals[0]) for v in vals]
    tot = exps[0] + exps[1] + exps[2] + exps[3]
    for k in range(TOP_K):
        gates_ref[k:k + 1, :] = exps[k] / tot

    chosen = jnp.zeros((N_EXPERTS, tm), F32)
    for hot in hots:
        chosen = chosen + jnp.where(hot, 1.0, 0.0)
    ti = lax.broadcasted_iota(jnp.int32, (tm, tm), 0)
    tj = lax.broadcasted_iota(jnp.int32, (tm, tm), 1)
    before = jnp.where(ti < tj, 1.0, 0.0).astype(BF16)
    run = run_ref[:, 0:1]
    base = _dot(chosen.astype(BF16), before) + run
    for k in range(TOP_K):
        rank_ref[k:k + 1, :] = jnp.sum(jnp.where(hots[k], base, 0.0), axis=0,
                                       keepdims=True).astype(jnp.int32)
    run_new = run + jnp.sum(chosen, axis=1, keepdims=True)
    run_ref[...] = jnp.broadcast_to(run_new, run_ref.shape)
    counts_ref[...] = jnp.broadcast_to(run_new, counts_ref.shape).astype(jnp.int32)
    lane = lax.broadcasted_iota(jnp.int32, base_ref.shape, 1)
    starts = jnp.zeros(base_ref.shape, F32)
    for j in range(tm // ROUTE_TILE):
        starts = jnp.where(lane == j, base[:, j * ROUTE_TILE:j * ROUTE_TILE + 1], starts)
    base_ref[...] = starts.astype(jnp.int32)


def _post(x2, y_rwkv, att, pg, gate1, scale2, shift2, norm2_g, w_br_r, w_br_a, w_out,
          rw_hi, rw_lo, rb, seq):
    n, d = x2.shape
    tm = TOKEN_TILE
    per = seq // tm
    row = lambda width: pl.BlockSpec((tm, width), lambda i: (i, 0))
    const = lambda shape: pl.BlockSpec(shape, lambda i: (0,) * len(shape))
    bvec = pl.BlockSpec((1, 1, d), lambda i: (i // per, 0, 0))
    tcol = pl.BlockSpec((TOP_K, tm), lambda i: (0, i))
    return pl.pallas_call(
        _post_kernel,
        grid=(n // tm,),
        in_specs=[row(d), row(RWKV_WIDTH), row(GROUP_WIDTH), row(GATE_COLS),
                  bvec, bvec, bvec, const((1, d)), const(w_br_r.shape), const(w_br_a.shape),
                  const(w_out.shape), const(rw_hi.shape), const(rw_lo.shape), const(rb.shape)],
        out_specs=[row(d), row(d), tcol, tcol, tcol, const((N_EXPERTS, LANES)),
                   pl.BlockSpec((None, N_EXPERTS, LANES), lambda i: (i, 0, 0))],
        out_shape=[jax.ShapeDtypeStruct((n, d), F32),
                   jax.ShapeDtypeStruct((n, d), BF16),
                   jax.ShapeDtypeStruct((TOP_K, n), F32),
                   jax.ShapeDtypeStruct((TOP_K, n), jnp.int32),
                   jax.ShapeDtypeStruct((TOP_K, n), jnp.int32),
                   jax.ShapeDtypeStruct((N_EXPERTS, LANES), jnp.int32),
                   jax.ShapeDtypeStruct((n // tm, N_EXPERTS, LANES), jnp.int32)],
        scratch_shapes=[pltpu.VMEM((N_EXPERTS, 128), F32)],
        compiler_params=_cparams(("arbitrary",)),
        name="post",
    )(x2, y_rwkv, att, pg, gate1, scale2, shift2, norm2_g, w_br_r, w_br_a, w_out,
      rw_hi, rw_lo, rb)


def _pack_halves(x):
    w = x.shape[1] // 2
    lo = lax.bitcast_convert_type(x[:, :w], jnp.uint32)
    hi = lax.bitcast_convert_type(x[:, w:], jnp.uint32)
    return (hi & jnp.uint32(0xFFFF0000)) | (lo >> 16)


def _unpack_halves(u):
    lo = lax.bitcast_convert_type(u << 16, F32).astype(BF16)
    hi = lax.bitcast_convert_type(u & jnp.uint32(0xFFFF0000), F32).astype(BF16)
    return lo, hi


def _slab_rows(eidx_ref, rank_ref, base_ref, sub):
    tt = eidx_ref.shape[1]
    eid = lax.broadcasted_iota(jnp.int32, (N_EXPERTS, tt), 0)
    which = sub % (TOKEN_TILE // ROUTE_TILE)
    basevec = base_ref[:, 0:1]
    for j in range(1, TOKEN_TILE // ROUTE_TILE):
        basevec = jnp.where(which == j, base_ref[:, j:j + 1], basevec)
    lead = basevec & (ROW_ALIGN - 1)
    rows = jnp.full((N_EXPERTS, tt), -1, jnp.int32)
    for k in range(TOP_K):
        rows = jnp.where(eidx_ref[k:k + 1, :] == eid, rank_ref[k:k + 1, :] - (basevec - lead), rows)
    return rows, lead


def _dispatch_kernel(sbase_ref, scnt_ref, pstart_ref, nx_ref, xlist_ref, fill_ref,
                     eidx_ref, rank_ref, base_ref, h2_ref, xs_ref,
                     stage_ref, xstage_ref, extra_ref, zero_ref, carry_ref, rows_ref, sem, xsem):
    s = pl.program_id(0)
    nblk = xs_ref.shape[0] // MOE_ROWS

    @pl.when(s == 0)
    def _():
        zero_ref[...] = jnp.zeros_like(zero_ref)

        def fill(j, carry):
            @pl.when(fill_ref[j] != 0)
            def _():
                row0 = pl.multiple_of(j * MOE_ROWS, MOE_ROWS)
                cp = pltpu.make_async_copy(zero_ref, xs_ref.at[pl.ds(row0, MOE_ROWS)], xsem)
                cp.start()
                cp.wait()
            return carry

        lax.fori_loop(0, nblk, fill, 0)

    @pl.when(s == 0)
    def _():
        carry_ref[...] = jnp.zeros_like(carry_ref)

    rows, leadvec = _slab_rows(eidx_ref, rank_ref, base_ref, s)
    h2 = h2_ref[...]
    rid = lax.broadcasted_iota(jnp.int32, (SLAB, 1), 0)
    align = ROW_ALIGN

    def onehot(row, chunk):
        return jnp.where(row - chunk * SLAB == rid, 1.0, 0.0).astype(BF16)

    def lead(e):
        return sbase_ref[s * N_EXPERTS + e] & (align - 1)

    def used(e):
        return lead(e) + scnt_ref[s * N_EXPERTS + e]

    def window_start(e):
        return pl.multiple_of(pstart_ref[e] + sbase_ref[s * N_EXPERTS + e] - lead(e), align)

    total = leadvec + jnp.sum(jnp.where(rows >= 0, 1, 0), axis=1, keepdims=True)
    group0 = total - (total & (align - 1))
    gid = lax.broadcasted_iota(jnp.int32, (align, 1), 0)
    slot = s % 2
    for g0 in range(0, N_EXPERTS, EXPERT_GROUP):
        sel = jnp.concatenate([onehot(rows[e:e + 1, :], 0)
                               for e in range(g0, g0 + EXPERT_GROUP)], axis=0)
        packed = _pack_halves(_dot(sel, h2))
        for e in range(g0, g0 + EXPERT_GROUP):
            stage_ref[slot, e] = packed[(e - g0) * SLAB:(e - g0 + 1) * SLAB]
            stage_ref[slot, e, 0:align, :] = stage_ref[slot, e, 0:align, :] | carry_ref[e]
    tails = jnp.concatenate(
        [jnp.where(rows[e:e + 1, :] - group0[e:e + 1, :] == gid, 1.0, 0.0).astype(BF16)
         for e in range(N_EXPERTS)], axis=0)
    tail_rows = _pack_halves(_dot(tails, h2))
    for e in range(N_EXPERTS):
        keep = used(e) < align
        new = tail_rows[e * align:(e + 1) * align]
        carry_ref[e] = jnp.where(keep, new | carry_ref[e], new)

    rows_ref[...] = rows

    def extra(i):
        cand = xlist_ref[s * EXTRA_MAX + i]
        e = cand // EXTRA_PER
        c = cand % EXTRA_PER + 1
        first = pstart_ref[e] + sbase_ref[s * N_EXPERTS + e]
        return e, c, pl.multiple_of(first - (first & (align - 1)) + c * SLAB, align)

    sels = []
    for i in range(EXTRA_BATCH):
        e, c, _ = extra(i)
        hit = jnp.logical_and(rows_ref[pl.ds(e, 1), :] - c * SLAB == rid, i < nx_ref[s])
        sels.append(jnp.where(hit, 1.0, 0.0).astype(BF16))
    xpacked = _pack_halves(_dot(jnp.concatenate(sels, axis=0), h2))
    for i in range(EXTRA_BATCH):
        xstage_ref[slot, i] = xpacked[i * SLAB:(i + 1) * SLAB]

    @pl.when(s > 0)
    def _():
        for e in range(N_EXPERTS):
            pltpu.make_async_copy(stage_ref.at[1 - slot, e], xs_ref.at[pl.ds(0, SLAB)],
                                  sem.at[1 - slot]).wait()
        for i in range(EXTRA_BATCH):
            @pl.when(i < nx_ref[s - 1])
            def _():
                pltpu.make_async_copy(xstage_ref.at[1 - slot, i], xs_ref.at[pl.ds(0, SLAB)],
                                      sem.at[1 - slot]).wait()

    slab_copy = lambda e: pltpu.make_async_copy(
        stage_ref.at[slot, e], xs_ref.at[pl.ds(window_start(e), SLAB)], sem.at[slot])
    xslab_copy = lambda i: pltpu.make_async_copy(
        xstage_ref.at[slot, i], xs_ref.at[pl.ds(extra(i)[2], SLAB)], sem.at[slot])
    for e in range(N_EXPERTS):
        slab_copy(e).start()
    for i in range(EXTRA_BATCH):
        @pl.when(i < nx_ref[s])
        def _():
            xslab_copy(i).start()

    def more(i, carry):
        e, c, first = extra(i)
        extra_ref[...] = _pack_halves(_dot(onehot(rows_ref[pl.ds(e, 1), :], c), h2))
        cp = pltpu.make_async_copy(extra_ref, xs_ref.at[pl.ds(first, SLAB)], xsem)
        cp.start()
        cp.wait()
        return carry

    lax.fori_loop(EXTRA_BATCH, nx_ref[s], more, 0)

    @pl.when(s == pl.num_programs(0) - 1)
    def _():
        for e in range(N_EXPERTS):
            slab_copy(e).wait()
        for i in range(EXTRA_BATCH):
            @pl.when(i < nx_ref[s])
            def _():
                xslab_copy(i).wait()


def _dispatch(route, fill, eidx, rank, base, h2, nblk):
    n, d = h2.shape
    tt = ROUTE_TILE
    subs = TOKEN_TILE // tt
    tcol = pl.BlockSpec((TOP_K, tt), lambda s, *_: (0, s))
    return pl.pallas_call(
        _dispatch_kernel,
        grid_spec=pltpu.PrefetchScalarGridSpec(
            num_scalar_prefetch=6,
            grid=(n // tt,),
            in_specs=[tcol, tcol,
                      pl.BlockSpec((None, N_EXPERTS, LANES), lambda s, *_: (s // subs, 0, 0)),
                      pl.BlockSpec((tt, d), lambda s, *_: (s, 0))],
            out_specs=pl.BlockSpec(memory_space=pl.ANY),
            scratch_shapes=[pltpu.VMEM((2, N_EXPERTS, SLAB, d // 2), jnp.uint32),
                            pltpu.VMEM((2, EXTRA_BATCH, SLAB, d // 2), jnp.uint32),
                            pltpu.VMEM((SLAB, d // 2), jnp.uint32),
                            pltpu.VMEM((MOE_ROWS, d // 2), jnp.uint32),
                            pltpu.VMEM((N_EXPERTS, ROW_ALIGN, d // 2), jnp.uint32),
                            pltpu.VMEM((N_EXPERTS, tt), jnp.int32),
                            pltpu.SemaphoreType.DMA((2,)), pltpu.SemaphoreType.DMA(())]),
        out_shape=jax.ShapeDtypeStruct((nblk * MOE_ROWS, d // 2), jnp.uint32),
        compiler_params=_cparams(("arbitrary",)),
        name="dispatch",
    )(*route, fill, eidx, rank, base, h2)


def _expert_kernel(blk_ref, src_ref, valid_ref, xs_ref, w1g_ref, w1l_ref, b1g_ref, b1l_ref,
                   w2_ref, b2_ref, o_ref):
    j = pl.program_id(0)

    @pl.when(valid_ref[j] != 0)
    def _():
        lo, hi = _unpack_halves(xs_ref[...])
        half = lo.shape[1]

        def up(w_ref, b_ref):
            return _dot(lo, w_ref[:half, :]) + _dot(hi, w_ref[half:, :]) + b_ref[...]

        glu = jnp.minimum(up(w1g_ref, b1g_ref), SWIGLU_LIMIT)
        lin = jnp.clip(up(w1l_ref, b1l_ref), -SWIGLU_LIMIT, SWIGLU_LIMIT)
        act = glu * _sigmoid(SWIGLU_ALPHA * glu) * (lin + 1.0)
        o = _dot(act.astype(BF16), w2_ref[...]) + b2_ref[...]
        o_ref[...] = _pack_halves(o.astype(BF16).astype(F32))

    @pl.when(valid_ref[j] == 0)
    def _():
        o_ref[...] = jnp.zeros_like(o_ref)


def _experts(blk_e, src, valid, xs, w1g, w1l, b1g, b1l, w2, b2):
    rows, half = xs.shape
    nblk = rows // MOE_ROWS
    d, de = w1g.shape[1], w1g.shape[2]
    emap = lambda j, be, src, valid: (be[j], 0, 0)
    return pl.pallas_call(
        _expert_kernel,
        grid_spec=pltpu.PrefetchScalarGridSpec(
            num_scalar_prefetch=3,
            grid=(nblk,),
            in_specs=[pl.BlockSpec((MOE_ROWS, half), lambda j, be, src, valid: (src[j], 0)),
                      pl.BlockSpec((None, d, de), emap), pl.BlockSpec((None, d, de), emap),
                      pl.BlockSpec((None, 1, de), emap), pl.BlockSpec((None, 1, de), emap),
                      pl.BlockSpec((None, de, d), emap), pl.BlockSpec((None, 1, d), emap)],
            out_specs=pl.BlockSpec((MOE_ROWS, half), lambda j, be, src, valid: (j, 0))),
        out_shape=jax.ShapeDtypeStruct((rows, half), jnp.uint32),
        compiler_params=_cparams(("arbitrary",)),
        name="experts",
    )(blk_e, src, valid, xs, w1g, w1l, b1g, b1l, w2, b2)


def _split_w1_kernel(w_ref, perm_ref, g_ref, l_ref):
    t = MXU_TILE
    for cb in range(w_ref.shape[1] // t):
        res = _dot(w_ref[:, cb * t:(cb + 1) * t].astype(BF16), perm_ref[...])
        g_ref[:, cb * (t // 2):(cb + 1) * (t // 2)] = res[:, :t // 2].astype(BF16)
        l_ref[:, cb * (t // 2):(cb + 1) * (t // 2)] = res[:, t // 2:].astype(BF16)


def _split_w1(w1):
    ne, d, two_de = w1.shape
    t = MXU_TILE
    src = jnp.arange(t)[:, None]
    dst = jnp.arange(t)[None, :]
    perm = (src == jnp.where(dst < t // 2, 2 * dst, 2 * (dst - t // 2) + 1)).astype(BF16)
    return pl.pallas_call(
        _split_w1_kernel,
        grid=(ne,),
        in_specs=[pl.BlockSpec((None, d, two_de), lambda e: (e, 0, 0)),
                  pl.BlockSpec((t, t), lambda e: (0, 0))],
        out_specs=[pl.BlockSpec((None, d, two_de // 2), lambda e: (e, 0, 0)),
                   pl.BlockSpec((None, d, two_de // 2), lambda e: (e, 0, 0))],
        out_shape=[jax.ShapeDtypeStruct((ne, d, two_de // 2), BF16),
                   jax.ShapeDtypeStruct((ne, d, two_de // 2), BF16)],
        compiler_params=_cparams(("parallel",)),
        name="split_w1",
    )(w1, perm)


def _combine_kernel(sbase_ref, scnt_ref, pstart_ref, nx_ref, xlist_ref,
                    eidx_ref, rank_ref, gates_ref, base_ref, x1_ref, gate2_ref, o_hbm, out_ref,
                    slab_ref, xslab_ref, extra_ref, acc_ref, rows_ref, ge_ref, sem, xsem):
    s = pl.program_id(0)
    last = pl.num_programs(0) - 1
    half = acc_ref.shape[1] // 2
    slot = s % 2

    def window_start(sub, e, chunk=0):
        first = pstart_ref[e] + sbase_ref[sub * N_EXPERTS + e]
        return pl.multiple_of(first - (first & (ROW_ALIGN - 1)) + chunk * SLAB, ROW_ALIGN)

    def slab_copy(sub, e, to):
        return pltpu.make_async_copy(o_hbm.at[pl.ds(window_start(sub, e), SLAB)],
                                     slab_ref.at[to, e], sem.at[to])

    def extra(sub, i):
        cand = xlist_ref[sub * EXTRA_MAX + i]
        return cand // EXTRA_PER, cand % EXTRA_PER + 1

    def xslab_copy(sub, i, to):
        e, c = extra(sub, i)
        return pltpu.make_async_copy(o_hbm.at[pl.ds(window_start(sub, e, c), SLAB)],
                                     xslab_ref.at[to, i], sem.at[to])

    def fetch(sub, to):
        for e in range(N_EXPERTS):
            slab_copy(sub, e, to).start()
        for i in range(EXTRA_BATCH):
            @pl.when(i < nx_ref[sub])
            def _():
                xslab_copy(sub, i, to).start()

    @pl.when(s == 0)
    def _():
        xslab_ref[...] = jnp.zeros_like(xslab_ref)
        fetch(0, 0)

    @pl.when(s < last)
    def _():
        fetch(s + 1, 1 - slot)

    rows, _ = _slab_rows(eidx_ref, rank_ref, base_ref, s)
    eid = lax.broadcasted_iota(jnp.int32, rows.shape, 0)
    ge = jnp.zeros(rows.shape, F32)
    for k in range(TOP_K):
        ge = jnp.where(eidx_ref[k:k + 1, :] == eid, gates_ref[k:k + 1, :], ge)
    rid = lax.broadcasted_iota(jnp.int32, (SLAB, 1), 0)

    def weights(row, gate, chunk):
        return jnp.where(row - chunk * SLAB == rid, gate, 0.0).astype(BF16)

    rows_ref[...] = rows
    ge_ref[...] = ge
    for e in range(N_EXPERTS):
        slab_copy(s, e, slot).wait()
    for i in range(EXTRA_BATCH):
        @pl.when(i < nx_ref[s])
        def _():
            xslab_copy(s, i, slot).wait()
    acc_lo = acc_hi = None
    for g0 in range(0, N_EXPERTS, EXPERT_GROUP):
        wt = jnp.concatenate([weights(rows[e:e + 1, :], ge[e:e + 1, :], 0)
                              for e in range(g0, g0 + EXPERT_GROUP)], axis=0)
        lo, hi = _unpack_halves(
            slab_ref[slot, g0:g0 + EXPERT_GROUP].reshape(EXPERT_GROUP * SLAB, half))
        p_lo, p_hi = _dot_tn(wt, lo), _dot_tn(wt, hi)
        acc_lo = p_lo if acc_lo is None else acc_lo + p_lo
        acc_hi = p_hi if acc_hi is None else acc_hi + p_hi
    wts = []
    for i in range(EXTRA_BATCH):
        e, c = extra(s, i)
        gate = jnp.where(i < nx_ref[s], ge_ref[pl.ds(e, 1), :], 0.0)
        wts.append(weights(rows_ref[pl.ds(e, 1), :], gate, c))
    lo, hi = _unpack_halves(xslab_ref[slot].reshape(EXTRA_BATCH * SLAB, half))
    wt = jnp.concatenate(wts, axis=0)
    acc_ref[:, :half] = acc_lo + _dot_tn(wt, lo)
    acc_ref[:, half:] = acc_hi + _dot_tn(wt, hi)

    def more(i, carry):
        e, c = extra(s, i)
        cp = pltpu.make_async_copy(o_hbm.at[pl.ds(window_start(s, e, c), SLAB)], extra_ref, xsem)
        cp.start()
        cp.wait()
        lo, hi = _unpack_halves(extra_ref[...])
        wt = weights(rows_ref[pl.ds(e, 1), :], ge_ref[pl.ds(e, 1), :], c)
        acc_ref[:, :half] += _dot_tn(wt, lo)
        acc_ref[:, half:] += _dot_tn(wt, hi)
        return carry

    lax.fori_loop(EXTRA_BATCH, nx_ref[s], more, 0)
    out_ref[...] = x1_ref[...] + gate2_ref[0] * acc_ref[...]


def _combine(route, eidx, rank, gates, base, x1, gate2, o_rows, seq):
    n, d = x1.shape
    tt = ROUTE_TILE
    subs = TOKEN_TILE // tt
    per = seq // tt
    tcol = pl.BlockSpec((TOP_K, tt), lambda s, *_: (0, s))
    return pl.pallas_call(
        _combine_kernel,
        grid_spec=pltpu.PrefetchScalarGridSpec(
            num_scalar_prefetch=5,
            grid=(n // tt,),
            in_specs=[tcol, tcol, tcol,
                      pl.BlockSpec((None, N_EXPERTS, LANES), lambda s, *_: (s // subs, 0, 0)),
                      pl.BlockSpec((tt, d), lambda s, *_: (s, 0)),
                      pl.BlockSpec((1, 1, d), lambda s, *_: (s // per, 0, 0)),
                      pl.BlockSpec(memory_space=pl.ANY)],
            out_specs=pl.BlockSpec((tt, d), lambda s, *_: (s, 0)),
            scratch_shapes=[pltpu.VMEM((2, N_EXPERTS, SLAB, d // 2), jnp.uint32),
                            pltpu.VMEM((2, EXTRA_BATCH, SLAB, d // 2), jnp.uint32),
                            pltpu.VMEM((SLAB, d // 2), jnp.uint32),
                            pltpu.VMEM((tt, d), F32),
                            pltpu.VMEM((N_EXPERTS, tt), jnp.int32),
                            pltpu.VMEM((N_EXPERTS, tt), F32),
                            pltpu.SemaphoreType.DMA((2,)), pltpu.SemaphoreType.DMA(())]),
        out_shape=jax.ShapeDtypeStruct((n, d), F32),
        compiler_params=_cparams(("arbitrary",)),
        name="combine",
    )(*route, eidx, rank, gates, base, x1, gate2, o_rows)


def _layer(x, c, ada_w, ada_b, norm1_g, norm2_g, w_in, mu_rkv, mu_wag, w0, w1, w2, a0, a1, a2,
           g1, g2, k_k, k_a, r_k, ln_w, ln_b, qn_g, kn_g, rel_bias, w_br_rwkv, w_br_attn, w_out,
           router_w, router_b, exp_w1, exp_b1, exp_w2, exp_b2):
    bsz, seq, d = x.shape
    n = bsz * seq
    rw = RWKV_WIDTH
    assert d == D_MODEL and seq % (ATTN_BLOCK * ATTN_GROUPS[-1][1]) == 0 and seq % TOKEN_TILE == 0

    mod = _modulation(c, ada_w, ada_b)
    shift1, scale1, gate1, shift2, scale2, gate2 = [
        m.reshape(bsz, 1, d) for m in jnp.split(mod, 6, axis=-1)]

    lora_w = jnp.concatenate([w1, a1, g1], axis=1)
    lora_mu = jnp.concatenate([jnp.broadcast_to(mu_wag[0][:, None], w1.shape),
                               jnp.broadcast_to(mu_wag[1][:, None], a1.shape),
                               jnp.broadcast_to(mu_wag[2][:, None], g1.shape)], axis=1)
    w_r = jnp.concatenate([w_in[:, :3 * rw], lora_w * (1.0 - lora_mu), lora_w * lora_mu],
                          axis=1).astype(BF16)
    qa0 = 3 * rw
    acols = []
    for gi in range(len(ATTN_GROUPS)):
        for part in range(3):
            lo = qa0 + part * ATTN_WIDTH + gi * GROUP_WIDTH
            acols.append(w_in[:, lo:lo + GROUP_WIDTH])
    w_a = jnp.concatenate(acols, axis=1).astype(BF16)
    w_g = w_in[:, qa0 + 3 * ATTN_WIDTH:].astype(BF16)

    x2 = x.reshape(n, d)
    pr, pg, qkv = _inproj(x2, norm1_g.reshape(1, d), scale1, shift1, w_r, w_a, w_g, bsz, seq)

    par = jnp.stack([w0, a0, k_k, k_a, r_k.reshape(rw), ln_w, ln_b,
                     mu_rkv[0], mu_rkv[1], mu_rkv[2]]
                    + [jnp.zeros((rw,), F32)] * 6)
    w_lora = jnp.concatenate([jnp.pad(w2, ((0, 0), (0, 2 * rw))),
                              jnp.pad(a2, ((0, 0), (rw, rw))),
                              jnp.pad(g2, ((0, 0), (2 * rw, 0)))], axis=0).astype(BF16)
    hid = jnp.arange(rw) // HEAD_DIM
    ones_bd = (hid[:, None] == hid[None, :]).astype(BF16)
    y_rwkv = _rwkv(pr, par, w_lora, ones_bd, bsz, seq)

    gains = jnp.pad(jnp.stack([jnp.tile(qn_g, HEADS_PER_GROUP), jnp.tile(kn_g, HEADS_PER_GROUP)]),
                    ((0, 6), (0, 0)))
    bias = jnp.stack([_band_bias(rel_bias, gi) for gi in range(len(ATTN_GROUPS))])
    ghead = jnp.arange(GROUP_WIDTH) // HEAD_DIM
    mean_bd = ((ghead[:, None] == ghead[None, :]).astype(F32) / HEAD_DIM).astype(BF16)
    att = _attention(qkv, bias, gains, mean_bd, bsz, seq)

    rw_t = router_w.T
    rw_hi = rw_t.astype(BF16)
    rw_lo = (rw_t - rw_hi.astype(F32)).astype(BF16)
    rb = jnp.broadcast_to(router_b[:, None], (N_EXPERTS, 128))
    x1, h2, gates, eidx, rank, counts, base = _post(
        x2, y_rwkv, att, pg, gate1, scale2, shift2, norm2_g.reshape(1, d),
        w_br_rwkv.astype(BF16), w_br_attn.astype(BF16), w_out.astype(BF16), rw_hi, rw_lo, rb, seq)

    i32 = jnp.int32
    cnt = counts[:, 0]
    real = (cnt + MOE_ROWS - 1) // MOE_ROWS
    reserved = (cnt + SLAB + ROW_ALIGN + MOE_ROWS - 1) // MOE_ROWS
    bend = jnp.cumsum(reserved)
    bstart = bend - reserved
    nblk = (n * TOP_K) // MOE_ROWS + 2 * N_EXPERTS
    j = jnp.arange(nblk, dtype=i32)
    owner = jnp.clip(jnp.sum((bend[None, :] <= j[:, None]).astype(i32), axis=1), 0, N_EXPERTS - 1)
    local = j - jnp.take(bstart, owner)
    valid = local < jnp.take(real, owner)
    src = lax.cummax(jnp.where(valid, j, 0))
    blk_e = jnp.take(owner, src)
    fill = jnp.logical_or(~valid, local == jnp.take(real, owner) - 1)
    pstart = bstart * MOE_ROWS
    subs = TOKEN_TILE // ROUTE_TILE
    sbase = jnp.transpose(base[:, :, :subs], (0, 2, 1)).reshape(n // ROUTE_TILE, N_EXPERTS)
    scnt = jnp.concatenate([sbase[1:], cnt[None, :]], axis=0) - sbase
    chunks = ((sbase & (ROW_ALIGN - 1)) + scnt + SLAB - 1) // SLAB
    wanted = (jnp.arange(1, EXTRA_PER + 1, dtype=i32)[None, None, :] < chunks[:, :, None])
    wanted = wanted.reshape(n // ROUTE_TILE, N_EXPERTS * EXTRA_PER)
    xlist = jnp.argsort(jnp.logical_not(wanted), axis=1, stable=True)[:, :EXTRA_MAX]
    nx = jnp.sum(wanted.astype(i32), axis=1)
    route = (sbase.reshape(-1).astype(i32), scnt.reshape(-1).astype(i32), pstart.astype(i32),
             nx.astype(i32), xlist.reshape(-1).astype(i32))

    xs = _dispatch(route, fill.astype(i32), eidx, rank, base, h2, nblk)
    w1g, w1l = _split_w1(exp_w1)
    o_rows = _experts(blk_e.astype(i32), src.astype(i32), valid.astype(i32), xs, w1g, w1l,
                      exp_b1[:, None, 0::2], exp_b1[:, None, 1::2],
                      exp_w2.astype(BF16), exp_b2[:, None, :])
    out = _combine(route, eidx, rank, gates, base, x1, gate2, o_rows, seq)
    return out.reshape(bsz, seq, d)


def kernel(x, c, ada_w, ada_b, norm1_g, norm2_g, w_in, rwkv_mu_rkv, rwkv_mu_wag, rwkv_w0, rwkv_w1, rwkv_w2, rwkv_a0, rwkv_a1, rwkv_a2, rwkv_g1, rwkv_g2, rwkv_k_k, rwkv_k_a, rwkv_r_k, rwkv_ln_w, rwkv_ln_b, attn_qn_g, attn_kn_g, rel_bias, w_br_rwkv, w_br_attn, w_out, router_w, router_b, exp_w1, exp_b1, exp_w2, exp_b2):
    per_layer = (ada_w, ada_b, norm1_g, norm2_g, w_in, rwkv_mu_rkv, rwkv_mu_wag, rwkv_w0, rwkv_w1,
                 rwkv_w2, rwkv_a0, rwkv_a1, rwkv_a2, rwkv_g1, rwkv_g2, rwkv_k_k, rwkv_k_a,
                 rwkv_r_k, rwkv_ln_w, rwkv_ln_b, attn_qn_g, attn_kn_g)
    tail = (w_br_rwkv, w_br_attn, w_out, router_w, router_b, exp_w1, exp_b1, exp_w2, exp_b2)
    for layer in range(ada_w.shape[0]):
        head = [p[layer] for p in per_layer]
        rest = [p[layer] for p in tail]
        x = _layer(x, c, *head, rel_bias, *rest)
    return x
```

```python
import math

import jax
import jax.numpy as jnp
from jax import lax
from jax.experimental import pallas as pl
from jax.experimental.pallas import tpu as pltpu

F32 = jnp.float32
BF16 = jnp.bfloat16

D_MODEL = 1024
HEAD_DIM = 64
RWKV_HEADS = 8
RWKV_WIDTH = RWKV_HEADS * HEAD_DIM
DECAY_LORA = 64
ICLR_LORA = 64
GATE_LORA = 128
LORA_WIDTH = DECAY_LORA + ICLR_LORA + GATE_LORA
GROUPNORM_EPS = 64e-5
ATTN_GROUPS = ((128, 1), (512, 4), (2048, 16))
HEADS_PER_GROUP = 4
ATTN_HEADS = HEADS_PER_GROUP * len(ATTN_GROUPS)
ATTN_WIDTH = ATTN_HEADS * HEAD_DIM
GROUP_WIDTH = HEADS_PER_GROUP * HEAD_DIM
ATTN_BLOCK = 128
N_BUCKETS = 32
MAX_DISTANCE = 2048
N_EXPERTS = 32
TOP_K = 4
SWIGLU_ALPHA = 1.702
SWIGLU_LIMIT = 7.0
NORM_EPS = 1e-6
NEG_INF = -1e30

RWKV_COLS = 3 * RWKV_WIDTH + 2 * LORA_WIDTH
GATE_COLS = 2 * D_MODEL

CHUNK = 64
RWKV_BATCH = 4
ATTN_UNROLL = 4
TOKEN_TILE = 512
MOE_ROWS = 512
ROUTE_TILE = 256
SLAB = 64
EXTRA_BATCH = 8
ROW_ALIGN = 8
EXTRA_PER = (ROW_ALIGN - 1 + ROUTE_TILE + SLAB - 1) // SLAB - 1
EXTRA_MAX = (N_EXPERTS * (ROW_ALIGN - 1) + TOP_K * ROUTE_TILE) // SLAB + 1
EXPERT_GROUP = 8
MXU_TILE = 256
LANES = 128
VMEM_LIMIT = 56 * 1024 * 1024


def _cparams(sem, vmem=VMEM_LIMIT):
    return pltpu.CompilerParams(dimension_semantics=sem, vmem_limit_bytes=vmem)


def _split2(a):
    hi = a.astype(BF16)
    lo = (a - hi.astype(F32)).astype(BF16)
    return hi, lo


def _split3(a):
    hi = a.astype(BF16)
    r = a - hi.astype(F32)
    mid = r.astype(BF16)
    lo = (r - mid.astype(F32)).astype(BF16)
    return hi, mid, lo


def _dot(a, b):
    return jnp.dot(a, b, preferred_element_type=F32)


def _dot_nt(a, b):
    return lax.dot_general(a, b, (((1,), (1,)), ((), ())), preferred_element_type=F32)


def _dot_tn(a, b):
    return lax.dot_general(a, b, (((0,), (0,)), ((), ())), preferred_element_type=F32)


def _sigmoid(x):
    return 0.5 * jnp.tanh(0.5 * x) + 0.5


def _mod_kernel(c_ref, w_ref, b_ref, o_ref):
    c = c_ref[...]
    s = c * _sigmoid(c)
    s_hi, s_lo = _split2(s)
    w_hi, w_lo = _split2(w_ref[...])
    o_ref[...] = _dot(s_hi, w_hi) + _dot(s_hi, w_lo) + _dot(s_lo, w_hi) + b_ref[...]


def _modulation(c, ada_w, ada_b):
    bsz, d = c.shape
    cols = ada_w.shape[1]
    tn = 1024
    return pl.pallas_call(
        _mod_kernel,
        grid=(cols // tn,),
        in_specs=[pl.BlockSpec((bsz, d), lambda j: (0, 0)),
                  pl.BlockSpec((d, tn), lambda j: (0, j)),
                  pl.BlockSpec((1, tn), lambda j: (0, j))],
        out_specs=pl.BlockSpec((bsz, tn), lambda j: (0, j)),
        out_shape=jax.ShapeDtypeStruct((bsz, cols), F32),
        compiler_params=_cparams(("arbitrary",)),
        name="mod",
    )(c, ada_w, ada_b.reshape(1, cols))


def _inproj_kernel(x_ref, g_ref, scale_ref, shift_ref, wr_ref, wa_ref, wg_ref,
                   pr_ref, pg_ref, a0_ref, a1_ref, a2_ref, h_ref):
    x = x_ref[...]
    tm = x.shape[0]
    ms = jnp.mean(x * x, axis=-1, keepdims=True)
    h = x * lax.rsqrt(ms + NORM_EPS) * (g_ref[...] * (1.0 + scale_ref[0])) + shift_ref[0]
    ntile = h_ref.shape[0]
    for c in range(ntile):
        h_ref[c] = h[:, c * LANES:(c + 1) * LANES]
    hb = h.astype(BF16)
    step = 512
    gcols = 3 * GROUP_WIDTH
    for w_ref, o_ref, cols in ((wr_ref, pr_ref, RWKV_COLS), (wg_ref, pg_ref, GATE_COLS),
                               (wa_ref, a0_ref, gcols)):
        for j in range(0, cols, step):
            width = min(step, cols - j)
            o_ref[:, j:j + width] = _dot(hb, w_ref[:, j:j + width]).astype(BF16)
    for gi, a_ref in ((1, a1_ref), (2, a2_ref)):
        dil = ATTN_GROUPS[gi][1]
        run = tm // dil
        hp = jnp.concatenate(
            [jnp.concatenate([h_ref[c, pl.ds(z, run, stride=dil), :] for z in range(dil)], axis=0)
             for c in range(ntile)], axis=1).astype(BF16)
        res = _dot(hp, wa_ref[:, gi * gcols:(gi + 1) * gcols]).astype(BF16)
        for z in range(dil):
            a_ref[z] = res[z * run:(z + 1) * run]


def _inproj(x2, norm_g, scale, shift, w_r, w_a, w_g, bsz, seq):
    n, d = x2.shape
    tm = TOKEN_TILE
    per = seq // tm
    gcols = 3 * GROUP_WIDTH
    const = lambda shape: pl.BlockSpec(shape, lambda i: (0, 0), pipeline_mode=pl.Buffered(1))
    row = lambda width: pl.BlockSpec((tm, width), lambda i: (i, 0))
    dil1, dil2 = ATTN_GROUPS[1][1], ATTN_GROUPS[2][1]
    res_spec = lambda dil: pl.BlockSpec((None, dil, None, tm // dil, gcols),
                                        lambda i: (i // per, 0, i % per, 0, 0))
    res_shape = lambda dil: jax.ShapeDtypeStruct((bsz, dil, per, tm // dil, gcols), BF16)
    pr, pg, a0, a1, a2 = pl.pallas_call(
        _inproj_kernel,
        grid=(n // tm,),
        in_specs=[row(d),
                  pl.BlockSpec((1, d), lambda i: (0, 0)),
                  pl.BlockSpec((1, 1, d), lambda i: (i // per, 0, 0)),
                  pl.BlockSpec((1, 1, d), lambda i: (i // per, 0, 0)),
                  const(w_r.shape), const(w_a.shape), const(w_g.shape)],
        out_specs=[row(RWKV_COLS), row(GATE_COLS), row(gcols), res_spec(dil1), res_spec(dil2)],
        out_shape=[jax.ShapeDtypeStruct((n, RWKV_COLS), BF16),
                   jax.ShapeDtypeStruct((n, GATE_COLS), BF16),
                   jax.ShapeDtypeStruct((n, gcols), BF16),
                   res_shape(dil1), res_shape(dil2)],
        scratch_shapes=[pltpu.VMEM((d // LANES, tm, LANES), F32)],
        compiler_params=_cparams(("parallel",)),
        name="inproj",
    )(x2, norm_g, scale, shift, w_r, w_a, w_g)
    qkv = [a.reshape(bsz, seq, gcols) for a in (a0, a1, a2)]
    return pr, pg, qkv


_P_W0, _P_A0, _P_KK, _P_KA, _P_RK, _P_LNW, _P_LNB, _P_MUR, _P_MUK, _P_MUV = range(10)


def _rwkv_kernel(pr_ref, par_ref, wl_ref, ones_ref, o_ref, state_ref, prev_ref, y_ref):
    c_idx = pl.program_id(1)
    nb = pr_ref.shape[0]
    rows_all = nb * CHUNK

    @pl.when(c_idx == 0)
    def _():
        state_ref[...] = jnp.zeros_like(state_ref)
        prev_ref[...] = jnp.zeros_like(prev_ref)

    rw = RWKV_WIDTH
    par = lambda i: par_ref[i:i + 1, :]
    ones_bd = ones_ref[...]

    def headsum(a):
        hi, lo = _split2(a)
        return _dot(hi, ones_bd) + _dot(lo, ones_bd)

    pr = jnp.concatenate([pr_ref[b].astype(F32) for b in range(nb)], axis=0)
    row = lax.broadcasted_iota(jnp.int32, (rows_all, 1), 0)
    first = row % CHUNK == 0

    def shifted(lo, hi):
        prev_row = prev_ref[0:1, lo:hi]
        for b in range(1, nb):
            prev_row = jnp.where(row >= b * CHUNK, prev_ref[b:b + 1, lo:hi], prev_row)
        return jnp.where(first, prev_row, pltpu.roll(pr[:, lo:hi], 1, 0))

    sh_rkv = shifted(0, 3 * rw)
    sh_lora = shifted(3 * rw + LORA_WIDTH, 3 * rw + 2 * LORA_WIDTH)
    for b in range(nb):
        prev_ref[b:b + 1, :] = pr[(b + 1) * CHUNK - 1:(b + 1) * CHUNK, :]

    lora = pr[:, 3 * rw:3 * rw + LORA_WIDTH] + sh_lora
    lane = lax.broadcasted_iota(jnp.int32, (1, LORA_WIDTH), 1)
    act = jnp.where(lane < DECAY_LORA, jnp.tanh(lora),
                    jnp.where(lane < DECAY_LORA + ICLR_LORA, lora, _sigmoid(lora)))
    lo_out = _dot(act.astype(BF16), wl_ref[...])
    w_pre = par(_P_W0) + lo_out[:, :rw]
    z = -w_pre
    softplus = jnp.maximum(z, 0.0) + jnp.log(1.0 + jnp.exp(-jnp.abs(z)))
    logd = -jnp.exp(-softplus - 0.5)
    a = _sigmoid(par(_P_A0) + lo_out[:, rw:2 * rw])
    g = lo_out[:, 2 * rw:3 * rw]

    r0, k0, v0 = pr[:, :rw], pr[:, rw:2 * rw], pr[:, 2 * rw:3 * rw]
    r = r0 + (sh_rkv[:, :rw] - r0) * par(_P_MUR)
    k = k0 + (sh_rkv[:, rw:2 * rw] - k0) * par(_P_MUK)
    v = v0 + (sh_rkv[:, 2 * rw:] - v0) * par(_P_MUV)

    kk = k * par(_P_KK)
    kk = kk / jnp.maximum(jnp.sqrt(headsum(kk * kk)), 1e-12)
    k2 = k * (1.0 + (a - 1.0) * par(_P_KA))

    ri = lax.broadcasted_iota(jnp.int32, (rows_all, rows_all), 0)
    rj = lax.broadcasted_iota(jnp.int32, (rows_all, rows_all), 1)
    tril = jnp.where(jnp.logical_and(ri >= rj, ri // CHUNK == rj // CHUNK), 1.0, 0.0).astype(BF16)
    ti = lax.broadcasted_iota(jnp.int32, (CHUNK, CHUNK), 0)
    tj = lax.broadcasted_iota(jnp.int32, (CHUNK, CHUNK), 1)
    low_incl = ti >= tj
    low_strict = ti > tj
    d_hi, d_mid, d_lo = _split3(logd)
    cum = _dot(tril, d_hi) + _dot(tril, d_mid) + _dot(tril, d_lo)
    p_incl = jnp.exp(cum)
    p_excl = jnp.exp(cum - logd)
    p_inv = jnp.exp(-cum)

    a_t = (-kk * p_excl).astype(BF16)
    b_t = (kk * a * p_inv).astype(BF16)
    k_t = (k2 * p_inv).astype(BF16)
    r_t = (r * p_incl).astype(BF16)
    v_b = v.astype(BF16)
    eye = jnp.where(ti == tj, 1.0, 0.0)

    chains = [(b, h) for b in range(nb) for h in range(RWKV_HEADS)]
    ids = range(len(chains))
    rs = [slice(b * CHUNK, (b + 1) * CHUNK) for b, _ in chains]
    ls = [slice(h * HEAD_DIM, (h + 1) * HEAD_DIM) for _, h in chains]
    ar = [jnp.concatenate([a_t[rs[i], ls[i]], r_t[rs[i], ls[i]]], axis=0) for i in ids]
    bk = [jnp.concatenate([b_t[rs[i], ls[i]], k_t[rs[i], ls[i]]], axis=0) for i in ids]
    vh = [v_b[rs[i], ls[i]] for i in ids]
    s0 = [state_ref[b, h] for b, h in chains]
    m = [_dot_nt(ar[i], bk[i]) for i in ids]
    ars = [_dot_nt(ar[i], s0[i].astype(BF16)) for i in ids]
    a_ab = [jnp.where(low_strict, m[i][:CHUNK, :CHUNK], 0.0) for i in ids]
    a_ak = [jnp.where(low_strict, m[i][:CHUNK, CHUNK:], 0.0).astype(BF16) for i in ids]
    low2 = (lax.broadcasted_iota(jnp.int32, (CHUNK, 2 * CHUNK), 0)
            >= lax.broadcasted_iota(jnp.int32, (CHUNK, 2 * CHUNK), 1) % CHUNK)
    a_r = [jnp.where(low2, m[i][CHUNK:, :], 0.0).astype(BF16) for i in ids]
    t_inv = [eye + a_ab[i] for i in ids]
    pw = [a_ab[i].astype(BF16) for i in ids]
    for _ in range(int(math.log2(CHUNK)) - 1):
        pw = [_dot(pw[i], pw[i]).astype(BF16) for i in ids]
        t_inv = [t_inv[i] + _dot(pw[i], t_inv[i].astype(BF16)) for i in ids]
    rhs = [ars[i][:CHUNK] + _dot(a_ak[i], vh[i]) for i in ids]
    ub = [_dot(t_inv[i].astype(BF16), rhs[i].astype(BF16)).astype(BF16) for i in ids]
    uv = [jnp.concatenate([ub[i], vh[i]], axis=0) for i in ids]
    for i in ids:
        y_ref[rs[i], ls[i]] = ars[i][CHUNK:] + _dot(a_r[i], uv[i])
    for i, (b, h) in enumerate(chains):
        last = (b + 1) * CHUNK - 1
        state_ref[b, h] = (s0[i] + _dot_tn(uv[i], bk[i])) * p_incl[last:last + 1, ls[i]]

    y = y_ref[...]
    inv_n = 1.0 / HEAD_DIM
    mu = headsum(y) * inv_n
    dlt = y - mu
    var = headsum(dlt * dlt) * inv_n
    yn = dlt * lax.rsqrt(var + GROUPNORM_EPS) * par(_P_LNW) + par(_P_LNB)
    bonus = headsum(r * k2 * par(_P_RK)) * v
    out = ((yn + bonus) * g).astype(BF16)
    for b in range(nb):
        o_ref[b] = out[b * CHUNK:(b + 1) * CHUNK]


def _rwkv(pr, par, w_lora, ones_bd, bsz, seq):
    n = pr.shape[0]
    nchunk = seq // CHUNK
    nb = RWKV_BATCH if bsz % RWKV_BATCH == 0 else 1
    const = lambda shape: pl.BlockSpec(shape, lambda b, c: (0, 0))
    out = pl.pallas_call(
        _rwkv_kernel,
        grid=(bsz // nb, nchunk),
        in_specs=[pl.BlockSpec((nb, CHUNK, RWKV_COLS), lambda b, c: (b, c, 0)),
                  const(par.shape), const(w_lora.shape), const(ones_bd.shape)],
        out_specs=pl.BlockSpec((nb, CHUNK, RWKV_WIDTH), lambda b, c: (b, c, 0)),
        out_shape=jax.ShapeDtypeStruct((bsz, seq, RWKV_WIDTH), BF16),
        scratch_shapes=[pltpu.VMEM((nb, RWKV_HEADS, HEAD_DIM, HEAD_DIM), F32),
                        pltpu.VMEM((8, RWKV_COLS), F32),
                        pltpu.VMEM((nb * CHUNK, RWKV_WIDTH), F32)],
        compiler_params=_cparams(("parallel", "arbitrary")),
        name="rwkv",
    )(pr.reshape(bsz, seq, RWKV_COLS), par, w_lora, ones_bd)
    return out.reshape(n, RWKV_WIDTH)


def _attn_kernel(a0_ref, a1_ref, a2_ref, bias_ref, gain_ref, bd_ref, out_ref,
                 qkv_ref, m_ref, s_ref, o_ref):
    seq = a0_ref.shape[0]
    ntile = m_ref.shape[0]
    gw = GROUP_WIDTH
    blk = ATTN_BLOCK
    bd = bd_ref[...]
    gq = gain_ref[0:1, :] * (HEAD_DIM ** -0.5)
    gk = gain_ref[1:2, :]
    lane_head = lax.broadcasted_iota(jnp.int32, (1, gw), 1) // HEAD_DIM
    qkv_ref[0:blk, :] = jnp.zeros((blk, 3 * gw), BF16)

    def headwise(parts):
        acc = parts[0]
        for h in range(1, HEADS_PER_GROUP):
            acc = jnp.where(lane_head == h, parts[h], acc)
        return acc

    for gi, a_ref in enumerate((a0_ref, a1_ref, a2_ref)):
        dil = ATTN_GROUPS[gi][1]
        nb = seq // dil // blk

        def normalise(c, carry, a_ref=a_ref):
            r0 = pl.multiple_of(c * 256, 256)
            for part, gain in ((0, gq), (1, gk)):
                t = a_ref[pl.ds(r0, 256), part * gw:(part + 1) * gw].astype(F32)
                hi, lo = _split2(t * t)
                ms = _dot(hi, bd) + _dot(lo, bd)
                qkv_ref[pl.ds(blk + r0, 256), part * gw:(part + 1) * gw] = (
                    t * lax.rsqrt(ms + NORM_EPS) * gain).astype(BF16)
            qkv_ref[pl.ds(blk + r0, 256), 2 * gw:] = a_ref[pl.ds(r0, 256), 2 * gw:]
            return carry

        lax.fori_loop(0, seq // 256, normalise, 0)

        def blocks(it, carry, gi=gi, dil=dil, nb=nb):
            ts = [it * ATTN_UNROLL + u for u in range(ATTN_UNROLL)]
            r0 = [pl.multiple_of(t * blk, blk) for t in ts]
            first = [jnp.where(t % nb == 0, 1, 0) for t in ts]
            q = [qkv_ref[pl.ds(blk + r, blk), 0:gw] for r in r0]
            kwin = [qkv_ref[pl.ds(r, 2 * blk), gw:2 * gw] for r in r0]
            vwin = [qkv_ref[pl.ds(r, 2 * blk), 2 * gw:] for r in r0]
            pairs = [(u, h) for u in range(ATTN_UNROLL) for h in range(HEADS_PER_GROUP)]
            qm = [jnp.where(lane_head == h, q[u], jnp.zeros_like(q[u])) for u, h in pairs]
            s = [_dot_nt(qm[i], kwin[u]) + bias_ref[gi, first[u], h]
                 for i, (u, h) in enumerate(pairs)]
            m = [jnp.max(si, axis=-1, keepdims=True) for si in s]
            p = [jnp.exp(si - mi) for si, mi in zip(s, m)]
            den = [jnp.sum(pi, axis=-1, keepdims=True) for pi in p]
            pv = [_dot(p[i].astype(BF16), vwin[u]) for i, (u, h) in enumerate(pairs)]
            for u, t in enumerate(ts):
                sel = slice(u * HEADS_PER_GROUP, (u + 1) * HEADS_PER_GROUP)
                o_new = headwise(pv[sel]) / headwise(
                    [jnp.broadcast_to(d, (blk, gw)) for d in den[sel]])
                lse = headwise([jnp.broadcast_to(mi + jnp.log(d), (blk, gw))
                                for mi, d in zip(m[sel], den[sel])])
                z = t // nb
                n = t % nb
                rows = pl.ds(z + n * (blk * dil), blk, stride=dil) if dil > 1 else pl.ds(r0[u], blk)
                ld = lambda ref: jnp.concatenate([ref[c, rows, :] for c in range(ntile)], axis=1)

                def st(ref, val):
                    for c in range(ntile):
                        ref[c, rows, :] = val[:, c * LANES:(c + 1) * LANES]

                if gi == 0:
                    st(m_ref, lse)
                    st(o_ref, o_new)
                else:
                    m_old = ld(m_ref)
                    m_new = jnp.maximum(m_old, lse)
                    wa = jnp.exp(m_old - m_new)
                    wb = jnp.exp(lse - m_new)
                    mix = wa * ld(o_ref) + wb * o_new
                    if gi == 1:
                        st(m_ref, m_new)
                        st(s_ref, wa + wb)
                        st(o_ref, mix)
                    else:
                        st(o_ref, mix / (wa * ld(s_ref) + wb))
            return carry

        lax.fori_loop(0, seq // blk // ATTN_UNROLL, blocks, 0)

    out_ref[...] = jnp.concatenate([o_ref[c] for c in range(ntile)], axis=1).astype(out_ref.dtype)


def _attention(qkv, bias, gains, bd, bsz, seq):
    gcols = 3 * GROUP_WIDTH
    const = lambda a: pl.BlockSpec(a.shape, lambda b: (0,) * a.ndim)
    seq_spec = pl.BlockSpec((None, seq, gcols), lambda b: (b, 0, 0))
    out = pl.pallas_call(
        _attn_kernel,
        grid=(bsz,),
        in_specs=[seq_spec, seq_spec, seq_spec, const(bias), const(gains), const(bd)],
        out_specs=pl.BlockSpec((None, seq, GROUP_WIDTH), lambda b: (b, 0, 0)),
        out_shape=jax.ShapeDtypeStruct((bsz, seq, GROUP_WIDTH), BF16),
        scratch_shapes=[pltpu.VMEM((ATTN_BLOCK + seq, gcols), BF16)]
        + [pltpu.VMEM((GROUP_WIDTH // LANES, seq, LANES), F32)] * 3,
        compiler_params=_cparams(("parallel",)),
        name="attn",
    )(*qkv, bias, gains, bd)
    return out.reshape(bsz * seq, GROUP_WIDTH)


def _t5_bucket(dist):
    max_exact = N_BUCKETS // 2
    large = max_exact + (jnp.log(jnp.maximum(dist, max_exact).astype(F32) / max_exact)
                         / math.log(MAX_DISTANCE / max_exact) * (N_BUCKETS - max_exact)).astype(jnp.int32)
    return jnp.where(dist < max_exact, dist, jnp.minimum(large, N_BUCKETS - 1))


def _band_bias(rel_bias, gi):
    window, dil = ATTN_GROUPS[gi]
    qi = jnp.arange(ATTN_BLOCK)[:, None]
    kj = jnp.arange(2 * ATTN_BLOCK)[None, :]
    steps = qi - kj + ATTN_BLOCK
    heads = slice(gi * HEADS_PER_GROUP, (gi + 1) * HEADS_PER_GROUP)
    bucket = _t5_bucket(jnp.maximum(steps, 0) * dil)
    onehot = (bucket[None] == jnp.arange(N_BUCKETS)[:, None, None]).astype(F32)
    bias = jnp.sum(onehot[:, None] * rel_bias[:, heads][:, :, None, None], axis=0)
    band = (steps >= 0) & (steps <= window // dil)
    later = jnp.where(band[None], bias.astype(F32), NEG_INF)
    first = jnp.where((kj >= ATTN_BLOCK)[None], later, NEG_INF)
    return jnp.stack([later, first])


def _post_kernel(x_ref, yr_ref, att_ref, pg_ref,
                 gate1_ref, scale2_ref, shift2_ref, g2_ref, wbr_ref, wba_ref, wout_ref,
                 rwh_ref, rwl_ref, rb_ref,
                 x1_ref, h2_ref, gates_ref, eidx_ref, rank_ref, counts_ref, base_ref, run_ref):
    i = pl.program_id(0)
    tm = x_ref.shape[0]

    @pl.when(i == 0)
    def _():
        run_ref[...] = jnp.zeros_like(run_ref)

    y_r = _dot(yr_ref[...], wbr_ref[...])
    y_a = _dot(att_ref[...], wba_ref[...])
    d = D_MODEL
    mixed = (_sigmoid(pg_ref[:, :d].astype(F32)) * y_r
             + _sigmoid(pg_ref[:, d:].astype(F32)) * y_a)
    x1 = x_ref[...] + gate1_ref[0] * _dot(mixed.astype(BF16), wout_ref[...])
    x1_ref[...] = x1

    ms = jnp.mean(x1 * x1, axis=-1, keepdims=True)
    h2 = x1 * lax.rsqrt(ms + NORM_EPS) * (g2_ref[...] * (1.0 + scale2_ref[0])) + shift2_ref[0]
    h2_ref[...] = h2.astype(h2_ref.dtype)

    h_hi, h_lo = _split2(h2)
    rw_hi, rw_lo = rwh_ref[...], rwl_ref[...]
    logits = (_dot_nt(rw_hi, h_hi) + _dot_nt(rw_hi, h_lo) + _dot_nt(rw_lo, h_hi)
              + rb_ref[:, 0:1])
    eid = lax.broadcasted_iota(jnp.int32, (N_EXPERTS, tm), 0)
    vals, hots = [], []
    lg = logits
    for k in range(TOP_K):
        m = jnp.max(lg, axis=0, keepdims=True)
        idx = jnp.min(jnp.where(lg == m, eid, N_EXPERTS), axis=0, keepdims=True)
        hot = eid == idx
        vals.append(m)
        hots.append(hot)
        eidx_ref[k:k + 1, :] = idx
        lg = jnp.where(hot, -jnp.inf, lg)
    exps = [jnp.exp(v - vals[0]) for v in vals]
    tot = exps[0] + exps[1] + exps[2] + exps[3]
    for k in range(TOP_K):
        gates_ref[k:k + 1, :] = exps[k] / tot

    chosen = jnp.zeros((N_EXPERTS, tm), F32)
    for hot in hots:
        chosen = chosen + jnp.where(hot, 1.0, 0.0)
    ti = lax.broadcasted_iota(jnp.int32, (tm, tm), 0)
    tj = lax.broadcasted_iota(jnp.int32, (tm, tm), 1)
    before = jnp.where(ti < tj, 1.0, 0.0).astype(BF16)
    run = run_ref[:, 0:1]
    base = _dot(chosen.astype(BF16), before) + run
    for k in range(TOP_K):
        rank_ref[k:k + 1, :] = jnp.sum(jnp.where(hots[k], base, 0.0), axis=0,
                                       keepdims=True).astype(jnp.int32)
    run_new = run + jnp.sum(chosen, axis=1, keepdims=True)
    run_ref[...] = jnp.broadcast_to(run_new, run_ref.shape)
    counts_ref[...] = jnp.broadcast_to(run_new, counts_ref.shape).astype(jnp.int32)
    lane = lax.broadcasted_iota(jnp.int32, base_ref.shape, 1)
    starts = jnp.zeros(base_ref.shape, F32)
    for j in range(tm // ROUTE_TILE):
        starts = jnp.where(lane == j, base[:, j * ROUTE_TILE:j * ROUTE_TILE + 1], starts)
    base_ref[...] = starts.astype(jnp.int32)


def _post(x2, y_rwkv, att, pg, gate1, scale2, shift2, norm2_g, w_br_r, w_br_a, w_out,
          rw_hi, rw_lo, rb, seq):
    n, d = x2.shape
    tm = TOKEN_TILE
    per = seq // tm
    row = lambda width: pl.BlockSpec((tm, width), lambda i: (i, 0))
    const = lambda shape: pl.BlockSpec(shape, lambda i: (0,) * len(shape))
    bvec = pl.BlockSpec((1, 1, d), lambda i: (i // per, 0, 0))
    tcol = pl.BlockSpec((TOP_K, tm), lambda i: (0, i))
    return pl.pallas_call(
        _post_kernel,
        grid=(n // tm,),
        in_specs=[row(d), row(RWKV_WIDTH), row(GROUP_WIDTH), row(GATE_COLS),
                  bvec, bvec, bvec, const((1, d)), const(w_br_r.shape), const(w_br_a.shape),
                  const(w_out.shape), const(rw_hi.shape), const(rw_lo.shape), const(rb.shape)],
        out_specs=[row(d), row(d), tcol, tcol, tcol, const((N_EXPERTS, LANES)),
                   pl.BlockSpec((None, N_EXPERTS, LANES), lambda i: (i, 0, 0))],
        out_shape=[jax.ShapeDtypeStruct((n, d), F32),
                   jax.ShapeDtypeStruct((n, d), BF16),
                   jax.ShapeDtypeStruct((TOP_K, n), F32),
                   jax.ShapeDtypeStruct((TOP_K, n), jnp.int32),
                   jax.ShapeDtypeStruct((TOP_K, n), jnp.int32),
                   jax.ShapeDtypeStruct((N_EXPERTS, LANES), jnp.int32),
                   jax.ShapeDtypeStruct((n // tm, N_EXPERTS, LANES), jnp.int32)],
        scratch_shapes=[pltpu.VMEM((N_EXPERTS, 128), F32)],
        compiler_params=_cparams(("arbitrary",)),
        name="post",
    )(x2, y_rwkv, att, pg, gate1, scale2, shift2, norm2_g, w_br_r, w_br_a, w_out,
      rw_hi, rw_lo, rb)


def _pack_halves(x):
    w = x.shape[1] // 2
    lo = lax.bitcast_convert_type(x[:, :w], jnp.uint32)
    hi = lax.bitcast_convert_type(x[:, w:], jnp.uint32)
    return (hi & jnp.uint32(0xFFFF0000)) | (lo >> 16)


def _unpack_halves(u):
    lo = lax.bitcast_convert_type(u << 16, F32).astype(BF16)
    hi = lax.bitcast_convert_type(u & jnp.uint32(0xFFFF0000), F32).astype(BF16)
    return lo, hi


def _slab_rows(eidx_ref, rank_ref, base_ref, sub):
    tt = eidx_ref.shape[1]
    eid = lax.broadcasted_iota(jnp.int32, (N_EXPERTS, tt), 0)
    which = sub % (TOKEN_TILE // ROUTE_TILE)
    basevec = base_ref[:, 0:1]
    for j in range(1, TOKEN_TILE // ROUTE_TILE):
        basevec = jnp.where(which == j, base_ref[:, j:j + 1], basevec)
    lead = basevec & (ROW_ALIGN - 1)
    rows = jnp.full((N_EXPERTS, tt), -1, jnp.int32)
    for k in range(TOP_K):
        rows = jnp.where(eidx_ref[k:k + 1, :] == eid, rank_ref[k:k + 1, :] - (basevec - lead), rows)
    return rows, lead


def _dispatch_kernel(sbase_ref, scnt_ref, pstart_ref, nx_ref, xlist_ref, fill_ref,
                     eidx_ref, rank_ref, base_ref, h2_ref, xs_ref,
                     stage_ref, xstage_ref, extra_ref, zero_ref, carry_ref, rows_ref, sem, xsem):
    s = pl.program_id(0)
    nblk = xs_ref.shape[0] // MOE_ROWS

    @pl.when(s == 0)
    def _():
        zero_ref[...] = jnp.zeros_like(zero_ref)

        def fill(j, carry):
            @pl.when(fill_ref[j] != 0)
            def _():
                row0 = pl.multiple_of(j * MOE_ROWS, MOE_ROWS)
                cp = pltpu.make_async_copy(zero_ref, xs_ref.at[pl.ds(row0, MOE_ROWS)], xsem)
                cp.start()
                cp.wait()
            return carry

        lax.fori_loop(0, nblk, fill, 0)

    @pl.when(s == 0)
    def _():
        carry_ref[...] = jnp.zeros_like(carry_ref)

    rows, leadvec = _slab_rows(eidx_ref, rank_ref, base_ref, s)
    h2 = h2_ref[...]
    rid = lax.broadcasted_iota(jnp.int32, (SLAB, 1), 0)
    align = ROW_ALIGN

    def onehot(row, chunk):
        return jnp.where(row - chunk * SLAB == rid, 1.0, 0.0).astype(BF16)

    def lead(e):
        return sbase_ref[s * N_EXPERTS + e] & (align - 1)

    def used(e):
        return lead(e) + scnt_ref[s * N_EXPERTS + e]

    def window_start(e):
        return pl.multiple_of(pstart_ref[e] + sbase_ref[s * N_EXPERTS + e] - lead(e), align)

    total = leadvec + jnp.sum(jnp.where(rows >= 0, 1, 0), axis=1, keepdims=True)
    group0 = total - (total & (align - 1))
    gid = lax.broadcasted_iota(jnp.int32, (align, 1), 0)
    slot = s % 2
    for g0 in range(0, N_EXPERTS, EXPERT_GROUP):
        sel = jnp.concatenate([onehot(rows[e:e + 1, :], 0)
                               for e in range(g0, g0 + EXPERT_GROUP)], axis=0)
        packed = _pack_halves(_dot(sel, h2))
        for e in range(g0, g0 + EXPERT_GROUP):
            stage_ref[slot, e] = packed[(e - g0) * SLAB:(e - g0 + 1) * SLAB]
            stage_ref[slot, e, 0:align, :] = stage_ref[slot, e, 0:align, :] | carry_ref[e]
    tails = jnp.concatenate(
        [jnp.where(rows[e:e + 1, :] - group0[e:e + 1, :] == gid, 1.0, 0.0).astype(BF16)
         for e in range(N_EXPERTS)], axis=0)
    tail_rows = _pack_halves(_dot(tails, h2))
    for e in range(N_EXPERTS):
        keep = used(e) < align
        new = tail_rows[e * align:(e + 1) * align]
        carry_ref[e] = jnp.where(keep, new | carry_ref[e], new)

    rows_ref[...] = rows

    def extra(i):
        cand = xlist_ref[s * EXTRA_MAX + i]
        e = cand // EXTRA_PER
        c = cand % EXTRA_PER + 1
        first = pstart_ref[e] + sbase_ref[s * N_EXPERTS + e]
        return e, c, pl.multiple_of(first - (first & (align - 1)) + c * SLAB, align)

    sels = []
    for i in range(EXTRA_BATCH):
        e, c, _ = extra(i)
        hit = jnp.logical_and(rows_ref[pl.ds(e, 1), :] - c * SLAB == rid, i < nx_ref[s])
        sels.append(jnp.where(hit, 1.0, 0.0).astype(BF16))
    xpacked = _pack_halves(_dot(jnp.concatenate(sels, axis=0), h2))
    for i in range(EXTRA_BATCH):
        xstage_ref[slot, i] = xpacked[i * SLAB:(i + 1) * SLAB]

    @pl.when(s > 0)
    def _():
        for e in range(N_EXPERTS):
            pltpu.make_async_copy(stage_ref.at[1 - slot, e], xs_ref.at[pl.ds(0, SLAB)],
                                  sem.at[1 - slot]).wait()
        for i in range(EXTRA_BATCH):
            @pl.when(i < nx_ref[s - 1])
            def _():
                pltpu.make_async_copy(xstage_ref.at[1 - slot, i], xs_ref.at[pl.ds(0, SLAB)],
                                      sem.at[1 - slot]).wait()

    slab_copy = lambda e: pltpu.make_async_copy(
        stage_ref.at[slot, e], xs_ref.at[pl.ds(window_start(e), SLAB)], sem.at[slot])
    xslab_copy = lambda i: pltpu.make_async_copy(
        xstage_ref.at[slot, i], xs_ref.at[pl.ds(extra(i)[2], SLAB)], sem.at[slot])
    for e in range(N_EXPERTS):
        slab_copy(e).start()
    for i in range(EXTRA_BATCH):
        @pl.when(i < nx_ref[s])
        def _():
            xslab_copy(i).start()

    def more(i, carry):
        e, c, first = extra(i)
        extra_ref[...] = _pack_halves(_dot(onehot(rows_ref[pl.ds(e, 1), :], c), h2))
        cp = pltpu.make_async_copy(extra_ref, xs_ref.at[pl.ds(first, SLAB)], xsem)
        cp.start()
        cp.wait()
        return carry

    lax.fori_loop(EXTRA_BATCH, nx_ref[s], more, 0)

    @pl.when(s == pl.num_programs(0) - 1)
    def _():
        for e in range(N_EXPERTS):
            slab_copy(e).wait()
        for i in range(EXTRA_BATCH):
            @pl.when(i < nx_ref[s])
            def _():
                xslab_copy(i).wait()


def _dispatch(route, fill, eidx, rank, base, h2, nblk):
    n, d = h2.shape
    tt = ROUTE_TILE
    subs = TOKEN_TILE // tt
    tcol = pl.BlockSpec((TOP_K, tt), lambda s, *_: (0, s))
    return pl.pallas_call(
        _dispatch_kernel,
        grid_spec=pltpu.PrefetchScalarGridSpec(
            num_scalar_prefetch=6,
            grid=(n // tt,),
            in_specs=[tcol, tcol,
                      pl.BlockSpec((None, N_EXPERTS, LANES), lambda s, *_: (s // subs, 0, 0)),
                      pl.BlockSpec((tt, d), lambda s, *_: (s, 0))],
            out_specs=pl.BlockSpec(memory_space=pl.ANY),
            scratch_shapes=[pltpu.VMEM((2, N_EXPERTS, SLAB, d // 2), jnp.uint32),
                            pltpu.VMEM((2, EXTRA_BATCH, SLAB, d // 2), jnp.uint32),
                            pltpu.VMEM((SLAB, d // 2), jnp.uint32),
                            pltpu.VMEM((MOE_ROWS, d // 2), jnp.uint32),
                            pltpu.VMEM((N_EXPERTS, ROW_ALIGN, d // 2), jnp.uint32),
                            pltpu.VMEM((N_EXPERTS, tt), jnp.int32),
                            pltpu.SemaphoreType.DMA((2,)), pltpu.SemaphoreType.DMA(())]),
        out_shape=jax.ShapeDtypeStruct((nblk * MOE_ROWS, d // 2), jnp.uint32),
        compiler_params=_cparams(("arbitrary",)),
        name="dispatch",
    )(*route, fill, eidx, rank, base, h2)


def _expert_kernel(blk_ref, src_ref, valid_ref, xs_ref, w1g_ref, w1l_ref, b1g_ref, b1l_ref,
                   w2_ref, b2_ref, o_ref):
    j = pl.program_id(0)

    @pl.when(valid_ref[j] != 0)
    def _():
        lo, hi = _unpack_halves(xs_ref[...])
        half = lo.shape[1]

        def up(w_ref, b_ref):
            return _dot(lo, w_ref[:half, :]) + _dot(hi, w_ref[half:, :]) + b_ref[...]

        glu = jnp.minimum(up(w1g_ref, b1g_ref), SWIGLU_LIMIT)
        lin = jnp.clip(up(w1l_ref, b1l_ref), -SWIGLU_LIMIT, SWIGLU_LIMIT)
        act = glu * _sigmoid(SWIGLU_ALPHA * glu) * (lin + 1.0)
        o = _dot(act.astype(BF16), w2_ref[...]) + b2_ref[...]
        o_ref[...] = _pack_halves(o.astype(BF16).astype(F32))

    @pl.when(valid_ref[j] == 0)
    def _():
        o_ref[...] = jnp.zeros_like(o_ref)


def _experts(blk_e, src, valid, xs, w1g, w1l, b1g, b1l, w2, b2):
    rows, half = xs.shape
    nblk = rows // MOE_ROWS
    d, de = w1g.shape[1], w1g.shape[2]
    emap = lambda j, be, src, valid: (be[j], 0, 0)
    return pl.pallas_call(
        _expert_kernel,
        grid_spec=pltpu.PrefetchScalarGridSpec(
            num_scalar_prefetch=3,
            grid=(nblk,),
            in_specs=[pl.BlockSpec((MOE_ROWS, half), lambda j, be, src, valid: (src[j], 0)),
                      pl.BlockSpec((None, d, de), emap), pl.BlockSpec((None, d, de), emap),
                      pl.BlockSpec((None, 1, de), emap), pl.BlockSpec((None, 1, de), emap),
                      pl.BlockSpec((None, de, d), emap), pl.BlockSpec((None, 1, d), emap)],
            out_specs=pl.BlockSpec((MOE_ROWS, half), lambda j, be, src, valid: (j, 0))),
        out_shape=jax.ShapeDtypeStruct((rows, half), jnp.uint32),
        compiler_params=_cparams(("arbitrary",)),
        name="experts",
    )(blk_e, src, valid, xs, w1g, w1l, b1g, b1l, w2, b2)


def _split_w1_kernel(w_ref, perm_ref, g_ref, l_ref):
    t = MXU_TILE
    for cb in range(w_ref.shape[1] // t):
        res = _dot(w_ref[:, cb * t:(cb + 1) * t].astype(BF16), perm_ref[...])
        g_ref[:, cb * (t // 2):(cb + 1) * (t // 2)] = res[:, :t // 2].astype(BF16)
        l_ref[:, cb * (t // 2):(cb + 1) * (t // 2)] = res[:, t // 2:].astype(BF16)


def _split_w1(w1):
    ne, d, two_de = w1.shape
    t = MXU_TILE
    src = jnp.arange(t)[:, None]
    dst = jnp.arange(t)[None, :]
    perm = (src == jnp.where(dst < t // 2, 2 * dst, 2 * (dst - t // 2) + 1)).astype(BF16)
    return pl.pallas_call(
        _split_w1_kernel,
        grid=(ne,),
        in_specs=[pl.BlockSpec((None, d, two_de), lambda e: (e, 0, 0)),
                  pl.BlockSpec((t, t), lambda e: (0, 0))],
        out_specs=[pl.BlockSpec((None, d, two_de // 2), lambda e: (e, 0, 0)),
                   pl.BlockSpec((None, d, two_de // 2), lambda e: (e, 0, 0))],
        out_shape=[jax.ShapeDtypeStruct((ne, d, two_de // 2), BF16),
                   jax.ShapeDtypeStruct((ne, d, two_de // 2), BF16)],
        compiler_params=_cparams(("parallel",)),
        name="split_w1",
    )(w1, perm)


def _combine_kernel(sbase_ref, scnt_ref, pstart_ref, nx_ref, xlist_ref,
                    eidx_ref, rank_ref, gates_ref, base_ref, x1_ref, gate2_ref, o_hbm, out_ref,
                    slab_ref, xslab_ref, extra_ref, acc_ref, rows_ref, ge_ref, sem, xsem):
    s = pl.program_id(0)
    last = pl.num_programs(0) - 1
    half = acc_ref.shape[1] // 2
    slot = s % 2

    def window_start(sub, e, chunk=0):
        first = pstart_ref[e] + sbase_ref[sub * N_EXPERTS + e]
        return pl.multiple_of(first - (first & (ROW_ALIGN - 1)) + chunk * SLAB, ROW_ALIGN)

    def slab_copy(sub, e, to):
        return pltpu.make_async_copy(o_hbm.at[pl.ds(window_start(sub, e), SLAB)],
                                     slab_ref.at[to, e], sem.at[to])

    def extra(sub, i):
        cand = xlist_ref[sub * EXTRA_MAX + i]
        return cand // EXTRA_PER, cand % EXTRA_PER + 1

    def xslab_copy(sub, i, to):
        e, c = extra(sub, i)
        return pltpu.make_async_copy(o_hbm.at[pl.ds(window_start(sub, e, c), SLAB)],
                                     xslab_ref.at[to, i], sem.at[to])

    def fetch(sub, to):
        for e in range(N_EXPERTS):
            slab_copy(sub, e, to).start()
        for i in range(EXTRA_BATCH):
            @pl.when(i < nx_ref[sub])
            def _():
                xslab_copy(sub, i, to).start()

    @pl.when(s == 0)
    def _():
        xslab_ref[...] = jnp.zeros_like(xslab_ref)
        fetch(0, 0)

    @pl.when(s < last)
    def _():
        fetch(s + 1, 1 - slot)

    rows, _ = _slab_rows(eidx_ref, rank_ref, base_ref, s)
    eid = lax.broadcasted_iota(jnp.int32, rows.shape, 0)
    ge = jnp.zeros(rows.shape, F32)
    for k in range(TOP_K):
        ge = jnp.where(eidx_ref[k:k + 1, :] == eid, gates_ref[k:k + 1, :], ge)
    rid = lax.broadcasted_iota(jnp.int32, (SLAB, 1), 0)

    def weights(row, gate, chunk):
        return jnp.where(row - chunk * SLAB == rid, gate, 0.0).astype(BF16)

    rows_ref[...] = rows
    ge_ref[...] = ge
    for e in range(N_EXPERTS):
        slab_copy(s, e, slot).wait()
    for i in range(EXTRA_BATCH):
        @pl.when(i < nx_ref[s])
        def _():
            xslab_copy(s, i, slot).wait()
    acc_lo = acc_hi = None
    for g0 in range(0, N_EXPERTS, EXPERT_GROUP):
        wt = jnp.concatenate([weights(rows[e:e + 1, :], ge[e:e + 1, :], 0)
                              for e in range(g0, g0 + EXPERT_GROUP)], axis=0)
        lo, hi = _unpack_halves(
            slab_ref[slot, g0:g0 + EXPERT_GROUP].reshape(EXPERT_GROUP * SLAB, half))
        p_lo, p_hi = _dot_tn(wt, lo), _dot_tn(wt, hi)
        acc_lo = p_lo if acc_lo is None else acc_lo + p_lo
        acc_hi = p_hi if acc_hi is None else acc_hi + p_hi
    wts = []
    for i in range(EXTRA_BATCH):
        e, c = extra(s, i)
        gate = jnp.where(i < nx_ref[s], ge_ref[pl.ds(e, 1), :], 0.0)
        wts.append(weights(rows_ref[pl.ds(e, 1), :], gate, c))
    lo, hi = _unpack_halves(xslab_ref[slot].reshape(EXTRA_BATCH * SLAB, half))
    wt = jnp.concatenate(wts, axis=0)
    acc_ref[:, :half] = acc_lo + _dot_tn(wt, lo)
    acc_ref[:, half:] = acc_hi + _dot_tn(wt, hi)

    def more(i, carry):
        e, c = extra(s, i)
        cp = pltpu.make_async_copy(o_hbm.at[pl.ds(window_start(s, e, c), SLAB)], extra_ref, xsem)
        cp.start()
        cp.wait()
        lo, hi = _unpack_halves(extra_ref[...])
        wt = weights(rows_ref[pl.ds(e, 1), :], ge_ref[pl.ds(e, 1), :], c)
        acc_ref[:, :half] += _dot_tn(wt, lo)
        acc_ref[:, half:] += _dot_tn(wt, hi)
        return carry

    lax.fori_loop(EXTRA_BATCH, nx_ref[s], more, 0)
    out_ref[...] = x1_ref[...] + gate2_ref[0] * acc_ref[...]


def _combine(route, eidx, rank, gates, base, x1, gate2, o_rows, seq):
    n, d = x1.shape
    tt = ROUTE_TILE
    subs = TOKEN_TILE // tt
    per = seq // tt
    tcol = pl.BlockSpec((TOP_K, tt), lambda s, *_: (0, s))
    return pl.pallas_call(
        _combine_kernel,
        grid_spec=pltpu.PrefetchScalarGridSpec(
            num_scalar_prefetch=5,
            grid=(n // tt,),
            in_specs=[tcol, tcol, tcol,
                      pl.BlockSpec((None, N_EXPERTS, LANES), lambda s, *_: (s // subs, 0, 0)),
                      pl.BlockSpec((tt, d), lambda s, *_: (s, 0)),
                      pl.BlockSpec((1, 1, d), lambda s, *_: (s // per, 0, 0)),
                      pl.BlockSpec(memory_space=pl.ANY)],
            out_specs=pl.BlockSpec((tt, d), lambda s, *_: (s, 0)),
            scratch_shapes=[pltpu.VMEM((2, N_EXPERTS, SLAB, d // 2), jnp.uint32),
                            pltpu.VMEM((2, EXTRA_BATCH, SLAB, d // 2), jnp.uint32),
                            pltpu.VMEM((SLAB, d // 2), jnp.uint32),
                            pltpu.VMEM((tt, d), F32),
                            pltpu.VMEM((N_EXPERTS, tt), jnp.int32),
                            pltpu.VMEM((N_EXPERTS, tt), F32),
                            pltpu.SemaphoreType.DMA((2,)), pltpu.SemaphoreType.DMA(())]),
        out_shape=jax.ShapeDtypeStruct((n, d), F32),
        compiler_params=_cparams(("arbitrary",)),
        name="combine",
    )(*route, eidx, rank, gates, base, x1, gate2, o_rows)


def _layer(x, c, ada_w, ada_b, norm1_g, norm2_g, w_in, mu_rkv, mu_wag, w0, w1, w2, a0, a1, a2,
           g1, g2, k_k, k_a, r_k, ln_w, ln_b, qn_g, kn_g, rel_bias, w_br_rwkv, w_br_attn, w_out,
           router_w, router_b, exp_w1, exp_b1, exp_w2, exp_b2):
    bsz, seq, d = x.shape
    n = bsz * seq
    rw = RWKV_WIDTH
    assert d == D_MODEL and seq % (ATTN_BLOCK * ATTN_GROUPS[-1][1]) == 0 and seq % TOKEN_TILE == 0

    mod = _modulation(c, ada_w, ada_b)
    shift1, scale1, gate1, shift2, scale2, gate2 = [
        m.reshape(bsz, 1, d) for m in jnp.split(mod, 6, axis=-1)]

    lora_w = jnp.concatenate([w1, a1, g1], axis=1)
    lora_mu = jnp.concatenate([jnp.broadcast_to(mu_wag[0][:, None], w1.shape),
                               jnp.broadcast_to(mu_wag[1][:, None], a1.shape),
                               jnp.broadcast_to(mu_wag[2][:, None], g1.shape)], axis=1)
    w_r = jnp.concatenate([w_in[:, :3 * rw], lora_w * (1.0 - lora_mu), lora_w * lora_mu],
                          axis=1).astype(BF16)
    qa0 = 3 * rw
    acols = []
    for gi in range(len(ATTN_GROUPS)):
        for part in range(3):
            lo = qa0 + part * ATTN_WIDTH + gi * GROUP_WIDTH
            acols.append(w_in[:, lo:lo + GROUP_WIDTH])
    w_a = jnp.concatenate(acols, axis=1).astype(BF16)
    w_g = w_in[:, qa0 + 3 * ATTN_WIDTH:].astype(BF16)

    x2 = x.reshape(n, d)
    pr, pg, qkv = _inproj(x2, norm1_g.reshape(1, d), scale1, shift1, w_r, w_a, w_g, bsz, seq)

    par = jnp.stack([w0, a0, k_k, k_a, r_k.reshape(rw), ln_w, ln_b,
                     mu_rkv[0], mu_rkv[1], mu_rkv[2]]
                    + [jnp.zeros((rw,), F32)] * 6)
    w_lora = jnp.concatenate([jnp.pad(w2, ((0, 0), (0, 2 * rw))),
                              jnp.pad(a2, ((0, 0), (rw, rw))),
                              jnp.pad(g2, ((0, 0), (2 * rw, 0)))], axis=0).astype(BF16)
    hid = jnp.arange(rw) // HEAD_DIM
    ones_bd = (hid[:, None] == hid[None, :]).astype(BF16)
    y_rwkv = _rwkv(pr, par, w_lora, ones_bd, bsz, seq)

    gains = jnp.pad(jnp.stack([jnp.tile(qn_g, HEADS_PER_GROUP), jnp.tile(kn_g, HEADS_PER_GROUP)]),
                    ((0, 6), (0, 0)))
    bias = jnp.stack([_band_bias(rel_bias, gi) for gi in range(len(ATTN_GROUPS))])
    ghead = jnp.arange(GROUP_WIDTH) // HEAD_DIM
    mean_bd = ((ghead[:, None] == ghead[None, :]).astype(F32) / HEAD_DIM).astype(BF16)
    att = _attention(qkv, bias, gains, mean_bd, bsz, seq)

    rw_t = router_w.T
    rw_hi = rw_t.astype(BF16)
    rw_lo = (rw_t - rw_hi.astype(F32)).astype(BF16)
    rb = jnp.broadcast_to(router_b[:, None], (N_EXPERTS, 128))
    x1, h2, gates, eidx, rank, counts, base = _post(
        x2, y_rwkv, att, pg, gate1, scale2, shift2, norm2_g.reshape(1, d),
        w_br_rwkv.astype(BF16), w_br_attn.astype(BF16), w_out.astype(BF16), rw_hi, rw_lo, rb, seq)

    i32 = jnp.int32
    cnt = counts[:, 0]
    real = (cnt + MOE_ROWS - 1) // MOE_ROWS
    reserved = (cnt + SLAB + ROW_ALIGN + MOE_ROWS - 1) // MOE_ROWS
    bend = jnp.cumsum(reserved)
    bstart = bend - reserved
    nblk = (n * TOP_K) // MOE_ROWS + 2 * N_EXPERTS
    j = jnp.arange(nblk, dtype=i32)
    owner = jnp.clip(jnp.sum((bend[None, :] <= j[:, None]).astype(i32), axis=1), 0, N_EXPERTS - 1)
    local = j - jnp.take(bstart, owner)
    valid = local < jnp.take(real, owner)
    src = lax.cummax(jnp.where(valid, j, 0))
    blk_e = jnp.take(owner, src)
    fill = jnp.logical_or(~valid, local == jnp.take(real, owner) - 1)
    pstart = bstart * MOE_ROWS
    subs = TOKEN_TILE // ROUTE_TILE
    sbase = jnp.transpose(base[:, :, :subs], (0, 2, 1)).reshape(n // ROUTE_TILE, N_EXPERTS)
    scnt = jnp.concatenate([sbase[1:], cnt[None, :]], axis=0) - sbase
    chunks = ((sbase & (ROW_ALIGN - 1)) + scnt + SLAB - 1) // SLAB
    wanted = (jnp.arange(1, EXTRA_PER + 1, dtype=i32)[None, None, :] < chunks[:, :, None])
    wanted = wanted.reshape(n // ROUTE_TILE, N_EXPERTS * EXTRA_PER)
    xlist = jnp.argsort(jnp.logical_not(wanted), axis=1, stable=True)[:, :EXTRA_MAX]
    nx = jnp.sum(wanted.astype(i32), axis=1)
    route = (sbase.reshape(-1).astype(i32), scnt.reshape(-1).astype(i32), pstart.astype(i32),
             nx.astype(i32), xlist.reshape(-1).astype(i32))

    xs = _dispatch(route, fill.astype(i32), eidx, rank, base, h2, nblk)
    w1g, w1l = _split_w1(exp_w1)
    o_rows = _experts(blk_e.astype(i32), src.astype(i32), valid.astype(i32), xs, w1g, w1l,
                      exp_b1[:, None, 0::2], exp_b1[:, None, 1::2],
                      exp_w2.astype(BF16), exp_b2[:, None, :])
    out = _combine(route, eidx, rank, gates, base, x1, gate2, o_rows, seq)
    return out.reshape(bsz, seq, d)


def kernel(x, c, ada_w, ada_b, norm1_g, norm2_g, w_in, rwkv_mu_rkv, rwkv_mu_wag, rwkv_w0, rwkv_w1, rwkv_w2, rwkv_a0, rwkv_a1, rwkv_a2, rwkv_g1, rwkv_g2, rwkv_k_k, rwkv_k_a, rwkv_r_k, rwkv_ln_w, rwkv_ln_b, attn_qn_g, attn_kn_g, rel_bias, w_br_rwkv, w_br_attn, w_out, router_w, router_b, exp_w1, exp_b1, exp_w2, exp_b2):
    per_layer = (ada_w, ada_b, norm1_g, norm2_g, w_in, rwkv_mu_rkv, rwkv_mu_wag, rwkv_w0, rwkv_w1,
                 rwkv_w2, rwkv_a0, rwkv_a1, rwkv_a2, rwkv_g1, rwkv_g2, rwkv_k_k, rwkv_k_a,
                 rwkv_r_k, rwkv_ln_w, rwkv_ln_b, attn_qn_g, attn_kn_g)
    tail = (w_br_rwkv, w_br_attn, w_out, router_w, router_b, exp_w1, exp_b1, exp_w2, exp_b2)
    for layer in range(ada_w.shape[0]):
        head = [p[layer] for p in per_layer]
        rest = [p[layer] for p in tail]
        x = _layer(x, c, *head, rel_bias, *rest)
    return x
```

```python
import math

import jax
import jax.numpy as jnp
from jax import lax
from jax.experimental import pallas as pl
from jax.experimental.pallas import tpu as pltpu

F32 = jnp.float32
BF16 = jnp.bfloat16

D_MODEL = 1024
HEAD_DIM = 64
RWKV_HEADS = 8
RWKV_WIDTH = RWKV_HEADS * HEAD_DIM
DECAY_LORA = 64
ICLR_LORA = 64
GATE_LORA = 128
LORA_WIDTH = DECAY_LORA + ICLR_LORA + GATE_LORA
GROUPNORM_EPS = 64e-5
ATTN_GROUPS = ((128, 1), (512, 4), (2048, 16))
HEADS_PER_GROUP = 4
ATTN_HEADS = HEADS_PER_GROUP * len(ATTN_GROUPS)
ATTN_WIDTH = ATTN_HEADS * HEAD_DIM
GROUP_WIDTH = HEADS_PER_GROUP * HEAD_DIM
ATTN_BLOCK = 128
N_BUCKETS = 32
MAX_DISTANCE = 2048
N_EXPERTS = 32
TOP_K = 4
SWIGLU_ALPHA = 1.702
SWIGLU_LIMIT = 7.0
NORM_EPS = 1e-6
NEG_INF = -1e30

RWKV_COLS = 3 * RWKV_WIDTH + 2 * LORA_WIDTH
GATE_COLS = 2 * D_MODEL

CHUNK = 64
RWKV_BATCH = 4
ATTN_UNROLL = 16
TOKEN_TILE = 512
MOE_ROWS = 512
ROUTE_TILE = 256
SLAB = 64
EXTRA_BATCH = 8
ROW_ALIGN = 8
EXTRA_PER = (ROW_ALIGN - 1 + ROUTE_TILE + SLAB - 1) // SLAB - 1
EXTRA_MAX = (N_EXPERTS * (ROW_ALIGN - 1) + TOP_K * ROUTE_TILE) // SLAB + 1
EXPERT_GROUP = 8
MXU_TILE = 256
LANES = 128
VMEM_LIMIT = 56 * 1024 * 1024


def _cparams(sem, vmem=VMEM_LIMIT):
    return pltpu.CompilerParams(dimension_semantics=sem, vmem_limit_bytes=vmem)


def _split2(a):
    hi = a.astype(BF16)
    lo = (a - hi.astype(F32)).astype(BF16)
    return hi, lo


def _split3(a):
    hi = a.astype(BF16)
    r = a - hi.astype(F32)
    mid = r.astype(BF16)
    lo = (r - mid.astype(F32)).astype(BF16)
    return hi, mid, lo


def _dot(a, b):
    return jnp.dot(a, b, preferred_element_type=F32)


def _dot_nt(a, b):
    return lax.dot_general(a, b, (((1,), (1,)), ((), ())), preferred_element_type=F32)


def _dot_tn(a, b):
    return lax.dot_general(a, b, (((0,), (0,)), ((), ())), preferred_element_type=F32)


def _sigmoid(x):
    return 0.5 * jnp.tanh(0.5 * x) + 0.5


def _mod_kernel(c_ref, w_ref, b_ref, o_ref):
    c = c_ref[...]
    s = c * _sigmoid(c)
    s_hi, s_lo = _split2(s)
    w_hi, w_lo = _split2(w_ref[...])
    o_ref[...] = _dot(s_hi, w_hi) + _dot(s_hi, w_lo) + _dot(s_lo, w_hi) + b_ref[...]


def _modulation(c, ada_w, ada_b):
    bsz, d = c.shape
    cols = ada_w.shape[1]
    tn = 1024
    return pl.pallas_call(
        _mod_kernel,
        grid=(cols // tn,),
        in_specs=[pl.BlockSpec((bsz, d), lambda j: (0, 0)),
                  pl.BlockSpec((d, tn), lambda j: (0, j)),
                  pl.BlockSpec((1, tn), lambda j: (0, j))],
        out_specs=pl.BlockSpec((bsz, tn), lambda j: (0, j)),
        out_shape=jax.ShapeDtypeStruct((bsz, cols), F32),
        compiler_params=_cparams(("arbitrary",)),
        name="mod",
    )(c, ada_w, ada_b.reshape(1, cols))


def _inproj_kernel(x_ref, g_ref, scale_ref, shift_ref, wr_ref, wa_ref, wg_ref,
                   pr_ref, pg_ref, a0_ref, a1_ref, a2_ref, h_ref):
    x = x_ref[...]
    tm = x.shape[0]
    ms = jnp.mean(x * x, axis=-1, keepdims=True)
    h = x * lax.rsqrt(ms + NORM_EPS) * (g_ref[...] * (1.0 + scale_ref[0])) + shift_ref[0]
    ntile = h_ref.shape[0]
    for c in range(ntile):
        h_ref[c] = h[:, c * LANES:(c + 1) * LANES]
    hb = h.astype(BF16)
    step = 512
    gcols = 3 * GROUP_WIDTH
    for w_ref, o_ref, cols in ((wr_ref, pr_ref, RWKV_COLS), (wg_ref, pg_ref, GATE_COLS),
                               (wa_ref, a0_ref, gcols)):
        for j in range(0, cols, step):
            width = min(step, cols - j)
            o_ref[:, j:j + width] = _dot(hb, w_ref[:, j:j + width]).astype(BF16)
    for gi, a_ref in ((1, a1_ref), (2, a2_ref)):
        dil = ATTN_GROUPS[gi][1]
        run = tm // dil
        hp = jnp.concatenate(
            [jnp.concatenate([h_ref[c, pl.ds(z, run, stride=dil), :] for z in range(dil)], axis=0)
             for c in range(ntile)], axis=1).astype(BF16)
        res = _dot(hp, wa_ref[:, gi * gcols:(gi + 1) * gcols]).astype(BF16)
        for z in range(dil):
            a_ref[z] = res[z * run:(z + 1) * run]


def _inproj(x2, norm_g, scale, shift, w_r, w_a, w_g, bsz, seq):
    n, d = x2.shape
    tm = TOKEN_TILE
    per = seq // tm
    gcols = 3 * GROUP_WIDTH
    const = lambda shape: pl.BlockSpec(shape, lambda i: (0, 0), pipeline_mode=pl.Buffered(1))
    row = lambda width: pl.BlockSpec((tm, width), lambda i: (i, 0))
    dil1, dil2 = ATTN_GROUPS[1][1], ATTN_GROUPS[2][1]
    res_spec = lambda dil: pl.BlockSpec((None, dil, None, tm // dil, gcols),
                                        lambda i: (i // per, 0, i % per, 0, 0))
    res_shape = lambda dil: jax.ShapeDtypeStruct((bsz, dil, per, tm // dil, gcols), BF16)
    pr, pg, a0, a1, a2 = pl.pallas_call(
        _inproj_kernel,
        grid=(n // tm,),
        in_specs=[row(d),
                  pl.BlockSpec((1, d), lambda i: (0, 0)),
                  pl.BlockSpec((1, 1, d), lambda i: (i // per, 0, 0)),
                  pl.BlockSpec((1, 1, d), lambda i: (i // per, 0, 0)),
                  const(w_r.shape), const(w_a.shape), const(w_g.shape)],
        out_specs=[row(RWKV_COLS), row(GATE_COLS), row(gcols), res_spec(dil1), res_spec(dil2)],
        out_shape=[jax.ShapeDtypeStruct((n, RWKV_COLS), BF16),
                   jax.ShapeDtypeStruct((n, GATE_COLS), BF16),
                   jax.ShapeDtypeStruct((n, gcols), BF16),
                   res_shape(dil1), res_shape(dil2)],
        scratch_shapes=[pltpu.VMEM((d // LANES, tm, LANES), F32)],
        compiler_params=_cparams(("parallel",)),
        name="inproj",
    )(x2, norm_g, scale, shift, w_r, w_a, w_g)
    qkv = [a.reshape(bsz, seq, gcols) for a in (a0, a1, a2)]
    return pr, pg, qkv


_P_W0, _P_A0, _P_KK, _P_KA, _P_RK, _P_LNW, _P_LNB, _P_MUR, _P_MUK, _P_MUV = range(10)


def _rwkv_kernel(pr_ref, par_ref, wl_ref, ones_ref, o_ref, state_ref, prev_ref, y_ref):
    c_idx = pl.program_id(1)
    nb = pr_ref.shape[0]
    rows_all = nb * CHUNK

    @pl.when(c_idx == 0)
    def _():
        state_ref[...] = jnp.zeros_like(state_ref)
        prev_ref[...] = jnp.zeros_like(prev_ref)

    rw = RWKV_WIDTH
    par = lambda i: par_ref[i:i + 1, :]
    ones_bd = ones_ref[...]

    def headsum(a):
        hi, lo = _split2(a)
        return _dot(hi, ones_bd) + _dot(lo, ones_bd)

    pr = jnp.concatenate([pr_ref[b].astype(F32) for b in range(nb)], axis=0)
    row = lax.broadcasted_iota(jnp.int32, (rows_all, 1), 0)
    first = row % CHUNK == 0

    def shifted(lo, hi):
        prev_row = prev_ref[0:1, lo:hi]
        for b in range(1, nb):
            prev_row = jnp.where(row >= b * CHUNK, prev_ref[b:b + 1, lo:hi], prev_row)
        return jnp.where(first, prev_row, pltpu.roll(pr[:, lo:hi], 1, 0))

    sh_rkv = shifted(0, 3 * rw)
    sh_lora = shifted(3 * rw + LORA_WIDTH, 3 * rw + 2 * LORA_WIDTH)
    for b in range(nb):
        prev_ref[b:b + 1, :] = pr[(b + 1) * CHUNK - 1:(b + 1) * CHUNK, :]

    lora = pr[:, 3 * rw:3 * rw + LORA_WIDTH] + sh_lora
    lane = lax.broadcasted_iota(jnp.int32, (1, LORA_WIDTH), 1)
    act = jnp.where(lane < DECAY_LORA, jnp.tanh(lora),
                    jnp.where(lane < DECAY_LORA + ICLR_LORA, lora, _sigmoid(lora)))
    lo_out = _dot(act.astype(BF16), wl_ref[...])
    w_pre = par(_P_W0) + lo_out[:, :rw]
    z = -w_pre
    softplus = jnp.maximum(z, 0.0) + jnp.log(1.0 + jnp.exp(-jnp.abs(z)))
    logd = -jnp.exp(-softplus - 0.5)
    a = _sigmoid(par(_P_A0) + lo_out[:, rw:2 * rw])
    g = lo_out[:, 2 * rw:3 * rw]

    r0, k0, v0 = pr[:, :rw], pr[:, rw:2 * rw], pr[:, 2 * rw:3 * rw]
    r = r0 + (sh_rkv[:, :rw] - r0) * par(_P_MUR)
    k = k0 + (sh_rkv[:, rw:2 * rw] - k0) * par(_P_MUK)
    v = v0 + (sh_rkv[:, 2 * rw:] - v0) * par(_P_MUV)

    kk = k * par(_P_KK)
    kk = kk / jnp.maximum(jnp.sqrt(headsum(kk * kk)), 1e-12)
    k2 = k * (1.0 + (a - 1.0) * par(_P_KA))

    ri = lax.broadcasted_iota(jnp.int32, (rows_all, rows_all), 0)
    rj = lax.broadcasted_iota(jnp.int32, (rows_all, rows_all), 1)
    tril = jnp.where(jnp.logical_and(ri >= rj, ri // CHUNK == rj // CHUNK), 1.0, 0.0).astype(BF16)
    ti = lax.broadcasted_iota(jnp.int32, (CHUNK, CHUNK), 0)
    tj = lax.broadcasted_iota(jnp.int32, (CHUNK, CHUNK), 1)
    low_incl = ti >= tj
    low_strict = ti > tj
    d_hi, d_mid, d_lo = _split3(logd)
    cum = _dot(tril, d_hi) + _dot(tril, d_mid) + _dot(tril, d_lo)
    p_incl = jnp.exp(cum)
    p_excl = jnp.exp(cum - logd)
    p_inv = jnp.exp(-cum)

    a_t = (-kk * p_excl).astype(BF16)
    b_t = (kk * a * p_inv).astype(BF16)
    k_t = (k2 * p_inv).astype(BF16)
    r_t = (r * p_incl).astype(BF16)
    v_b = v.astype(BF16)
    eye = jnp.where(ti == tj, 1.0, 0.0)

    chains = [(b, h) for b in range(nb) for h in range(RWKV_HEADS)]
    ids = range(len(chains))
    rs = [slice(b * CHUNK, (b + 1) * CHUNK) for b, _ in chains]
    ls = [slice(h * HEAD_DIM, (h + 1) * HEAD_DIM) for _, h in chains]
    ar = [jnp.concatenate([a_t[rs[i], ls[i]], r_t[rs[i], ls[i]]], axis=0) for i in ids]
    bk = [jnp.concatenate([b_t[rs[i], ls[i]], k_t[rs[i], ls[i]]], axis=0) for i in ids]
    vh = [v_b[rs[i], ls[i]] for i in ids]
    s0 = [state_ref[b, h] for b, h in chains]
    m = [_dot_nt(ar[i], bk[i]) for i in ids]
    ars = [_dot_nt(ar[i], s0[i].astype(BF16)) for i in ids]
    a_ab = [jnp.where(low_strict, m[i][:CHUNK, :CHUNK], 0.0) for i in ids]
    a_ak = [jnp.where(low_strict, m[i][:CHUNK, CHUNK:], 0.0).astype(BF16) for i in ids]
    low2 = (lax.broadcasted_iota(jnp.int32, (CHUNK, 2 * CHUNK), 0)
            >= lax.broadcasted_iota(jnp.int32, (CHUNK, 2 * CHUNK), 1) % CHUNK)
    a_r = [jnp.where(low2, m[i][CHUNK:, :], 0.0).astype(BF16) for i in ids]
    t_inv = [eye + a_ab[i] for i in ids]
    pw = [a_ab[i].astype(BF16) for i in ids]
    for _ in range(int(math.log2(CHUNK)) - 1):
        pw = [_dot(pw[i], pw[i]).astype(BF16) for i in ids]
        t_inv = [t_inv[i] + _dot(pw[i], t_inv[i].astype(BF16)) for i in ids]
    rhs = [ars[i][:CHUNK] + _dot(a_ak[i], vh[i]) for i in ids]
    ub = [_dot(t_inv[i].astype(BF16), rhs[i].astype(BF16)).astype(BF16) for i in ids]
    uv = [jnp.concatenate([ub[i], vh[i]], axis=0) for i in ids]
    for i in ids:
        y_ref[rs[i], ls[i]] = ars[i][CHUNK:] + _dot(a_r[i], uv[i])
    for i, (b, h) in enumerate(chains):
        last = (b + 1) * CHUNK - 1
        state_ref[b, h] = (s0[i] + _dot_tn(uv[i], bk[i])) * p_incl[last:last + 1, ls[i]]

    y = y_ref[...]
    inv_n = 1.0 / HEAD_DIM
    mu = headsum(y) * inv_n
    dlt = y - mu
    var = headsum(dlt * dlt) * inv_n
    yn = dlt * lax.rsqrt(var + GROUPNORM_EPS) * par(_P_LNW) + par(_P_LNB)
    bonus = headsum(r * k2 * par(_P_RK)) * v
    out = ((yn + bonus) * g).astype(BF16)
    for b in range(nb):
        o_ref[b] = out[b * CHUNK:(b + 1) * CHUNK]


def _rwkv(pr, par, w_lora, ones_bd, bsz, seq):
    n = pr.shape[0]
    nchunk = seq // CHUNK
    nb = RWKV_BATCH if bsz % RWKV_BATCH == 0 else 1
    const = lambda shape: pl.BlockSpec(shape, lambda b, c: (0, 0))
    out = pl.pallas_call(
        _rwkv_kernel,
        grid=(bsz // nb, nchunk),
        in_specs=[pl.BlockSpec((nb, CHUNK, RWKV_COLS), lambda b, c: (b, c, 0)),
                  const(par.shape), const(w_lora.shape), const(ones_bd.shape)],
        out_specs=pl.BlockSpec((nb, CHUNK, RWKV_WIDTH), lambda b, c: (b, c, 0)),
        out_shape=jax.ShapeDtypeStruct((bsz, seq, RWKV_WIDTH), BF16),
        scratch_shapes=[pltpu.VMEM((nb, RWKV_HEADS, HEAD_DIM, HEAD_DIM), F32),
                        pltpu.VMEM((8, RWKV_COLS), F32),
                        pltpu.VMEM((nb * CHUNK, RWKV_WIDTH), F32)],
        compiler_params=_cparams(("parallel", "arbitrary")),
        name="rwkv",
    )(pr.reshape(bsz, seq, RWKV_COLS), par, w_lora, ones_bd)
    return out.reshape(n, RWKV_WIDTH)


def _attn_kernel(a0_ref, a1_ref, a2_ref, bias_ref, gain_ref, bd_ref, out_ref,
                 qkv_ref, m_ref, s_ref, o_ref):
    seq = a0_ref.shape[0]
    ntile = m_ref.shape[0]
    gw = GROUP_WIDTH
    blk = ATTN_BLOCK
    bd = bd_ref[...]
    gq = gain_ref[0:1, :] * (HEAD_DIM ** -0.5)
    gk = gain_ref[1:2, :]
    lane_head = lax.broadcasted_iota(jnp.int32, (1, gw), 1) // HEAD_DIM
    qkv_ref[0:blk, :] = jnp.zeros((blk, 3 * gw), BF16)

    def headwise(parts):
        acc = parts[0]
        for h in range(1, HEADS_PER_GROUP):
            acc = jnp.where(lane_head == h, parts[h], acc)
        return acc

    for gi, a_ref in enumerate((a0_ref, a1_ref, a2_ref)):
        dil = ATTN_GROUPS[gi][1]
        nb = seq // dil // blk

        def normalise(c, carry, a_ref=a_ref):
            r0 = pl.multiple_of(c * 256, 256)
            for part, gain in ((0, gq), (1, gk)):
                t = a_ref[pl.ds(r0, 256), part * gw:(part + 1) * gw].astype(F32)
                hi, lo = _split2(t * t)
                ms = _dot(hi, bd) + _dot(lo, bd)
                qkv_ref[pl.ds(blk + r0, 256), part * gw:(part + 1) * gw] = (
                    t * lax.rsqrt(ms + NORM_EPS) * gain).astype(BF16)
            qkv_ref[pl.ds(blk + r0, 256), 2 * gw:] = a_ref[pl.ds(r0, 256), 2 * gw:]
            return carry

        lax.fori_loop(0, seq // 256, normalise, 0)

        def blocks(it, carry, gi=gi, dil=dil, nb=nb):
            ts = [it * ATTN_UNROLL + u for u in range(ATTN_UNROLL)]
            r0 = [pl.multiple_of(t * blk, blk) for t in ts]
            first = [jnp.where(t % nb == 0, 1, 0) for t in ts]
            q = [qkv_ref[pl.ds(blk + r, blk), 0:gw] for r in r0]
            kwin = [qkv_ref[pl.ds(r, 2 * blk), gw:2 * gw] for r in r0]
            vwin = [qkv_ref[pl.ds(r, 2 * blk), 2 * gw:] for r in r0]
            pairs = [(u, h) for u in range(ATTN_UNROLL) for h in range(HEADS_PER_GROUP)]
            qm = [jnp.where(lane_head == h, q[u], jnp.zeros_like(q[u])) for u, h in pairs]
            s = [_dot_nt(qm[i], kwin[u]) + bias_ref[gi, first[u], h]
                 for i, (u, h) in enumerate(pairs)]
            m = [jnp.max(si, axis=-1, keepdims=True) for si in s]
            p = [jnp.exp(si - mi) for si, mi in zip(s, m)]
            den = [jnp.sum(pi, axis=-1, keepdims=True) for pi in p]
            pv = [_dot(p[i].astype(BF16), vwin[u]) for i, (u, h) in enumerate(pairs)]
            for u, t in enumerate(ts):
                sel = slice(u * HEADS_PER_GROUP, (u + 1) * HEADS_PER_GROUP)
                o_new = headwise(pv[sel]) / headwise(
                    [jnp.broadcast_to(d, (blk, gw)) for d in den[sel]])
                lse = headwise([jnp.broadcast_to(mi + jnp.log(d), (blk, gw))
                                for mi, d in zip(m[sel], den[sel])])
                z = t // nb
                n = t % nb
                rows = pl.ds(z + n * (blk * dil), blk, stride=dil) if dil > 1 else pl.ds(r0[u], blk)
                ld = lambda ref: jnp.concatenate([ref[c, rows, :] for c in range(ntile)], axis=1)

                def st(ref, val):
                    for c in range(ntile):
                        ref[c, rows, :] = val[:, c * LANES:(c + 1) * LANES]

                if gi == 0:
                    st(m_ref, lse)
                    st(o_ref, o_new)
                else:
                    m_old = ld(m_ref)
                    m_new = jnp.maximum(m_old, lse)
                    wa = jnp.exp(m_old - m_new)
                    wb = jnp.exp(lse - m_new)
                    mix = wa * ld(o_ref) + wb * o_new
                    if gi == 1:
                        st(m_ref, m_new)
                        st(s_ref, wa + wb)
                        st(o_ref, mix)
                    else:
                        st(o_ref, mix / (wa * ld(s_ref) + wb))
            return carry

        lax.fori_loop(0, seq // blk // ATTN_UNROLL, blocks, 0)

    out_ref[...] = jnp.concatenate([o_ref[c] for c in range(ntile)], axis=1).astype(out_ref.dtype)


def _attention(qkv, bias, gains, bd, bsz, seq):
    gcols = 3 * GROUP_WIDTH
    const = lambda a: pl.BlockSpec(a.shape, lambda b: (0,) * a.ndim)
    seq_spec = pl.BlockSpec((None, seq, gcols), lambda b: (b, 0, 0))
    out = pl.pallas_call(
        _attn_kernel,
        grid=(bsz,),
        in_specs=[seq_spec, seq_spec, seq_spec, const(bias), const(gains), const(bd)],
        out_specs=pl.BlockSpec((None, seq, GROUP_WIDTH), lambda b: (b, 0, 0)),
        out_shape=jax.ShapeDtypeStruct((bsz, seq, GROUP_WIDTH), BF16),
        scratch_shapes=[pltpu.VMEM((ATTN_BLOCK + seq, gcols), BF16)]
        + [pltpu.VMEM((GROUP_WIDTH // LANES, seq, LANES), F32)] * 3,
        compiler_params=_cparams(("parallel",)),
        name="attn",
    )(*qkv, bias, gains, bd)
    return out.reshape(bsz * seq, GROUP_WIDTH)


def _t5_bucket(dist):
    max_exact = N_BUCKETS // 2
    large = max_exact + (jnp.log(jnp.maximum(dist, max_exact).astype(F32) / max_exact)
                         / math.log(MAX_DISTANCE / max_exact) * (N_BUCKETS - max_exact)).astype(jnp.int32)
    return jnp.where(dist < max_exact, dist, jnp.minimum(large, N_BUCKETS - 1))


def _band_bias(rel_bias, gi):
    window, dil = ATTN_GROUPS[gi]
    qi = jnp.arange(ATTN_BLOCK)[:, None]
    kj = jnp.arange(2 * ATTN_BLOCK)[None, :]
    steps = qi - kj + ATTN_BLOCK
    heads = slice(gi * HEADS_PER_GROUP, (gi + 1) * HEADS_PER_GROUP)
    bucket = _t5_bucket(jnp.maximum(steps, 0) * dil)
    onehot = (bucket[None] == jnp.arange(N_BUCKETS)[:, None, None]).astype(F32)
    bias = jnp.sum(onehot[:, None] * rel_bias[:, heads][:, :, None, None], axis=0)
    band = (steps >= 0) & (steps <= window // dil)
    later = jnp.where(band[None], bias.astype(F32), NEG_INF)
    first = jnp.where((kj >= ATTN_BLOCK)[None], later, NEG_INF)
    return jnp.stack([later, first])


def _post_kernel(x_ref, yr_ref, att_ref, pg_ref,
                 gate1_ref, scale2_ref, shift2_ref, g2_ref, wbr_ref, wba_ref, wout_ref,
                 rwh_ref, rwl_ref, rb_ref,
                 x1_ref, h2_ref, gates_ref, eidx_ref, rank_ref, counts_ref, base_ref, run_ref):
    i = pl.program_id(0)
    tm = x_ref.shape[0]

    @pl.when(i == 0)
    def _():
        run_ref[...] = jnp.zeros_like(run_ref)

    y_r = _dot(yr_ref[...], wbr_ref[...])
    y_a = _dot(att_ref[...], wba_ref[...])
    d = D_MODEL
    mixed = (_sigmoid(pg_ref[:, :d].astype(F32)) * y_r
             + _sigmoid(pg_ref[:, d:].astype(F32)) * y_a)
    x1 = x_ref[...] + gate1_ref[0] * _dot(mixed.astype(BF16), wout_ref[...])
    x1_ref[...] = x1

    ms = jnp.mean(x1 * x1, axis=-1, keepdims=True)
    h2 = x1 * lax.rsqrt(ms + NORM_EPS) * (g2_ref[...] * (1.0 + scale2_ref[0])) + shift2_ref[0]
    h2_ref[...] = h2.astype(h2_ref.dtype)

    h_hi, h_lo = _split2(h2)
    rw_hi, rw_lo = rwh_ref[...], rwl_ref[...]
    logits = (_dot_nt(rw_hi, h_hi) + _dot_nt(rw_hi, h_lo) + _dot_nt(rw_lo, h_hi)
              + rb_ref[:, 0:1])
    eid = lax.broadcasted_iota(jnp.int32, (N_EXPERTS, tm), 0)
    vals, hots = [], []
    lg = logits
    for k in range(TOP_K):
        m = jnp.max(lg, axis=0, keepdims=True)
        idx = jnp.min(jnp.where(lg == m, eid, N_EXPERTS), axis=0, keepdims=True)
        hot = eid == idx
        vals.append(m)
        hots.append(hot)
        eidx_ref[k:k + 1, :] = idx
        lg = jnp.where(hot, -jnp.inf, lg)
    exps = [jnp.exp(v - vals[0]) for v in vals]
    tot = exps[0] + exps[1] + exps[2] + exps[3]
    for k in range(TOP_K):
        gates_ref[k:k + 1, :] = exps[k] / tot

    chosen = jnp.zeros((N_EXPERTS, tm), F32)
    for hot in hots:
        chosen = chosen + jnp.where(hot, 1.0, 0.0)
    ti = lax.broadcasted_iota(jnp.int32, (tm, tm), 0)
    tj = lax.broadcasted_iota(jnp.int32, (tm, tm), 1)
    before = jnp.where(ti < tj, 1.0, 0.0).astype(BF16)
    run = run_ref[:, 0:1]
    base = _dot(chosen.astype(BF16), before) + run
    for k in range(TOP_K):
        rank_ref[k:k + 1, :] = jnp.sum(jnp.where(hots[k], base, 0.0), axis=0,
                                       keepdims=True).astype(jnp.int32)
    run_new = run + jnp.sum(chosen, axis=1, keepdims=True)
    run_ref[...] = jnp.broadcast_to(run_new, run_ref.shape)
    counts_ref[...] = jnp.broadcast_to(run_new, counts_ref.shape).astype(jnp.int32)
    lane = lax.broadcasted_iota(jnp.int32, base_ref.shape, 1)
    starts = jnp.zeros(base_ref.shape, F32)
    for j in range(tm // ROUTE_TILE):
        starts = jnp.where(lane == j, base[:, j * ROUTE_TILE:j * ROUTE_TILE + 1], starts)
    base_ref[...] = starts.astype(jnp.int32)


def _post(x2, y_rwkv, att, pg, gate1, scale2, shift2, norm2_g, w_br_r, w_br_a, w_out,
          rw_hi, rw_lo, rb, seq):
    n, d = x2.shape
    tm = TOKEN_TILE
    per = seq // tm
    row = lambda width: pl.BlockSpec((tm, width), lambda i: (i, 0))
    const = lambda shape: pl.BlockSpec(shape, lambda i: (0,) * len(shape))
    bvec = pl.BlockSpec((1, 1, d), lambda i: (i // per, 0, 0))
    tcol = pl.BlockSpec((TOP_K, tm), lambda i: (0, i))
    return pl.pallas_call(
        _post_kernel,
        grid=(n // tm,),
        in_specs=[row(d), row(RWKV_WIDTH), row(GROUP_WIDTH), row(GATE_COLS),
                  bvec, bvec, bvec, const((1, d)), const(w_br_r.shape), const(w_br_a.shape),
                  const(w_out.shape), const(rw_hi.shape), const(rw_lo.shape), const(rb.shape)],
        out_specs=[row(d), row(d), tcol, tcol, tcol, const((N_EXPERTS, LANES)),
                   pl.BlockSpec((None, N_EXPERTS, LANES), lambda i: (i, 0, 0))],
        out_shape=[jax.ShapeDtypeStruct((n, d), F32),
                   jax.ShapeDtypeStruct((n, d), BF16),
                   jax.ShapeDtypeStruct((TOP_K, n), F32),
                   jax.ShapeDtypeStruct((TOP_K, n), jnp.int32),
                   jax.ShapeDtypeStruct((TOP_K, n), jnp.int32),
                   jax.ShapeDtypeStruct((N_EXPERTS, LANES), jnp.int32),
                   jax.ShapeDtypeStruct((n // tm, N_EXPERTS, LANES), jnp.int32)],
        scratch_shapes=[pltpu.VMEM((N_EXPERTS, 128), F32)],
        compiler_params=_cparams(("arbitrary",)),
        name="post",
    )(x2, y_rwkv, att, pg, gate1, scale2, shift2, norm2_g, w_br_r, w_br_a, w_out,
      rw_hi, rw_lo, rb)


def _pack_halves(x):
    w = x.shape[1] // 2
    lo = lax.bitcast_convert_type(x[:, :w], jnp.uint32)
    hi = lax.bitcast_convert_type(x[:, w:], jnp.uint32)
    return (hi & jnp.uint32(0xFFFF0000)) | (lo >> 16)


def _unpack_halves(u):
    lo = lax.bitcast_convert_type(u << 16, F32).astype(BF16)
    hi = lax.bitcast_convert_type(u & jnp.uint32(0xFFFF0000), F32).astype(BF16)
    return lo, hi


def _slab_rows(eidx_ref, rank_ref, base_ref, sub):
    tt = eidx_ref.shape[1]
    eid = lax.broadcasted_iota(jnp.int32, (N_EXPERTS, tt), 0)
    which = sub % (TOKEN_TILE // ROUTE_TILE)
    basevec = base_ref[:, 0:1]
    for j in range(1, TOKEN_TILE // ROUTE_TILE):
        basevec = jnp.where(which == j, base_ref[:, j:j + 1], basevec)
    lead = basevec & (ROW_ALIGN - 1)
    rows = jnp.full((N_EXPERTS, tt), -1, jnp.int32)
    for k in range(TOP_K):
        rows = jnp.where(eidx_ref[k:k + 1, :] == eid, rank_ref[k:k + 1, :] - (basevec - lead), rows)
    return rows, lead


def _dispatch_kernel(sbase_ref, scnt_ref, pstart_ref, nx_ref, xlist_ref, fill_ref,
                     eidx_ref, rank_ref, base_ref, h2_ref, xs_ref,
                     stage_ref, xstage_ref, extra_ref, zero_ref, carry_ref, rows_ref, sem, xsem):
    s = pl.program_id(0)
    nblk = xs_ref.shape[0] // MOE_ROWS

    @pl.when(s == 0)
    def _():
        zero_ref[...] = jnp.zeros_like(zero_ref)

        def fill(j, carry):
            @pl.when(fill_ref[j] != 0)
            def _():
                row0 = pl.multiple_of(j * MOE_ROWS, MOE_ROWS)
                cp = pltpu.make_async_copy(zero_ref, xs_ref.at[pl.ds(row0, MOE_ROWS)], xsem)
                cp.start()
                cp.wait()
            return carry

        lax.fori_loop(0, nblk, fill, 0)

    @pl.when(s == 0)
    def _():
        carry_ref[...] = jnp.zeros_like(carry_ref)

    rows, leadvec = _slab_rows(eidx_ref, rank_ref, base_ref, s)
    h2 = h2_ref[...]
    rid = lax.broadcasted_iota(jnp.int32, (SLAB, 1), 0)
    align = ROW_ALIGN

    def onehot(row, chunk):
        return jnp.where(row - chunk * SLAB == rid, 1.0, 0.0).astype(BF16)

    def lead(e):
        return sbase_ref[s * N_EXPERTS + e] & (align - 1)

    def used(e):
        return lead(e) + scnt_ref[s * N_EXPERTS + e]

    def window_start(e):
        return pl.multiple_of(pstart_ref[e] + sbase_ref[s * N_EXPERTS + e] - lead(e), align)

    total = leadvec + jnp.sum(jnp.where(rows >= 0, 1, 0), axis=1, keepdims=True)
    group0 = total - (total & (align - 1))
    gid = lax.broadcasted_iota(jnp.int32, (align, 1), 0)
    slot = s % 2
    for g0 in range(0, N_EXPERTS, EXPERT_GROUP):
        sel = jnp.concatenate([onehot(rows[e:e + 1, :], 0)
                               for e in range(g0, g0 + EXPERT_GROUP)], axis=0)
        packed = _pack_halves(_dot(sel, h2))
        for e in range(g0, g0 + EXPERT_GROUP):
            stage_ref[slot, e] = packed[(e - g0) * SLAB:(e - g0 + 1) * SLAB]
            stage_ref[slot, e, 0:align, :] = stage_ref[slot, e, 0:align, :] | carry_ref[e]
    tails = jnp.concatenate(
        [jnp.where(rows[e:e + 1, :] - group0[e:e + 1, :] == gid, 1.0, 0.0).astype(BF16)
         for e in range(N_EXPERTS)], axis=0)
    tail_rows = _pack_halves(_dot(tails, h2))
    for e in range(N_EXPERTS):
        keep = used(e) < align
        new = tail_rows[e * align:(e + 1) * align]
        carry_ref[e] = jnp.where(keep, new | carry_ref[e], new)

    rows_ref[...] = rows

    def extra(i):
        cand = xlist_ref[s * EXTRA_MAX + i]
        e = cand // EXTRA_PER
        c = cand % EXTRA_PER + 1
        first = pstart_ref[e] + sbase_ref[s * N_EXPERTS + e]
        return e, c, pl.multiple_of(first - (first & (align - 1)) + c * SLAB, align)

    sels = []
    for i in range(EXTRA_BATCH):
        e, c, _ = extra(i)
        hit = jnp.logical_and(rows_ref[pl.ds(e, 1), :] - c * SLAB == rid, i < nx_ref[s])
        sels.append(jnp.where(hit, 1.0, 0.0).astype(BF16))
    xpacked = _pack_halves(_dot(jnp.concatenate(sels, axis=0), h2))
    for i in range(EXTRA_BATCH):
        xstage_ref[slot, i] = xpacked[i * SLAB:(i + 1) * SLAB]

    @pl.when(s > 0)
    def _():
        for e in range(N_EXPERTS):
            pltpu.make_async_copy(stage_ref.at[1 - slot, e], xs_ref.at[pl.ds(0, SLAB)],
                                  sem.at[1 - slot]).wait()
        for i in range(EXTRA_BATCH):
            @pl.when(i < nx_ref[s - 1])
            def _():
                pltpu.make_async_copy(xstage_ref.at[1 - slot, i], xs_ref.at[pl.ds(0, SLAB)],
                                      sem.at[1 - slot]).wait()

    slab_copy = lambda e: pltpu.make_async_copy(
        stage_ref.at[slot, e], xs_ref.at[pl.ds(window_start(e), SLAB)], sem.at[slot])
    xslab_copy = lambda i: pltpu.make_async_copy(
        xstage_ref.at[slot, i], xs_ref.at[pl.ds(extra(i)[2], SLAB)], sem.at[slot])
    for e in range(N_EXPERTS):
        slab_copy(e).start()
    for i in range(EXTRA_BATCH):
        @pl.when(i < nx_ref[s])
        def _():
            xslab_copy(i).start()

    def more(i, carry):
        e, c, first = extra(i)
        extra_ref[...] = _pack_halves(_dot(onehot(rows_ref[pl.ds(e, 1), :], c), h2))
        cp = pltpu.make_async_copy(extra_ref, xs_ref.at[pl.ds(first, SLAB)], xsem)
        cp.start()
        cp.wait()
        return carry

    lax.fori_loop(EXTRA_BATCH, nx_ref[s], more, 0)

    @pl.when(s == pl.num_programs(0) - 1)
    def _():
        for e in range(N_EXPERTS):
            slab_copy(e).wait()
        for i in range(EXTRA_BATCH):
            @pl.when(i < nx_ref[s])
            def _():
                xslab_copy(i).wait()


def _dispatch(route, fill, eidx, rank, base, h2, nblk):
    n, d = h2.shape
    tt = ROUTE_TILE
    subs = TOKEN_TILE // tt
    tcol = pl.BlockSpec((TOP_K, tt), lambda s, *_: (0, s))
    return pl.pallas_call(
        _dispatch_kernel,
        grid_spec=pltpu.PrefetchScalarGridSpec(
            num_scalar_prefetch=6,
            grid=(n // tt,),
            in_specs=[tcol, tcol,
                      pl.BlockSpec((None, N_EXPERTS, LANES), lambda s, *_: (s // subs, 0, 0)),
                      pl.BlockSpec((tt, d), lambda s, *_: (s, 0))],
            out_specs=pl.BlockSpec(memory_space=pl.ANY),
            scratch_shapes=[pltpu.VMEM((2, N_EXPERTS, SLAB, d // 2), jnp.uint32),
                            pltpu.VMEM((2, EXTRA_BATCH, SLAB, d // 2), jnp.uint32),
                            pltpu.VMEM((SLAB, d // 2), jnp.uint32),
                            pltpu.VMEM((MOE_ROWS, d // 2), jnp.uint32),
                            pltpu.VMEM((N_EXPERTS, ROW_ALIGN, d // 2), jnp.uint32),
                            pltpu.VMEM((N_EXPERTS, tt), jnp.int32),
                            pltpu.SemaphoreType.DMA((2,)), pltpu.SemaphoreType.DMA(())]),
        out_shape=jax.ShapeDtypeStruct((nblk * MOE_ROWS, d // 2), jnp.uint32),
        compiler_params=_cparams(("arbitrary",)),
        name="dispatch",
    )(*route, fill, eidx, rank, base, h2)


def _expert_kernel(blk_ref, src_ref, valid_ref, xs_ref, w1g_ref, w1l_ref, b1g_ref, b1l_ref,
                   w2_ref, b2_ref, o_ref):
    j = pl.program_id(0)

    @pl.when(valid_ref[j] != 0)
    def _():
        lo, hi = _unpack_halves(xs_ref[...])
        half = lo.shape[1]

        def up(w_ref, b_ref):
            return _dot(lo, w_ref[:half, :]) + _dot(hi, w_ref[half:, :]) + b_ref[...]

        glu = jnp.minimum(up(w1g_ref, b1g_ref), SWIGLU_LIMIT)
        lin = jnp.clip(up(w1l_ref, b1l_ref), -SWIGLU_LIMIT, SWIGLU_LIMIT)
        act = glu * _sigmoid(SWIGLU_ALPHA * glu) * (lin + 1.0)
        o = _dot(act.astype(BF16), w2_ref[...]) + b2_ref[...]
        o_ref[...] = _pack_halves(o.astype(BF16).astype(F32))

    @pl.when(valid_ref[j] == 0)
    def _():
        o_ref[...] = jnp.zeros_like(o_ref)


def _experts(blk_e, src, valid, xs, w1g, w1l, b1g, b1l, w2, b2):
    rows, half = xs.shape
    nblk = rows // MOE_ROWS
    d, de = w1g.shape[1], w1g.shape[2]
    emap = lambda j, be, src, valid: (be[j], 0, 0)
    return pl.pallas_call(
        _expert_kernel,
        grid_spec=pltpu.PrefetchScalarGridSpec(
            num_scalar_prefetch=3,
            grid=(nblk,),
            in_specs=[pl.BlockSpec((MOE_ROWS, half), lambda j, be, src, valid: (src[j], 0)),
                      pl.BlockSpec((None, d, de), emap), pl.BlockSpec((None, d, de), emap),
                      pl.BlockSpec((None, 1, de), emap), pl.BlockSpec((None, 1, de), emap),
                      pl.BlockSpec((None, de, d), emap), pl.BlockSpec((None, 1, d), emap)],
            out_specs=pl.BlockSpec((MOE_ROWS, half), lambda j, be, src, valid: (j, 0))),
        out_shape=jax.ShapeDtypeStruct((rows, half), jnp.uint32),
        compiler_params=_cparams(("arbitrary",)),
        name="experts",
    )(blk_e, src, valid, xs, w1g, w1l, b1g, b1l, w2, b2)


def _split_w1_kernel(w_ref, perm_ref, g_ref, l_ref):
    t = MXU_TILE
    for cb in range(w_ref.shape[1] // t):
        res = _dot(w_ref[:, cb * t:(cb + 1) * t].astype(BF16), perm_ref[...])
        g_ref[:, cb * (t // 2):(cb + 1) * (t // 2)] = res[:, :t // 2].astype(BF16)
        l_ref[:, cb * (t // 2):(cb + 1) * (t // 2)] = res[:, t // 2:].astype(BF16)


def _split_w1(w1):
    ne, d, two_de = w1.shape
    t = MXU_TILE
    src = jnp.arange(t)[:, None]
    dst = jnp.arange(t)[None, :]
    perm = (src == jnp.where(dst < t // 2, 2 * dst, 2 * (dst - t // 2) + 1)).astype(BF16)
    return pl.pallas_call(
        _split_w1_kernel,
        grid=(ne,),
        in_specs=[pl.BlockSpec((None, d, two_de), lambda e: (e, 0, 0)),
                  pl.BlockSpec((t, t), lambda e: (0, 0))],
        out_specs=[pl.BlockSpec((None, d, two_de // 2), lambda e: (e, 0, 0)),
                   pl.BlockSpec((None, d, two_de // 2), lambda e: (e, 0, 0))],
        out_shape=[jax.ShapeDtypeStruct((ne, d, two_de // 2), BF16),
                   jax.ShapeDtypeStruct((ne, d, two_de // 2), BF16)],
        compiler_params=_cparams(("parallel",)),
        name="split_w1",
    )(w1, perm)


def _combine_kernel(sbase_ref, scnt_ref, pstart_ref, nx_ref, xlist_ref,
                    eidx_ref, rank_ref, gates_ref, base_ref, x1_ref, gate2_ref, o_hbm, out_ref,
                    slab_ref, xslab_ref, extra_ref, acc_ref, rows_ref, ge_ref, sem, xsem):
    s = pl.program_id(0)
    last = pl.num_programs(0) - 1
    half = acc_ref.shape[1] // 2
    slot = s % 2

    def window_start(sub, e, chunk=0):
        first = pstart_ref[e] + sbase_ref[sub * N_EXPERTS + e]
        return pl.multiple_of(first - (first & (ROW_ALIGN - 1)) + chunk * SLAB, ROW_ALIGN)

    def slab_copy(sub, e, to):
        return pltpu.make_async_copy(o_hbm.at[pl.ds(window_start(sub, e), SLAB)],
                                     slab_ref.at[to, e], sem.at[to])

    def extra(sub, i):
        cand = xlist_ref[sub * EXTRA_MAX + i]
        return cand // EXTRA_PER, cand % EXTRA_PER + 1

    def xslab_copy(sub, i, to):
        e, c = extra(sub, i)
        return pltpu.make_async_copy(o_hbm.at[pl.ds(window_start(sub, e, c), SLAB)],
                                     xslab_ref.at[to, i], sem.at[to])

    def fetch(sub, to):
        for e in range(N_EXPERTS):
            slab_copy(sub, e, to).start()
        for i in range(EXTRA_BATCH):
            @pl.when(i < nx_ref[sub])
            def _():
                xslab_copy(sub, i, to).start()

    @pl.when(s == 0)
    def _():
        xslab_ref[...] = jnp.zeros_like(xslab_ref)
        fetch(0, 0)

    @pl.when(s < last)
    def _():
        fetch(s + 1, 1 - slot)

    rows, _ = _slab_rows(eidx_ref, rank_ref, base_ref, s)
    eid = lax.broadcasted_iota(jnp.int32, rows.shape, 0)
    ge = jnp.zeros(rows.shape, F32)
    for k in range(TOP_K):
        ge = jnp.where(eidx_ref[k:k + 1, :] == eid, gates_ref[k:k + 1, :], ge)
    rid = lax.broadcasted_iota(jnp.int32, (SLAB, 1), 0)

    def weights(row, gate, chunk):
        return jnp.where(row - chunk * SLAB == rid, gate, 0.0).astype(BF16)

    rows_ref[...] = rows
    ge_ref[...] = ge
    for e in range(N_EXPERTS):
        slab_copy(s, e, slot).wait()
    for i in range(EXTRA_BATCH):
        @pl.when(i < nx_ref[s])
        def _():
            xslab_copy(s, i, slot).wait()
    acc_lo = acc_hi = None
    for g0 in range(0, N_EXPERTS, EXPERT_GROUP):
        wt = jnp.concatenate([weights(rows[e:e + 1, :], ge[e:e + 1, :], 0)
                              for e in range(g0, g0 + EXPERT_GROUP)], axis=0)
        lo, hi = _unpack_halves(
            slab_ref[slot, g0:g0 + EXPERT_GROUP].reshape(EXPERT_GROUP * SLAB, half))
        p_lo, p_hi = _dot_tn(wt, lo), _dot_tn(wt, hi)
        acc_lo = p_lo if acc_lo is None else acc_lo + p_lo
        acc_hi = p_hi if acc_hi is None else acc_hi + p_hi
    wts = []
    for i in range(EXTRA_BATCH):
        e, c = extra(s, i)
        gate = jnp.where(i < nx_ref[s], ge_ref[pl.ds(e, 1), :], 0.0)
        wts.append(weights(rows_ref[pl.ds(e, 1), :], gate, c))
    lo, hi = _unpack_halves(xslab_ref[slot].reshape(EXTRA_BATCH * SLAB, half))
    wt = jnp.concatenate(wts, axis=0)
    acc_ref[:, :half] = acc_lo + _dot_tn(wt, lo)
    acc_ref[:, half:] = acc_hi + _dot_tn(wt, hi)

    def more(i, carry):
        e, c = extra(s, i)
        cp = pltpu.make_async_copy(o_hbm.at[pl.ds(window_start(s, e, c), SLAB)], extra_ref, xsem)
        cp.start()
        cp.wait()
        lo, hi = _unpack_halves(extra_ref[...])
        wt = weights(rows_ref[pl.ds(e, 1), :], ge_ref[pl.ds(e, 1), :], c)
        acc_ref[:, :half] += _dot_tn(wt, lo)
        acc_ref[:, half:] += _dot_tn(wt, hi)
        return carry

    lax.fori_loop(EXTRA_BATCH, nx_ref[s], more, 0)
    out_ref[...] = x1_ref[...] + gate2_ref[0] * acc_ref[...]


def _combine(route, eidx, rank, gates, base, x1, gate2, o_rows, seq):
    n, d = x1.shape
    tt = ROUTE_TILE
    subs = TOKEN_TILE // tt
    per = seq // tt
    tcol = pl.BlockSpec((TOP_K, tt), lambda s, *_: (0, s))
    return pl.pallas_call(
        _combine_kernel,
        grid_spec=pltpu.PrefetchScalarGridSpec(
            num_scalar_prefetch=5,
            grid=(n // tt,),
            in_specs=[tcol, tcol, tcol,
                      pl.BlockSpec((None, N_EXPERTS, LANES), lambda s, *_: (s // subs, 0, 0)),
                      pl.BlockSpec((tt, d), lambda s, *_: (s, 0)),
                      pl.BlockSpec((1, 1, d), lambda s, *_: (s // per, 0, 0)),
                      pl.BlockSpec(memory_space=pl.ANY)],
            out_specs=pl.BlockSpec((tt, d), lambda s, *_: (s, 0)),
            scratch_shapes=[pltpu.VMEM((2, N_EXPERTS, SLAB, d // 2), jnp.uint32),
                            pltpu.VMEM((2, EXTRA_BATCH, SLAB, d // 2), jnp.uint32),
                            pltpu.VMEM((SLAB, d // 2), jnp.uint32),
                            pltpu.VMEM((tt, d), F32),
                            pltpu.VMEM((N_EXPERTS, tt), jnp.int32),
                            pltpu.VMEM((N_EXPERTS, tt), F32),
                            pltpu.SemaphoreType.DMA((2,)), pltpu.SemaphoreType.DMA(())]),
        out_shape=jax.ShapeDtypeStruct((n, d), F32),
        compiler_params=_cparams(("arbitrary",)),
        name="combine",
    )(*route, eidx, rank, gates, base, x1, gate2, o_rows)


def _layer(x, c, ada_w, ada_b, norm1_g, norm2_g, w_in, mu_rkv, mu_wag, w0, w1, w2, a0, a1, a2,
           g1, g2, k_k, k_a, r_k, ln_w, ln_b, qn_g, kn_g, rel_bias, w_br_rwkv, w_br_attn, w_out,
           router_w, router_b, exp_w1, exp_b1, exp_w2, exp_b2):
    bsz, seq, d = x.shape
    n = bsz * seq
    rw = RWKV_WIDTH
    assert d == D_MODEL and seq % (ATTN_BLOCK * ATTN_GROUPS[-1][1]) == 0 and seq % TOKEN_TILE == 0

    mod = _modulation(c, ada_w, ada_b)
    shift1, scale1, gate1, shift2, scale2, gate2 = [
        m.reshape(bsz, 1, d) for m in jnp.split(mod, 6, axis=-1)]

    lora_w = jnp.concatenate([w1, a1, g1], axis=1)
    lora_mu = jnp.concatenate([jnp.broadcast_to(mu_wag[0][:, None], w1.shape),
                               jnp.broadcast_to(mu_wag[1][:, None], a1.shape),
                               jnp.broadcast_to(mu_wag[2][:, None], g1.shape)], axis=1)
    w_r = jnp.concatenate([w_in[:, :3 * rw], lora_w * (1.0 - lora_mu), lora_w * lora_mu],
                          axis=1).astype(BF16)
    qa0 = 3 * rw
    acols = []
    for gi in range(len(ATTN_GROUPS)):
        for part in range(3):
            lo = qa0 + part * ATTN_WIDTH + gi * GROUP_WIDTH
            acols.append(w_in[:, lo:lo + GROUP_WIDTH])
    w_a = jnp.concatenate(acols, axis=1).astype(BF16)
    w_g = w_in[:, qa0 + 3 * ATTN_WIDTH:].astype(BF16)

    x2 = x.reshape(n, d)
    pr, pg, qkv = _inproj(x2, norm1_g.reshape(1, d), scale1, shift1, w_r, w_a, w_g, bsz, seq)

    par = jnp.stack([w0, a0, k_k, k_a, r_k.reshape(rw), ln_w, ln_b,
                     mu_rkv[0], mu_rkv[1], mu_rkv[2]]
                    + [jnp.zeros((rw,), F32)] * 6)
    w_lora = jnp.concatenate([jnp.pad(w2, ((0, 0), (0, 2 * rw))),
                              jnp.pad(a2, ((0, 0), (rw, rw))),
                              jnp.pad(g2, ((0, 0), (2 * rw, 0)))], axis=0).astype(BF16)
    hid = jnp.arange(rw) // HEAD_DIM
    ones_bd = (hid[:, None] == hid[None, :]).astype(BF16)
    y_rwkv = _rwkv(pr, par, w_lora, ones_bd, bsz, seq)

    gains = jnp.pad(jnp.stack([jnp.tile(qn_g, HEADS_PER_GROUP), jnp.tile(kn_g, HEADS_PER_GROUP)]),
                    ((0, 6), (0, 0)))
    bias = jnp.stack([_band_bias(rel_bias, gi) for gi in range(len(ATTN_GROUPS))])
    ghead = jnp.arange(GROUP_WIDTH) // HEAD_DIM
    mean_bd = ((ghead[:, None] == ghead[None, :]).astype(F32) / HEAD_DIM).astype(BF16)
    att = _attention(qkv, bias, gains, mean_bd, bsz, seq)

    rw_t = router_w.T
    rw_hi = rw_t.astype(BF16)
    rw_lo = (rw_t - rw_hi.astype(F32)).astype(BF16)
    rb = jnp.broadcast_to(router_b[:, None], (N_EXPERTS, 128))
    x1, h2, gates, eidx, rank, counts, base = _post(
        x2, y_rwkv, att, pg, gate1, scale2, shift2, norm2_g.reshape(1, d),
        w_br_rwkv.astype(BF16), w_br_attn.astype(BF16), w_out.astype(BF16), rw_hi, rw_lo, rb, seq)

    i32 = jnp.int32
    cnt = counts[:, 0]
    real = (cnt + MOE_ROWS - 1) // MOE_ROWS
    reserved = (cnt + SLAB + ROW_ALIGN + MOE_ROWS - 1) // MOE_ROWS
    bend = jnp.cumsum(reserved)
    bstart = bend - reserved
    nblk = (n * TOP_K) // MOE_ROWS + 2 * N_EXPERTS
    j = jnp.arange(nblk, dtype=i32)
    owner = jnp.clip(jnp.sum((bend[None, :] <= j[:, None]).astype(i32), axis=1), 0, N_EXPERTS - 1)
    local = j - jnp.take(bstart, owner)
    valid = local < jnp.take(real, owner)
    src = lax.cummax(jnp.where(valid, j, 0))
    blk_e = jnp.take(owner, src)
    fill = jnp.logical_or(~valid, local == jnp.take(real, owner) - 1)
    pstart = bstart * MOE_ROWS
    subs = TOKEN_TILE // ROUTE_TILE
    sbase = jnp.transpose(base[:, :, :subs], (0, 2, 1)).reshape(n // ROUTE_TILE, N_EXPERTS)
    scnt = jnp.concatenate([sbase[1:], cnt[None, :]], axis=0) - sbase
    chunks = ((sbase & (ROW_ALIGN - 1)) + scnt + SLAB - 1) // SLAB
    wanted = (jnp.arange(1, EXTRA_PER + 1, dtype=i32)[None, None, :] < chunks[:, :, None])
    wanted = wanted.reshape(n // ROUTE_TILE, N_EXPERTS * EXTRA_PER)
    xlist = jnp.argsort(jnp.logical_not(wanted), axis=1, stable=True)[:, :EXTRA_MAX]
    nx = jnp.sum(wanted.astype(i32), axis=1)
    route = (sbase.reshape(-1).astype(i32), scnt.reshape(-1).astype(i32), pstart.astype(i32),
             nx.astype(i32), xlist.reshape(-1).astype(i32))

    xs = _dispatch(route, fill.astype(i32), eidx, rank, base, h2, nblk)
    w1g, w1l = _split_w1(exp_w1)
    o_rows = _experts(blk_e.astype(i32), src.astype(i32), valid.astype(i32), xs, w1g, w1l,
                      exp_b1[:, None, 0::2], exp_b1[:, None, 1::2],
                      exp_w2.astype(BF16), exp_b2[:, None, :])
    out = _combine(route, eidx, rank, gates, base, x1, gate2, o_rows, seq)
    return out.reshape(bsz, seq, d)


def kernel(x, c, ada_w, ada_b, norm1_g, norm2_g, w_in, rwkv_mu_rkv, rwkv_mu_wag, rwkv_w0, rwkv_w1, rwkv_w2, rwkv_a0, rwkv_a1, rwkv_a2, rwkv_g1, rwkv_g2, rwkv_k_k, rwkv_k_a, rwkv_r_k, rwkv_ln_w, rwkv_ln_b, attn_qn_g, attn_kn_g, rel_bias, w_br_rwkv, w_br_attn, w_out, router_w, router_b, exp_w1, exp_b1, exp_w2, exp_b2):
    per_layer = (ada_w, ada_b, norm1_g, norm2_g, w_in, rwkv_mu_rkv, rwkv_mu_wag, rwkv_w0, rwkv_w1,
                 rwkv_w2, rwkv_a0, rwkv_a1, rwkv_a2, rwkv_g1, rwkv_g2, rwkv_k_k, rwkv_k_a,
                 rwkv_r_k, rwkv_ln_w, rwkv_ln_b, attn_qn_g, attn_kn_g)
    tail = (w_br_rwkv, w_br_attn, w_out, router_w, router_b, exp_w1, exp_b1, exp_w2, exp_b2)
    for layer in range(ada_w.shape[0]):
        head = [p[layer] for p in per_layer]
        rest = [p[layer] for p in tail]
        x = _layer(x, c, *head, rel_bias, *rest)
    return x
```
